```python
import jax, jax.numpy as jnp
from jax import lax
import numpy as np

D_MODEL = 1024
BATCH = 8
SEQ = 8192
DEPTH = 2
DEC_BATCH = 4
DEC_SEQ = 4096
PAST_LEN = 128

MLA_HEADS = 4
MLA_NOPE = 128
MLA_ROPE = 64
MLA_VDIM = 128
MLA_Q_LORA = 256
MLA_KV_LORA = 128
MLA_WIDTH = MLA_HEADS * MLA_VDIM
MLA_SCALE = (MLA_NOPE + MLA_ROPE) ** -0.5
ROPE_BASE = 10000.0
Q_BLOCK = 128
POOL_WINDOWS = (2, 4, 8, 16)
POOL_GROUPS = len(POOL_WINDOWS)
POOL_WIDTH = D_MODEL // 4
POOL_GROUP_DIM = POOL_WIDTH // POOL_GROUPS
GLA_HEADS = 4
GLA_WIDTH = D_MODEL // 4
GLA_KEY = GLA_WIDTH // 2
GLA_DK = GLA_KEY // GLA_HEADS
GLA_DV = GLA_WIDTH // GLA_HEADS
GLA_GATE_RANK = 16
GLA_GATE_NORM = 16.0
GLA_CHUNK = 64

MIX_WIDTH = MLA_WIDTH + POOL_WIDTH + GLA_WIDTH
NORM_EPS = 1e-6

IN_SIZES = (MLA_Q_LORA, MLA_KV_LORA, MLA_ROPE, MLA_WIDTH,
            POOL_WIDTH, POOL_WIDTH,
            GLA_KEY, GLA_KEY, GLA_WIDTH, GLA_GATE_RANK, GLA_GATE_RANK, GLA_WIDTH)
IN_COLS = sum(IN_SIZES)
IN_SPLITS = tuple(int(v) for v in np.cumsum(IN_SIZES)[:-1])

kernel_name = "hymba_mla_pool_gla_encoder"


def rmsnorm(x, g):
    xf = x.astype(jnp.float32)
    y = xf * lax.rsqrt(jnp.mean(xf * xf, axis=-1, keepdims=True) + NORM_EPS)
    return (y * g.astype(jnp.float32)).astype(x.dtype)


def rope_tables(seq):
    inv_freq = 1.0 / (ROPE_BASE ** (jnp.arange(0, MLA_ROPE, 2, dtype=jnp.float32) / MLA_ROPE))
    ang = jnp.arange(seq, dtype=jnp.float32)[:, None] * inv_freq[None, :]
    return jnp.cos(ang), jnp.sin(ang)


def apply_rope(x, cos, sin):
    xf = x.astype(jnp.float32)
    x1, x2 = jnp.split(xf, 2, axis=-1)
    return jnp.concatenate([x1 * cos - x2 * sin, x2 * cos + x1 * sin], axis=-1).astype(x.dtype)


def mla_mixer(c_q, c_kv, k_rope, q_norm_g, w_uq, kv_norm_g, w_ukv):
    B, S, _ = c_q.shape
    q = jnp.einsum('bsr,re->bse', rmsnorm(c_q, q_norm_g), w_uq).reshape(B, S, MLA_HEADS, MLA_NOPE + MLA_ROPE)
    q_nope, q_rope = q[..., :MLA_NOPE], q[..., MLA_NOPE:]
    kv = jnp.einsum('bsr,re->bse', rmsnorm(c_kv, kv_norm_g), w_ukv).reshape(B, S, MLA_HEADS, MLA_NOPE + MLA_VDIM)
    k_nope, v = kv[..., :MLA_NOPE], kv[..., MLA_NOPE:]
    cos, sin = rope_tables(S)
    q_rope = apply_rope(q_rope, cos[None, :, None, :], sin[None, :, None, :])
    k_rope = apply_rope(k_rope, cos[None], sin[None])
    nb = S // Q_BLOCK
    qn_b = q_nope.reshape(B, nb, Q_BLOCK, MLA_HEADS, MLA_NOPE).transpose(1, 0, 2, 3, 4)
    qr_b = q_rope.reshape(B, nb, Q_BLOCK, MLA_HEADS, MLA_ROPE).transpose(1, 0, 2, 3, 4)

    def attend(blk):
        qn, qr = blk
        s = (jnp.einsum('bqhd,bkhd->bhqk', qn, k_nope, preferred_element_type=jnp.float32)
             + jnp.einsum('bqhr,bkr->bhqk', qr, k_rope, preferred_element_type=jnp.float32))
        p = jax.nn.softmax(s * MLA_SCALE, axis=-1)
        return jnp.einsum('bhqk,bkhd->bqhd', p.astype(v.dtype), v)

    o = lax.map(attend, (qn_b, qr_b))
    return o.transpose(1, 0, 2, 3, 4).reshape(B, S, MLA_WIDTH).astype(c_q.dtype)


def pool_mixer(u, w_pool, pool_scale):
    B, S, C = u.shape
    uf = u.astype(jnp.float32)
    cs = jnp.concatenate([jnp.zeros((B, 1, C), jnp.float32), jnp.cumsum(uf, axis=1)], axis=1)
    t = jnp.arange(S)
    outs = []
    for gi, w in enumerate(POOL_WINDOWS):
        lo = jnp.clip(t - w // 2, 0, S)
        hi = jnp.clip(t + w // 2, 0, S)
        csg = cs[..., gi * POOL_GROUP_DIM:(gi + 1) * POOL_GROUP_DIM]
        win_sum = jnp.take(csg, hi, axis=1) - jnp.take(csg, lo, axis=1)
        cnt = (hi - lo).astype(jnp.float32)[None, :, None]
        outs.append(win_sum / cnt)
    pooled = jnp.concatenate(outs, axis=-1) - uf
    pooled = pooled.reshape(B, S, POOL_GROUPS, POOL_GROUP_DIM).astype(u.dtype)
    y = jnp.einsum('bsgc,gcd->bsgd', pooled, w_pool).reshape(B, S, C)
    return (y * pool_scale).astype(u.dtype)


def gla_direction(q, k, v, g):
    B, S, H, DK = q.shape
    DV = v.shape[-1]
    n = S // GLA_CHUNK

    def to_chunks(a):
        return a.reshape(B, n, GLA_CHUNK, H, a.shape[-1]).transpose(1, 0, 3, 2, 4)

    mask = jnp.tril(jnp.ones((GLA_CHUNK, GLA_CHUNK), dtype=bool))[None, None, :, :, None]

    def step(state, inp):
        qc, kc, vc, gc = inp
        b = jnp.cumsum(gc, axis=2)
        o_inter = jnp.einsum('bhcd,bhde->bhce', qc * jnp.exp(b), state)
        decay = jnp.exp(jnp.where(mask, b[:, :, :, None, :] - b[:, :, None, :, :], -jnp.inf))
        a = jnp.einsum('bhid,bhijd,bhjd->bhij', qc, decay, kc)
        o = o_inter + jnp.einsum('bhij,bhje->bhie', a, vc)
        b_last = b[:, :, -1, :]
        state = (jnp.exp(b_last)[..., None] * state
                 + jnp.einsum('bhcd,bhce->bhde', kc * jnp.exp(b_last[:, :, None, :] - b), vc))
        return state, o

    s0 = jnp.zeros((B, H, DK, DV), jnp.float32)
    _, o = lax.scan(step, s0, (to_chunks(q), to_chunks(k), to_chunks(v), to_chunks(g)))
    return o.transpose(1, 0, 3, 2, 4).reshape(B, S, H, DV)


def gla_mixer(q_in, k_in, v_in, lr_fwd, lr_bwd, gk_up_fwd, gk_bias_fwd, gk_up_bwd, gk_bias_bwd, gla_norm_g):
    B, S, _ = q_in.shape
    f32 = jnp.float32
    q = q_in.astype(f32).reshape(B, S, GLA_HEADS, GLA_DK) * (GLA_DK ** -0.5)
    k = k_in.astype(f32).reshape(B, S, GLA_HEADS, GLA_DK)
    v = v_in.astype(f32).reshape(B, S, GLA_HEADS, GLA_DV)

    def log_gate(lr, up, bias):
        z = jnp.einsum('bsr,rk->bsk', lr, up).astype(f32) + bias.astype(f32)
        return (jax.nn.log_sigmoid(z) / GLA_GATE_NORM).reshape(B, S, GLA_HEADS, GLA_DK)

    g_f = log_gate(lr_fwd, gk_up_fwd, gk_bias_fwd)
    g_b = log_gate(lr_bwd, gk_up_bwd, gk_bias_bwd)
    o_f = gla_direction(q, k, v, g_f)
    o_b = gla_direction(q[:, ::-1], k[:, ::-1], v[:, ::-1], g_b[:, ::-1])[:, ::-1]
    o = rmsnorm(o_f + o_b, gla_norm_g)
    return o.reshape(B, S, GLA_WIDTH).astype(q_in.dtype)


def hybrid_layer(x, norm_g, w_in, q_norm_g, w_uq, kv_norm_g, w_ukv, w_pool, pool_scale,
                 gk_up_fwd, gk_bias_fwd, gk_up_bwd, gk_bias_bwd, gla_norm_g, w_out):
    h = rmsnorm(x, norm_g)
    z = jnp.einsum('bsd,de->bse', h, w_in)
    (c_q, c_kv, k_rope, gate_mla, u_pool, gate_pool,
     q_gla, k_gla, v_gla, lr_fwd, lr_bwd, gate_gla) = jnp.split(z, IN_SPLITS, axis=-1)
    o_mla = mla_mixer(c_q, c_kv, k_rope, q_norm_g, w_uq, kv_norm_g, w_ukv)
    o_pool = pool_mixer(u_pool, w_pool, pool_scale)
    o_gla = gla_mixer(q_gla, k_gla, v_gla, lr_fwd, lr_bwd, gk_up_fwd, gk_bias_fwd,
                      gk_up_bwd, gk_bias_bwd, gla_norm_g)
    mixed = jnp.concatenate([o_mla * jax.nn.silu(gate_mla),
                             o_pool * jax.nn.silu(gate_pool),
                             o_gla * jax.nn.silu(gate_gla)], axis=-1).astype(x.dtype)
    return x + jnp.einsum('bse,ed->bsd', mixed, w_out)


def setup_inputs(seed: int = 0) -> dict:
    key = jax.random.key(seed)
    ks = jax.random.split(key, 20)
    f32 = jnp.float32

    def nrm(k, shape, fan_in):
        return jax.random.normal(k, shape, f32) * (fan_in ** -0.5)

    def gain(k, shape):
        return 1.0 + 0.02 * jax.random.normal(k, shape, f32)

    return {
        "x_prompt": jax.random.normal(ks[0], (BATCH, SEQ, D_MODEL), f32),
        "x_sample": jax.random.normal(ks[1], (DEC_BATCH, DEC_SEQ, D_MODEL), f32),
        "norm_g": gain(ks[2], (DEPTH, D_MODEL)),
        "w_in": nrm(ks[3], (DEPTH, D_MODEL, IN_COLS), D_MODEL),
        "q_norm_g": gain(ks[4], (DEPTH, MLA_Q_LORA)),
        "w_uq": nrm(ks[5], (DEPTH, MLA_Q_LORA, MLA_HEADS * (MLA_NOPE + MLA_ROPE)), MLA_Q_LORA),
        "kv_norm_g": gain(ks[6], (DEPTH, MLA_KV_LORA)),
        "w_ukv": nrm(ks[7], (DEPTH, MLA_KV_LORA, MLA_HEADS * (MLA_NOPE + MLA_VDIM)), MLA_KV_LORA),
        "w_pool": nrm(ks[8], (DEPTH, POOL_GROUPS, POOL_GROUP_DIM, POOL_GROUP_DIM), POOL_GROUP_DIM),
        "pool_scale": gain(ks[9], (DEPTH, POOL_WIDTH)),
        "gk_up_fwd": nrm(ks[10], (DEPTH, GLA_GATE_RANK, GLA_KEY), GLA_GATE_RANK),
        "gk_bias_fwd": 0.1 * jax.random.normal(ks[11], (DEPTH, GLA_KEY), f32),
        "gk_up_bwd": nrm(ks[12], (DEPTH, GLA_GATE_RANK, GLA_KEY), GLA_GATE_RANK),
        "gk_bias_bwd": 0.1 * jax.random.normal(ks[13], (DEPTH, GLA_KEY), f32),
        "gla_norm_g": gain(ks[14], (DEPTH, GLA_DV)),
        "w_out": nrm(ks[15], (DEPTH, MIX_WIDTH, D_MODEL), MIX_WIDTH),
        "final_norm_g": gain(ks[16], (D_MODEL,)),
    }


def reference(x_prompt, x_sample, norm_g, w_in, q_norm_g, w_uq, kv_norm_g, w_ukv, w_pool, pool_scale,
              gk_up_fwd, gk_bias_fwd, gk_up_bwd, gk_bias_bwd, gla_norm_g, w_out, final_norm_g):
    def trunk(x):
        for l in range(DEPTH):
            x = hybrid_layer(x, norm_g[l], w_in[l], q_norm_g[l], w_uq[l], kv_norm_g[l], w_ukv[l],
                             w_pool[l], pool_scale[l], gk_up_fwd[l], gk_bias_fwd[l], gk_up_bwd[l],
                             gk_bias_bwd[l], gla_norm_g[l], w_out[l])
        return rmsnorm(x, final_norm_g)

    y_prompt = trunk(x_prompt)
    y_sample = trunk(x_sample)
    return (y_prompt, y_sample)
```

```python
import functools
import math

import jax
import jax.numpy as jnp
import numpy as np
from jax import lax
from jax.experimental import pallas as pl
from jax.experimental.pallas import tpu as pltpu

F32 = jnp.float32
BF16 = jnp.bfloat16

D_MODEL = 1024
NORM_EPS = 1e-6
MLA_HEADS = 4
MLA_NOPE = 128
MLA_ROPE = 64
MLA_VDIM = 128
MLA_Q_LORA = 256
MLA_KV_LORA = 128
MLA_WIDTH = MLA_HEADS * MLA_VDIM
MLA_SCALE = (MLA_NOPE + MLA_ROPE) ** -0.5
ROPE_BASE = 10000.0
QK_WIDTH = 256
POOL_WINDOWS = (2, 4, 8, 16)
POOL_WIDTH = 256
POOL_GROUP_DIM = 64
POOL_HALO = 8
GLA_HEADS = 4
GLA_WIDTH = 256
GLA_KEY = 128
GLA_DK = 32
GLA_DV = 64
GLA_GATE_RANK = 16
GLA_GATE_NORM = 16.0
GLA_SUB = 16
GLA_TILE = 256

MIX_WIDTH = MLA_WIDTH + POOL_WIDTH + GLA_WIDTH
IN_SIZES = (MLA_Q_LORA, MLA_KV_LORA, MLA_ROPE, MLA_WIDTH, POOL_WIDTH, POOL_WIDTH,
            GLA_KEY, GLA_KEY, GLA_WIDTH, GLA_GATE_RANK, GLA_GATE_RANK, GLA_WIDTH)
IN_OFFS = tuple(int(v) for v in np.cumsum((0,) + IN_SIZES))

ZA_WIDTH = MLA_Q_LORA + MLA_KV_LORA + 2 * MLA_ROPE
ZMIX_WIDTH = MLA_WIDTH + 3 * 256
ZGLA_WIDTH = 2 * GLA_KEY + GLA_WIDTH + 128

PROJ_TILE = 512
MIX_TILE = 512
ATTN_TQ = 256
ATTN_TK = 1024
VMEM_LIMIT = 56 * 1024 * 1024


def _rms(x, g):
    return x * lax.rsqrt(jnp.mean(x * x, axis=-1, keepdims=True) + NORM_EPS) * g


def _dot(a, b):
    return jnp.dot(a, b, preferred_element_type=F32)


def _dot_nt(a, b):
    return lax.dot_general(a, b, (((1,), (1,)), ((), ())), preferred_element_type=F32)


def _proj_kernel(x_ref, ng_ref, wa_ref, wmix_ref, wgla_ref, qg_ref, wuq_ref, kvg_ref, wuk_ref,
                 wuvt_ref, tab_ref, q_ref, k_ref, vt_ref, zmix_ref, zgla_ref):
    x = x_ref[0]
    h = _rms(x, ng_ref[...]).astype(BF16)
    za = _dot(h, wa_ref[...])
    zmix_ref[0] = _dot(h, wmix_ref[...])
    zgla_ref[0] = _dot(h, wgla_ref[...])
    tab = tab_ref[...]
    qscale = MLA_SCALE * math.log2(math.e)
    cqn = _rms(za[:, :MLA_Q_LORA], qg_ref[...]).astype(BF16)
    qall = _dot(cqn, wuq_ref[...])
    for hh in range(MLA_HEADS):
        base = hh * QK_WIDTH
        qn = qall[:, base:base + MLA_NOPE] * qscale
        t = qall[:, base + MLA_NOPE:base + QK_WIDTH] * tab
        r = (t + pltpu.roll(t, MLA_ROPE, 1)) * qscale
        q_ref[0, hh, :, 0:MLA_NOPE] = qn.astype(BF16)
        q_ref[0, hh, :, MLA_NOPE:QK_WIDTH] = r.astype(BF16)
    ckvn = _rms(za[:, MLA_Q_LORA:MLA_Q_LORA + MLA_KV_LORA], kvg_ref[...]).astype(BF16)
    kn = _dot(ckvn, wuk_ref[...])
    t = za[:, MLA_Q_LORA + MLA_KV_LORA:] * tab
    kr = t + pltpu.roll(t, MLA_ROPE, 1)
    lane = lax.broadcasted_iota(jnp.int32, kr.shape, 1)
    kr = jnp.where(lane < MLA_ROPE, kr, 0.0).astype(BF16)
    for hh in range(MLA_HEADS):
        k_ref[0, hh, :, 0:MLA_NOPE] = kn[:, hh * MLA_NOPE:(hh + 1) * MLA_NOPE].astype(BF16)
        k_ref[0, hh, :, MLA_NOPE:QK_WIDTH] = kr
    vt = _dot_nt(wuvt_ref[...], ckvn)
    for hh in range(MLA_HEADS):
        vt_ref[0, hh] = vt[hh * MLA_VDIM:(hh + 1) * MLA_VDIM].astype(BF16)


def _proj(x, lw, tab):
    B, S, _ = x.shape
    tm = min(PROJ_TILE, S)
    full = lambda a: pl.BlockSpec(a.shape, lambda b, i: (0,) * a.ndim)
    weights = (lw["norm_g"], lw["wa"], lw["wmix"], lw["wgla"], lw["q_norm_g"], lw["wuq"],
               lw["kv_norm_g"], lw["wuk"], lw["wuvt"])
    return pl.pallas_call(
        _proj_kernel,
        grid=(B, S // tm),
        in_specs=[pl.BlockSpec((1, tm, D_MODEL), lambda b, i: (b, i, 0))]
        + [full(w) for w in weights]
        + [pl.BlockSpec((tm, 128), lambda b, i: (i, 0))],
        out_specs=[
            pl.BlockSpec((1, MLA_HEADS, tm, QK_WIDTH), lambda b, i: (b, 0, i, 0)),
            pl.BlockSpec((1, MLA_HEADS, tm, QK_WIDTH), lambda b, i: (b, 0, i, 0)),
            pl.BlockSpec((1, MLA_HEADS, MLA_VDIM, tm), lambda b, i: (b, 0, 0, i)),
            pl.BlockSpec((1, tm, ZMIX_WIDTH), lambda b, i: (b, i, 0)),
            pl.BlockSpec((1, tm, ZGLA_WIDTH), lambda b, i: (b, i, 0)),
        ],
        out_shape=[
            jax.ShapeDtypeStruct((B, MLA_HEADS, S, QK_WIDTH), BF16),
            jax.ShapeDtypeStruct((B, MLA_HEADS, S, QK_WIDTH), BF16),
            jax.ShapeDtypeStruct((B, MLA_HEADS, MLA_VDIM, S), BF16),
            jax.ShapeDtypeStruct((B, S, ZMIX_WIDTH), F32),
            jax.ShapeDtypeStruct((B, S, ZGLA_WIDTH), F32),
        ],
        compiler_params=pltpu.CompilerParams(
            dimension_semantics=("parallel", "parallel"), vmem_limit_bytes=VMEM_LIMIT),
        name="proj",
    )(x, *weights, tab)


def _attn_kernel(q_ref, k_ref, vt_ref, o_ref, *, tk, nk):
    q = q_ref[0, 0]
    tq = q.shape[0]

    def body(i, carry):
        m, l, acc = carry
        start = pl.multiple_of(i * tk, tk)
        s = _dot_nt(k_ref[0, 0, pl.ds(start, tk), :], q)
        m_new = jnp.maximum(m, jnp.max(s, axis=0, keepdims=True))
        alpha = jnp.exp2(m - m_new)
        p = jnp.exp2(s - m_new)
        l = alpha * l + jnp.sum(p, axis=0, keepdims=True)
        acc = alpha * acc + _dot(vt_ref[0, 0, :, pl.ds(start, tk)], p.astype(BF16))
        return m_new, l, acc

    m0 = jnp.full((1, tq), -jnp.inf, F32)
    l0 = jnp.zeros((1, tq), F32)
    acc0 = jnp.zeros((MLA_VDIM, tq), F32)
    _, l, acc = lax.fori_loop(0, nk, body, (m0, l0, acc0))
    o_ref[0] = (acc / l).T


def _attn(q, k, vt):
    B, H, S, _ = q.shape
    tq = min(ATTN_TQ, S)
    tk = min(ATTN_TK, S)
    return pl.pallas_call(
        functools.partial(_attn_kernel, tk=tk, nk=S // tk),
        grid=(B, H, S // tq),
        in_specs=[
            pl.BlockSpec((1, 1, tq, QK_WIDTH), lambda b, h, i: (b, h, i, 0)),
            pl.BlockSpec((1, 1, S, QK_WIDTH), lambda b, h, i: (b, h, 0, 0)),
            pl.BlockSpec((1, 1, MLA_VDIM, S), lambda b, h, i: (b, h, 0, 0)),
        ],
        out_specs=pl.BlockSpec((1, tq, MLA_VDIM), lambda b, h, i: (b, i, h)),
        out_shape=jax.ShapeDtypeStruct((B, S, MLA_WIDTH), F32),
        compiler_params=pltpu.CompilerParams(
            dimension_semantics=("parallel", "parallel", "parallel"), vmem_limit_bytes=VMEM_LIMIT),
        name="attn",
    )(q, k, vt)


def _split3(x):
    hi = x.astype(BF16)
    r1 = x - hi.astype(F32)
    mid = r1.astype(BF16)
    lo = (r1 - mid.astype(F32)).astype(BF16)
    return jnp.concatenate([hi, mid, lo], axis=1)


def _sum3(y, w):
    return y[:, :w] + y[:, w:2 * w] + y[:, 2 * w:]


def _gla_kernel(z_ref, up_ref, bias_ref, pin_ref, pex_ref, o_ref, s_ref, *, reverse):
    T = z_ref.shape[1]
    nsub = T // GLA_SUB

    @pl.when(pl.program_id(1) == 0)
    def _():
        s_ref[...] = jnp.zeros_like(s_ref)

    z = z_ref[0]
    q = z[:, 0:GLA_KEY] * (GLA_DK ** -0.5)
    k = z[:, GLA_KEY:2 * GLA_KEY]
    v = z[:, 2 * GLA_KEY:2 * GLA_KEY + GLA_WIDTH].astype(BF16)
    lr = z[:, 2 * GLA_KEY + GLA_WIDTH:].astype(BF16)
    zg = _dot(lr, up_ref[...]) + bias_ref[...]
    g = (jnp.minimum(zg, 0.0) - jnp.log1p(jnp.exp(-jnp.abs(zg)))) * (1.0 / GLA_GATE_NORM)
    g3 = _split3(g)
    pin = pin_ref[...]
    b_in = _sum3(_dot(pin, g3), GLA_KEY)
    b_ex = _sum3(_dot(pex_ref[...], g3), GLA_KEY)
    qt = q * jnp.exp(b_in)
    kt = (k * jnp.exp(-b_in)).astype(BF16)
    kp = k * jnp.exp(b_ex)
    dd = jnp.exp(b_in + b_ex)

    pmask = pin > 0
    qhead = lax.broadcasted_iota(jnp.int32, (1, GLA_KEY), 1) // GLA_DK
    vhead = lax.broadcasted_iota(jnp.int32, (1, GLA_WIDTH), 1) // GLA_DV
    o = jnp.zeros((T, GLA_WIDTH), F32)
    for hh in range(GLA_HEADS):
        qh = jnp.where(qhead == hh, qt, 0.0).astype(BF16)
        a = jnp.where(pmask, _dot_nt(qh, kt), 0.0).astype(BF16)
        o = o + _dot(a, jnp.where(vhead == hh, v, jnp.zeros_like(v)))

    kpt = kp.T
    ddt = dd.T
    sub_of_lane = lax.broadcasted_iota(jnp.int32, (1, T), 1) // GLA_SUB
    stack = jnp.concatenate(
        [jnp.where(sub_of_lane == c, kpt, 0.0).astype(BF16) for c in range(nsub)], axis=0)
    inc = _dot(stack, v)
    same_head = (lax.broadcasted_iota(jnp.int32, (GLA_KEY, 1), 0) // GLA_DK) == vhead

    state = s_ref[...]
    inter = [None] * nsub
    order = range(nsub - 1, -1, -1) if reverse else range(nsub)
    for c in order:
        rows = slice(c * GLA_SUB, (c + 1) * GLA_SUB)
        inter[c] = _dot(qt[rows].astype(BF16), state.astype(BF16))
        state = state * ddt[:, c * GLA_SUB:c * GLA_SUB + 1] + jnp.where(
            same_head, inc[c * GLA_KEY:(c + 1) * GLA_KEY], 0.0)
    s_ref[...] = state
    o_ref[0] = o + jnp.concatenate(inter, axis=0)


def _gla_masks(T, reverse):
    t = np.arange(T)
    same = (t[:, None] // GLA_SUB) == (t[None, :] // GLA_SUB)
    before = (t[None, :] >= t[:, None]) if reverse else (t[None, :] <= t[:, None])
    pin = same & before
    pex = same & ~before
    return jnp.asarray(pin, BF16), jnp.asarray(pex, BF16)


def _gla(zgla, up, bias, reverse):
    B, S, _ = zgla.shape
    T = min(GLA_TILE, S)
    n = S // T
    pin, pex = _gla_masks(T, reverse)
    tile = (lambda b, i: (b, n - 1 - i, 0)) if reverse else (lambda b, i: (b, i, 0))
    full = lambda a: pl.BlockSpec(a.shape, lambda b, i: (0,) * a.ndim)
    return pl.pallas_call(
        functools.partial(_gla_kernel, reverse=reverse),
        grid=(B, n),
        in_specs=[pl.BlockSpec((1, T, ZGLA_WIDTH), tile), full(up), full(bias), full(pin), full(pex)],
        out_specs=pl.BlockSpec((1, T, GLA_WIDTH), tile),
        out_shape=jax.ShapeDtypeStruct((B, S, GLA_WIDTH), F32),
        scratch_shapes=[pltpu.VMEM((GLA_KEY, GLA_WIDTH), F32)],
        compiler_params=pltpu.CompilerParams(
            dimension_semantics=("parallel", "arbitrary"), vmem_limit_bytes=VMEM_LIMIT),
        name="gla_bwd" if reverse else "gla_fwd",
    )(zgla, up, bias, pin, pex)


def _silu(x):
    return x / (1.0 + jnp.exp(-x))


def _mix_kernel(x_ref, omla_ref, zmix_ref, uprev_ref, unext_ref, gf_ref, gb_ref, wpool_ref, pscale_ref,
                gng_ref, hsum_ref, wout_ref, fng_ref, out_ref, ubuf, *, seq_len, final):
    tm = x_ref.shape[1]
    i = pl.program_id(1)
    zm = zmix_ref[0]
    gate_mla = zm[:, 0:MLA_WIDTH]
    u = zm[:, MLA_WIDTH:MLA_WIDTH + 256]
    gate_pool = zm[:, MLA_WIDTH + 256:MLA_WIDTH + 512]
    gate_gla = zm[:, MLA_WIDTH + 512:]

    H = POOL_HALO
    ubuf[0:H] = jnp.where(i > 0, uprev_ref[0], 0.0)
    ubuf[H:H + tm] = u
    ubuf[H + tm:] = jnp.where(i < pl.num_programs(1) - 1, unext_ref[0], 0.0)
    pos = i * tm + lax.broadcasted_iota(jnp.int32, (tm, 1), 0)
    lane = lax.broadcasted_iota(jnp.int32, (1, 128), 1)
    pooled = []
    for half, (w_small, w_big) in enumerate(((2, 4), (8, 16))):
        cols = slice(half * 128, (half + 1) * 128)
        acc_small = jnp.zeros((tm, 128), F32)
        acc_big = jnp.zeros((tm, 128), F32)
        for off in range(-(w_big // 2), w_big // 2):
            piece = ubuf[H + off:H + off + tm, cols]
            acc_big = acc_big + piece
            if -(w_small // 2) <= off < w_small // 2:
                acc_small = acc_small + piece
        cnt = lambda w: (jnp.minimum(pos + w // 2, seq_len) - jnp.maximum(pos - w // 2, 0)).astype(F32)
        pooled.append(jnp.where(lane < POOL_GROUP_DIM, acc_small / cnt(w_small), acc_big / cnt(w_big)))
    pooled = jnp.concatenate(pooled, axis=1) - u
    o_pool = _dot(pooled.astype(BF16), wpool_ref[...]) * pscale_ref[...]

    og = gf_ref[0] + gb_ref[0]
    sq = og * og
    sq_hi = sq.astype(BF16)
    sq_lo = (sq - sq_hi.astype(F32)).astype(BF16)
    ms = (_dot(sq_hi, hsum_ref[...]) + _dot(sq_lo, hsum_ref[...])) * (1.0 / GLA_DV)
    o_gla = og * lax.rsqrt(ms + NORM_EPS) * gng_ref[...]

    m_mla = (omla_ref[0] * _silu(gate_mla)).astype(BF16)
    m_pool = (o_pool * _silu(gate_pool)).astype(BF16)
    m_gla = (o_gla * _silu(gate_gla)).astype(BF16)
    y = (x_ref[0] + _dot(m_mla, wout_ref[0:MLA_WIDTH])
         + _dot(m_pool, wout_ref[MLA_WIDTH:MLA_WIDTH + POOL_WIDTH])
         + _dot(m_gla, wout_ref[MLA_WIDTH + POOL_WIDTH:]))
    if final:
        y = _rms(y, fng_ref[...])
    out_ref[0] = y


def _mix(x, omla, zmix, gf, gb, lw, fng, final):
    B, S, _ = x.shape
    tm = min(MIX_TILE, S)
    hb = tm // POOL_HALO
    nhalo = S // POOL_HALO
    ucol = MLA_WIDTH // POOL_WIDTH
    full = lambda a: pl.BlockSpec(a.shape, lambda b, i: (0,) * a.ndim)
    tile = lambda w: pl.BlockSpec((1, tm, w), lambda b, i: (b, i, 0))
    weights = (lw["wpool"], lw["pool_scale"], lw["gla_norm_g"], lw["hsum"], lw["wout"], fng)
    return pl.pallas_call(
        functools.partial(_mix_kernel, seq_len=S, final=final),
        grid=(B, S // tm),
        in_specs=[
            tile(D_MODEL), tile(MLA_WIDTH), tile(ZMIX_WIDTH),
            pl.BlockSpec((1, POOL_HALO, POOL_WIDTH), lambda b, i: (b, jnp.maximum(i * hb - 1, 0), ucol)),
            pl.BlockSpec((1, POOL_HALO, POOL_WIDTH), lambda b, i: (b, jnp.minimum((i + 1) * hb, nhalo - 1), ucol)),
            tile(GLA_WIDTH), tile(GLA_WIDTH),
        ] + [full(w) for w in weights],
        out_specs=tile(D_MODEL),
        out_shape=jax.ShapeDtypeStruct((B, S, D_MODEL), F32),
        scratch_shapes=[pltpu.VMEM((tm + 2 * POOL_HALO, POOL_WIDTH), F32)],
        compiler_params=pltpu.CompilerParams(
            dimension_semantics=("parallel", "parallel"), vmem_limit_bytes=VMEM_LIMIT),
        name="mix",
    )(x, omla, zmix, zmix, zmix, gf, gb, *weights)


def _rope_table(seq):
    inv_freq = 1.0 / (ROPE_BASE ** (jnp.arange(0, MLA_ROPE, 2, dtype=F32) / MLA_ROPE))
    ang = jnp.arange(seq, dtype=F32)[:, None] * inv_freq[None, :]
    cos, sin = jnp.cos(ang), jnp.sin(ang)
    return jnp.concatenate([cos, cos, -sin, sin], axis=1)


def _swap_halves(w):
    half = w.shape[-1] // 2
    return jnp.concatenate([w[..., half:], w[..., :half]], axis=-1)


def _layer_weights(l, norm_g, w_in, q_norm_g, w_uq, kv_norm_g, w_ukv, w_pool, pool_scale,
                   gk_up_fwd, gk_bias_fwd, gk_up_bwd, gk_bias_bwd, gla_norm_g, w_out):
    cols = [w_in[l][:, IN_OFFS[j]:IN_OFFS[j + 1]] for j in range(len(IN_SIZES))]
    (c_q, c_kv, k_rope, gate_mla, u_pool, gate_pool, q_gla, k_gla, v_gla, lr_f, lr_b, gate_gla) = cols
    lr_pad = jnp.zeros((D_MODEL, 128 - 2 * GLA_GATE_RANK), F32)
    wa = jnp.concatenate([c_q, c_kv, k_rope, _swap_halves(k_rope)], axis=1)
    wmix = jnp.concatenate([gate_mla, u_pool, gate_pool, gate_gla], axis=1)
    wgla = jnp.concatenate([q_gla, k_gla, v_gla, lr_f, lr_b, lr_pad], axis=1)
    uq = w_uq[l].reshape(MLA_Q_LORA, MLA_HEADS, MLA_NOPE + MLA_ROPE)
    uq_rope = uq[..., MLA_NOPE:]
    wuq = jnp.concatenate([uq, _swap_halves(uq_rope)], axis=-1).reshape(MLA_Q_LORA, MLA_HEADS * QK_WIDTH)
    ukv = w_ukv[l].reshape(MLA_KV_LORA, MLA_HEADS, MLA_NOPE + MLA_VDIM)
    wuk = ukv[..., :MLA_NOPE].reshape(MLA_KV_LORA, MLA_HEADS * MLA_NOPE)
    wuvt = ukv[..., MLA_NOPE:].reshape(MLA_KV_LORA, MLA_HEADS * MLA_VDIM).T
    wpool = jnp.zeros((POOL_WIDTH, POOL_WIDTH), F32)
    for gi in range(len(POOL_WINDOWS)):
        sl = slice(gi * POOL_GROUP_DIM, (gi + 1) * POOL_GROUP_DIM)
        wpool = wpool.at[sl, sl].set(w_pool[l, gi])
    up_pad = jnp.zeros((128 - GLA_GATE_RANK, GLA_KEY), F32)
    up_f = jnp.concatenate([gk_up_fwd[l], up_pad], axis=0)
    up_b = jnp.concatenate([up_pad[:GLA_GATE_RANK], gk_up_bwd[l], up_pad[GLA_GATE_RANK:]], axis=0)
    head = np.arange(GLA_WIDTH) // GLA_DV
    hsum = jnp.asarray(head[:, None] == head[None, :], BF16)
    return {
        "norm_g": norm_g[l][None, :],
        "wa": wa.astype(BF16), "wmix": wmix.astype(BF16), "wgla": wgla.astype(BF16),
        "q_norm_g": q_norm_g[l][None, :], "wuq": wuq.astype(BF16),
        "kv_norm_g": kv_norm_g[l][None, :], "wuk": wuk.astype(BF16), "wuvt": wuvt.astype(BF16),
        "wpool": wpool.astype(BF16), "pool_scale": pool_scale[l][None, :],
        "up_f": up_f.astype(BF16), "bias_f": gk_bias_fwd[l][None, :],
        "up_b": up_b.astype(BF16), "bias_b": gk_bias_bwd[l][None, :],
        "gla_norm_g": jnp.tile(gla_norm_g[l], GLA_HEADS)[None, :], "hsum": hsum,
        "wout": w_out[l].astype(BF16),
    }


def _trunk(x, layers, fng):
    tab = _rope_table(x.shape[1])
    for l, lw in enumerate(layers):
        q, k, vt, zmix, zgla = _proj(x, lw, tab)
        omla = _attn(q, k, vt)
        gf = _gla(zgla, lw["up_f"], lw["bias_f"], reverse=False)
        gb = _gla(zgla, lw["up_b"], lw["bias_b"], reverse=True)
        x = _mix(x, omla, zmix, gf, gb, lw, fng, final=(l == len(layers) - 1))
    return x


def kernel(x_prompt, x_sample, norm_g, w_in, q_norm_g, w_uq, kv_norm_g, w_ukv, w_pool, pool_scale,
           gk_up_fwd, gk_bias_fwd, gk_up_bwd, gk_bias_bwd, gla_norm_g, w_out, final_norm_g):
    depth = w_in.shape[0]
    layers = [_layer_weights(l, norm_g, w_in, q_norm_g, w_uq, kv_norm_g, w_ukv, w_pool, pool_scale,
                             gk_up_fwd, gk_bias_fwd, gk_up_bwd, gk_bias_bwd, gla_norm_g, w_out)
              for l in range(depth)]
    fng = final_norm_g[None, :]
    return (_trunk(x_prompt, layers, fng), _trunk(x_sample, layers, fng))
```

```python
import functools
import math

import jax
import jax.numpy as jnp
import numpy as np
from jax import lax
from jax.experimental import pallas as pl
from jax.experimental.pallas import tpu as pltpu

F32 = jnp.float32
BF16 = jnp.bfloat16

D_MODEL = 1024
NORM_EPS = 1e-6
MLA_HEADS = 4
MLA_NOPE = 128
MLA_ROPE = 64
MLA_VDIM = 128
MLA_Q_LORA = 256
MLA_KV_LORA = 128
MLA_WIDTH = MLA_HEADS * MLA_VDIM
MLA_SCALE = (MLA_NOPE + MLA_ROPE) ** -0.5
ROPE_BASE = 10000.0
QK_WIDTH = 256
POOL_WINDOWS = (2, 4, 8, 16)
POOL_WIDTH = 256
POOL_GROUP_DIM = 64
POOL_HALO = 8
GLA_HEADS = 4
GLA_WIDTH = 256
GLA_KEY = 128
GLA_DK = 32
GLA_DV = 64
GLA_GATE_RANK = 16
GLA_GATE_NORM = 16.0
GLA_SUB = 16
GLA_TILE = 256

MIX_WIDTH = MLA_WIDTH + POOL_WIDTH + GLA_WIDTH
IN_SIZES = (MLA_Q_LORA, MLA_KV_LORA, MLA_ROPE, MLA_WIDTH, POOL_WIDTH, POOL_WIDTH,
            GLA_KEY, GLA_KEY, GLA_WIDTH, GLA_GATE_RANK, GLA_GATE_RANK, GLA_WIDTH)
IN_OFFS = tuple(int(v) for v in np.cumsum((0,) + IN_SIZES))

ZA_WIDTH = MLA_Q_LORA + MLA_KV_LORA + 2 * MLA_ROPE
ZMIX_WIDTH = MLA_WIDTH + 3 * 256
ZGLA_WIDTH = 2 * GLA_KEY + GLA_WIDTH + 128

PROJ_TILE = 512
MIX_TILE = 512
ATTN_TQ = 512
ATTN_TK = 512
ATTN_VT_ROWS = MLA_VDIM + 16
VMEM_LIMIT = 56 * 1024 * 1024


def _rms(x, g):
    return x * lax.rsqrt(jnp.mean(x * x, axis=-1, keepdims=True) + NORM_EPS) * g


def _dot(a, b):
    return jnp.dot(a, b, preferred_element_type=F32)


def _dot_nt(a, b):
    return lax.dot_general(a, b, (((1,), (1,)), ((), ())), preferred_element_type=F32)


def _proj_kernel(x_ref, ng_ref, wa_ref, wmix_ref, wgla_ref, qg_ref, wuq_ref, kvg_ref, wuk_ref,
                 wuvt_ref, tab_ref, q_ref, k_ref, vt_ref, zmix_ref, zgla_ref):
    x = x_ref[0]
    h = _rms(x, ng_ref[...]).astype(BF16)
    za = _dot(h, wa_ref[...])
    zmix_ref[0] = _dot(h, wmix_ref[...])
    zgla_ref[0] = _dot(h, wgla_ref[...])
    tab = tab_ref[...]
    qscale = MLA_SCALE * math.log2(math.e)
    cqn = _rms(za[:, :MLA_Q_LORA], qg_ref[...]).astype(BF16)
    qall = _dot(cqn, wuq_ref[...])
    for hh in range(MLA_HEADS):
        base = hh * QK_WIDTH
        qn = qall[:, base:base + MLA_NOPE] * qscale
        t = qall[:, base + MLA_NOPE:base + QK_WIDTH] * tab
        r = (t + pltpu.roll(t, MLA_ROPE, 1)) * qscale
        q_ref[0, hh, :, 0:MLA_NOPE] = qn.astype(BF16)
        q_ref[0, hh, :, MLA_NOPE:QK_WIDTH] = r.astype(BF16)
    ckvn = _rms(za[:, MLA_Q_LORA:MLA_Q_LORA + MLA_KV_LORA], kvg_ref[...]).astype(BF16)
    kn = _dot(ckvn, wuk_ref[...])
    t = za[:, MLA_Q_LORA + MLA_KV_LORA:] * tab
    kr = t + pltpu.roll(t, MLA_ROPE, 1)
    lane = lax.broadcasted_iota(jnp.int32, kr.shape, 1)
    kr = jnp.where(lane < MLA_ROPE, kr, 0.0).astype(BF16)
    for hh in range(MLA_HEADS):
        k_ref[0, hh, :, 0:MLA_NOPE] = kn[:, hh * MLA_NOPE:(hh + 1) * MLA_NOPE].astype(BF16)
        k_ref[0, hh, :, MLA_NOPE:QK_WIDTH] = kr
    vt = _dot_nt(wuvt_ref[...], ckvn)
    ones = jnp.ones((ATTN_VT_ROWS - MLA_VDIM, vt.shape[1]), BF16)
    for hh in range(MLA_HEADS):
        vt_ref[0, hh, 0:MLA_VDIM] = vt[hh * MLA_VDIM:(hh + 1) * MLA_VDIM].astype(BF16)
        vt_ref[0, hh, MLA_VDIM:] = ones


def _proj(x, lw, tab):
    B, S, _ = x.shape
    tm = min(PROJ_TILE, S)
    full = lambda a: pl.BlockSpec(a.shape, lambda b, i: (0,) * a.ndim)
    weights = (lw["norm_g"], lw["wa"], lw["wmix"], lw["wgla"], lw["q_norm_g"], lw["wuq"],
               lw["kv_norm_g"], lw["wuk"], lw["wuvt"])
    return pl.pallas_call(
        _proj_kernel,
        grid=(B, S // tm),
        in_specs=[pl.BlockSpec((1, tm, D_MODEL), lambda b, i: (b, i, 0))]
        + [full(w) for w in weights]
        + [pl.BlockSpec((tm, 128), lambda b, i: (i, 0))],
        out_specs=[
            pl.BlockSpec((1, MLA_HEADS, tm, QK_WIDTH), lambda b, i: (b, 0, i, 0)),
            pl.BlockSpec((1, MLA_HEADS, tm, QK_WIDTH), lambda b, i: (b, 0, i, 0)),
            pl.BlockSpec((1, MLA_HEADS, ATTN_VT_ROWS, tm), lambda b, i: (b, 0, 0, i)),
            pl.BlockSpec((1, tm, ZMIX_WIDTH), lambda b, i: (b, i, 0)),
            pl.BlockSpec((1, tm, ZGLA_WIDTH), lambda b, i: (b, i, 0)),
        ],
        out_shape=[
            jax.ShapeDtypeStruct((B, MLA_HEADS, S, QK_WIDTH), BF16),
            jax.ShapeDtypeStruct((B, MLA_HEADS, S, QK_WIDTH), BF16),
            jax.ShapeDtypeStruct((B, MLA_HEADS, ATTN_VT_ROWS, S), BF16),
            jax.ShapeDtypeStruct((B, S, ZMIX_WIDTH), F32),
            jax.ShapeDtypeStruct((B, S, ZGLA_WIDTH), F32),
        ],
        compiler_params=pltpu.CompilerParams(
            dimension_semantics=("parallel", "parallel"), vmem_limit_bytes=VMEM_LIMIT),
        name="proj",
    )(x, *weights, tab)


def _attn_kernel(q_ref, k_ref, vt_ref, o_ref, sa_ref, sb_ref, *, tk, nk):
    q = q_ref[0, 0]
    tq = q.shape[0]

    def scores(i, s_ref):
        start = pl.multiple_of(i * tk, tk)
        s = _dot_nt(k_ref[0, 0, pl.ds(start, tk), :], q)
        s_ref[...] = s
        return jnp.max(s, axis=0, keepdims=True)

    def update(i, s_ref, mx, m, acc):
        start = pl.multiple_of(i * tk, tk)
        m_new = jnp.maximum(m, mx)
        p = jnp.exp2(s_ref[...] - m_new).astype(BF16)
        acc = jnp.exp2(m - m_new) * acc + _dot(vt_ref[0, 0, :, pl.ds(start, tk)], p)
        return m_new, acc

    def body(j, carry):
        m, acc, mx_a = carry
        i = 2 * j
        mx_b = scores(i + 1, sb_ref)
        m, acc = update(i, sa_ref, mx_a, m, acc)
        mx_a = scores(i + 2, sa_ref)
        m, acc = update(i + 1, sb_ref, mx_b, m, acc)
        return m, acc, mx_a

    m = jnp.full((1, tq), -jnp.inf, F32)
    acc = jnp.zeros((ATTN_VT_ROWS, tq), F32)
    mx_a = scores(0, sa_ref)
    if nk > 1:
        m, acc, mx_a = lax.fori_loop(0, nk // 2 - 1, body, (m, acc, mx_a))
        mx_b = scores(nk - 1, sb_ref)
        m, acc = update(nk - 2, sa_ref, mx_a, m, acc)
        m, acc = update(nk - 1, sb_ref, mx_b, m, acc)
    else:
        m, acc = update(0, sa_ref, mx_a, m, acc)
    o_ref[0] = (acc[:MLA_VDIM] / acc[MLA_VDIM:MLA_VDIM + 1]).T


def _attn(q, k, vt):
    B, H, S, _ = q.shape
    tq = min(ATTN_TQ, S)
    tk = min(ATTN_TK, S)
    nk = S // tk
    assert nk == 1 or nk % 2 == 0
    return pl.pallas_call(
        functools.partial(_attn_kernel, tk=tk, nk=nk),
        grid=(B, H, S // tq),
        in_specs=[
            pl.BlockSpec((1, 1, tq, QK_WIDTH), lambda b, h, i: (b, h, i, 0)),
            pl.BlockSpec((1, 1, S, QK_WIDTH), lambda b, h, i: (b, h, 0, 0)),
            pl.BlockSpec((1, 1, ATTN_VT_ROWS, S), lambda b, h, i: (b, h, 0, 0)),
        ],
        out_specs=pl.BlockSpec((1, tq, MLA_VDIM), lambda b, h, i: (b, i, h)),
        out_shape=jax.ShapeDtypeStruct((B, S, MLA_WIDTH), F32),
        scratch_shapes=[pltpu.VMEM((tk, tq), F32), pltpu.VMEM((tk, tq), F32)],
        compiler_params=pltpu.CompilerParams(
            dimension_semantics=("parallel", "parallel", "parallel"), vmem_limit_bytes=VMEM_LIMIT),
        name="attn",
    )(q, k, vt)


def _split3(x):
    hi = x.astype(BF16)
    r1 = x - hi.astype(F32)
    mid = r1.astype(BF16)
    lo = (r1 - mid.astype(F32)).astype(BF16)
    return jnp.concatenate([hi, mid, lo], axis=1)


def _sum3(y, w):
    return y[:, :w] + y[:, w:2 * w] + y[:, 2 * w:]


def _gla_kernel(z_ref, up_ref, bias_ref, pin_ref, pex_ref, o_ref, s_ref, *, reverse):
    T = z_ref.shape[1]
    nsub = T // GLA_SUB

    @pl.when(pl.program_id(1) == 0)
    def _():
        s_ref[...] = jnp.zeros_like(s_ref)

    z = z_ref[0]
    q = z[:, 0:GLA_KEY] * (GLA_DK ** -0.5)
    k = z[:, GLA_KEY:2 * GLA_KEY]
    v = z[:, 2 * GLA_KEY:2 * GLA_KEY + GLA_WIDTH].astype(BF16)
    lr = z[:, 2 * GLA_KEY + GLA_WIDTH:].astype(BF16)
    zg = _dot(lr, up_ref[...]) + bias_ref[...]
    g = (jnp.minimum(zg, 0.0) - jnp.log1p(jnp.exp(-jnp.abs(zg)))) * (1.0 / GLA_GATE_NORM)
    g3 = _split3(g)
    pin = pin_ref[...]
    b_in = _sum3(_dot(pin, g3), GLA_KEY)
    b_ex = _sum3(_dot(pex_ref[...], g3), GLA_KEY)
    qt = q * jnp.exp(b_in)
    kt = (k * jnp.exp(-b_in)).astype(BF16)
    kp = k * jnp.exp(b_ex)
    dd = jnp.exp(b_in + b_ex)

    pmask = pin > 0
    qhead = lax.broadcasted_iota(jnp.int32, (1, GLA_KEY), 1) // GLA_DK
    vhead = lax.broadcasted_iota(jnp.int32, (1, GLA_WIDTH), 1) // GLA_DV
    o = jnp.zeros((T, GLA_WIDTH), F32)
    for hh in range(GLA_HEADS):
        qh = jnp.where(qhead == hh, qt, 0.0).astype(BF16)
        a = jnp.where(pmask, _dot_nt(qh, kt), 0.0).astype(BF16)
        o = o + _dot(a, jnp.where(vhead == hh, v, jnp.zeros_like(v)))

    kpt = kp.T
    ddt = dd.T
    sub_of_lane = lax.broadcasted_iota(jnp.int32, (1, T), 1) // GLA_SUB
    stack = jnp.concatenate(
        [jnp.where(sub_of_lane == c, kpt, 0.0).astype(BF16) for c in range(nsub)], axis=0)
    inc = _dot(stack, v)
    same_head = (lax.broadcasted_iota(jnp.int32, (GLA_KEY, 1), 0) // GLA_DK) == vhead

    state = s_ref[...]
    inter = [None] * nsub
    order = range(nsub - 1, -1, -1) if reverse else range(nsub)
    for c in order:
        rows = slice(c * GLA_SUB, (c + 1) * GLA_SUB)
        inter[c] = _dot(qt[rows].astype(BF16), state.astype(BF16))
        state = state * ddt[:, c * GLA_SUB:c * GLA_SUB + 1] + jnp.where(
            same_head, inc[c * GLA_KEY:(c + 1) * GLA_KEY], 0.0)
    s_ref[...] = state
    o_ref[0] = o + jnp.concatenate(inter, axis=0)


def _gla_masks(T, reverse):
    t = np.arange(T)
    same = (t[:, None] // GLA_SUB) == (t[None, :] // GLA_SUB)
    before = (t[None, :] >= t[:, None]) if reverse else (t[None, :] <= t[:, None])
    pin = same & before
    pex = same & ~before
    return jnp.asarray(pin, BF16), jnp.asarray(pex, BF16)


def _gla(zgla, up, bias, reverse):
    B, S, _ = zgla.shape
    T = min(GLA_TILE, S)
    n = S // T
    pin, pex = _gla_masks(T, reverse)
    tile = (lambda b, i: (b, n - 1 - i, 0)) if reverse else (lambda b, i: (b, i, 0))
    full = lambda a: pl.BlockSpec(a.shape, lambda b, i: (0,) * a.ndim)
    return pl.pallas_call(
        functools.partial(_gla_kernel, reverse=reverse),
        grid=(B, n),
        in_specs=[pl.BlockSpec((1, T, ZGLA_WIDTH), tile), full(up), full(bias), full(pin), full(pex)],
        out_specs=pl.BlockSpec((1, T, GLA_WIDTH), tile),
        out_shape=jax.ShapeDtypeStruct((B, S, GLA_WIDTH), F32),
        scratch_shapes=[pltpu.VMEM((GLA_KEY, GLA_WIDTH), F32)],
        compiler_params=pltpu.CompilerParams(
            dimension_semantics=("parallel", "arbitrary"), vmem_limit_bytes=VMEM_LIMIT),
        name="gla_bwd" if reverse else "gla_fwd",
    )(zgla, up, bias, pin, pex)


def _silu(x):
    return x / (1.0 + jnp.exp(-x))


def _mix_kernel(x_ref, omla_ref, zmix_ref, uprev_ref, unext_ref, gf_ref, gb_ref, wpool_ref, pscale_ref,
                gng_ref, hsum_ref, wout_ref, fng_ref, out_ref, ubuf, *, seq_len, final):
    tm = x_ref.shape[1]
    i = pl.program_id(1)
    zm = zmix_ref[0]
    gate_mla = zm[:, 0:MLA_WIDTH]
    u = zm[:, MLA_WIDTH:MLA_WIDTH + 256]
    gate_pool = zm[:, MLA_WIDTH + 256:MLA_WIDTH + 512]
    gate_gla = zm[:, MLA_WIDTH + 512:]

    H = POOL_HALO
    ubuf[0:H] = jnp.where(i > 0, uprev_ref[0], 0.0)
    ubuf[H:H + tm] = u
    ubuf[H + tm:] = jnp.where(i < pl.num_programs(1) - 1, unext_ref[0], 0.0)
    pos = i * tm + lax.broadcasted_iota(jnp.int32, (tm, 1), 0)
    lane = lax.broadcasted_iota(jnp.int32, (1, 128), 1)
    pooled = []
    for half, (w_small, w_big) in enumerate(((2, 4), (8, 16))):
        cols = slice(half * 128, (half + 1) * 128)
        acc_small = jnp.zeros((tm, 128), F32)
        acc_big = jnp.zeros((tm, 128), F32)
        for off in range(-(w_big // 2), w_big // 2):
            piece = ubuf[H + off:H + off + tm, cols]
            acc_big = acc_big + piece
            if -(w_small // 2) <= off < w_small // 2:
                acc_small = acc_small + piece
        cnt = lambda w: (jnp.minimum(pos + w // 2, seq_len) - jnp.maximum(pos - w // 2, 0)).astype(F32)
        pooled.append(jnp.where(lane < POOL_GROUP_DIM, acc_small / cnt(w_small), acc_big / cnt(w_big)))
    pooled = jnp.concatenate(pooled, axis=1) - u
    o_pool = _dot(pooled.astype(BF16), wpool_ref[...]) * pscale_ref[...]

    og = gf_ref[0] + gb_ref[0]
    sq = og * og
    sq_hi = sq.astype(BF16)
    sq_lo = (sq - sq_hi.astype(F32)).astype(BF16)
    ms = (_dot(sq_hi, hsum_ref[...]) + _dot(sq_lo, hsum_ref[...])) * (1.0 / GLA_DV)
    o_gla = og * lax.rsqrt(ms + NORM_EPS) * gng_ref[...]

    m_mla = (omla_ref[0] * _silu(gate_mla)).astype(BF16)
    m_pool = (o_pool * _silu(gate_pool)).astype(BF16)
    m_gla = (o_gla * _silu(gate_gla)).astype(BF16)
    y = (x_ref[0] + _dot(m_mla, wout_ref[0:MLA_WIDTH])
         + _dot(m_pool, wout_ref[MLA_WIDTH:MLA_WIDTH + POOL_WIDTH])
         + _dot(m_gla, wout_ref[MLA_WIDTH + POOL_WIDTH:]))
    if final:
        y = _rms(y, fng_ref[...])
    out_ref[0] = y


def _mix(x, omla, zmix, gf, gb, lw, fng, final):
    B, S, _ = x.shape
    tm = min(MIX_TILE, S)
    hb = tm // POOL_HALO
    nhalo = S // POOL_HALO
    ucol = MLA_WIDTH // POOL_WIDTH
    full = lambda a: pl.BlockSpec(a.shape, lambda b, i: (0,) * a.ndim)
    tile = lambda w: pl.BlockSpec((1, tm, w), lambda b, i: (b, i, 0))
    weights = (lw["wpool"], lw["pool_scale"], lw["gla_norm_g"], lw["hsum"], lw["wout"], fng)
    return pl.pallas_call(
        functools.partial(_mix_kernel, seq_len=S, final=final),
        grid=(B, S // tm),
        in_specs=[
            tile(D_MODEL), tile(MLA_WIDTH), tile(ZMIX_WIDTH),
            pl.BlockSpec((1, POOL_HALO, POOL_WIDTH), lambda b, i: (b, jnp.maximum(i * hb - 1, 0), ucol)),
            pl.BlockSpec((1, POOL_HALO, POOL_WIDTH), lambda b, i: (b, jnp.minimum((i + 1) * hb, nhalo - 1), ucol)),
            tile(GLA_WIDTH), tile(GLA_WIDTH),
        ] + [full(w) for w in weights],
        out_specs=tile(D_MODEL),
        out_shape=jax.ShapeDtypeStruct((B, S, D_MODEL), F32),
        scratch_shapes=[pltpu.VMEM((tm + 2 * POOL_HALO, POOL_WIDTH), F32)],
        compiler_params=pltpu.CompilerParams(
            dimension_semantics=("parallel", "parallel"), vmem_limit_bytes=VMEM_LIMIT),
        name="mix",
    )(x, omla, zmix, zmix, zmix, gf, gb, *weights)


def _rope_table(seq):
    inv_freq = 1.0 / (ROPE_BASE ** (jnp.arange(0, MLA_ROPE, 2, dtype=F32) / MLA_ROPE))
    ang = jnp.arange(seq, dtype=F32)[:, None] * inv_freq[None, :]
    cos, sin = jnp.cos(ang), jnp.sin(ang)
    return jnp.concatenate([cos, cos, -sin, sin], axis=1)


def _swap_halves(w):
    half = w.shape[-1] // 2
    return jnp.concatenate([w[..., half:], w[..., :half]], axis=-1)


def _layer_weights(l, norm_g, w_in, q_norm_g, w_uq, kv_norm_g, w_ukv, w_pool, pool_scale,
                   gk_up_fwd, gk_bias_fwd, gk_up_bwd, gk_bias_bwd, gla_norm_g, w_out):
    cols = [w_in[l][:, IN_OFFS[j]:IN_OFFS[j + 1]] for j in range(len(IN_SIZES))]
    (c_q, c_kv, k_rope, gate_mla, u_pool, gate_pool, q_gla, k_gla, v_gla, lr_f, lr_b, gate_gla) = cols
    lr_pad = jnp.zeros((D_MODEL, 128 - 2 * GLA_GATE_RANK), F32)
    wa = jnp.concatenate([c_q, c_kv, k_rope, _swap_halves(k_rope)], axis=1)
    wmix = jnp.concatenate([gate_mla, u_pool, gate_pool, gate_gla], axis=1)
    wgla = jnp.concatenate([q_gla, k_gla, v_gla, lr_f, lr_b, lr_pad], axis=1)
    uq = w_uq[l].reshape(MLA_Q_LORA, MLA_HEADS, MLA_NOPE + MLA_ROPE)
    uq_rope = uq[..., MLA_NOPE:]
    wuq = jnp.concatenate([uq, _swap_halves(uq_rope)], axis=-1).reshape(MLA_Q_LORA, MLA_HEADS * QK_WIDTH)
    ukv = w_ukv[l].reshape(MLA_KV_LORA, MLA_HEADS, MLA_NOPE + MLA_VDIM)
    wuk = ukv[..., :MLA_NOPE].reshape(MLA_KV_LORA, MLA_HEADS * MLA_NOPE)
    wuvt = ukv[..., MLA_NOPE:].reshape(MLA_KV_LORA, MLA_HEADS * MLA_VDIM).T
    wpool = jnp.zeros((POOL_WIDTH, POOL_WIDTH), F32)
    for gi in range(len(POOL_WINDOWS)):
        sl = slice(gi * POOL_GROUP_DIM, (gi + 1) * POOL_GROUP_DIM)
        wpool = wpool.at[sl, sl].set(w_pool[l, gi])
    up_pad = jnp.zeros((128 - GLA_GATE_RANK, GLA_KEY), F32)
    up_f = jnp.concatenate([gk_up_fwd[l], up_pad], axis=0)
    up_b = jnp.concatenate([up_pad[:GLA_GATE_RANK], gk_up_bwd[l], up_pad[GLA_GATE_RANK:]], axis=0)
    head = np.arange(GLA_WIDTH) // GLA_DV
    hsum = jnp.asarray(head[:, None] == head[None, :], BF16)
    return {
        "norm_g": norm_g[l][None, :],
        "wa": wa.astype(BF16), "wmix": wmix.astype(BF16), "wgla": wgla.astype(BF16),
        "q_norm_g": q_norm_g[l][None, :], "wuq": wuq.astype(BF16),
        "kv_norm_g": kv_norm_g[l][None, :], "wuk": wuk.astype(BF16), "wuvt": wuvt.astype(BF16),
        "wpool": wpool.astype(BF16), "pool_scale": pool_scale[l][None, :],
        "up_f": up_f.astype(BF16), "bias_f": gk_bias_fwd[l][None, :],
        "up_b": up_b.astype(BF16), "bias_b": gk_bias_bwd[l][None, :],
        "gla_norm_g": jnp.tile(gla_norm_g[l], GLA_HEADS)[None, :], "hsum": hsum,
        "wout": w_out[l].astype(BF16),
    }


def _trunk(x, layers, fng):
    tab = _rope_table(x.shape[1])
    for l, lw in enumerate(layers):
        q, k, vt, zmix, zgla = _proj(x, lw, tab)
        omla = _attn(q, k, vt)
        gf = _gla(zgla, lw["up_f"], lw["bias_f"], reverse=False)
        gb = _gla(zgla, lw["up_b"], lw["bias_b"], reverse=True)
        x = _mix(x, omla, zmix, gf, gb, lw, fng, final=(l == len(layers) - 1))
    return x


def kernel(x_prompt, x_sample, norm_g, w_in, q_norm_g, w_uq, kv_norm_g, w_ukv, w_pool, pool_scale,
           gk_up_fwd, gk_bias_fwd, gk_up_bwd, gk_bias_bwd, gla_norm_g, w_out, final_norm_g):
    depth = w_in.shape[0]
    layers = [_layer_weights(l, norm_g, w_in, q_norm_g, w_uq, kv_norm_g, w_ukv, w_pool, pool_scale,
                             gk_up_fwd, gk_bias_fwd, gk_up_bwd, gk_bias_bwd, gla_norm_g, w_out)
              for l in range(depth)]
    fng = final_norm_g[None, :]
    return (_trunk(x_prompt, layers, fng), _trunk(x_sample, layers, fng))
```

```python
import functools
import math

import jax
import jax.numpy as jnp
import numpy as np
from jax import lax
from jax.experimental import pallas as pl
from jax.experimental.pallas import tpu as pltpu

F32 = jnp.float32
BF16 = jnp.bfloat16

D_MODEL = 1024
NORM_EPS = 1e-6
MLA_HEADS = 4
MLA_NOPE = 128
MLA_ROPE = 64
MLA_VDIM = 128
MLA_Q_LORA = 256
MLA_KV_LORA = 128
MLA_WIDTH = MLA_HEADS * MLA_VDIM
MLA_SCALE = (MLA_NOPE + MLA_ROPE) ** -0.5
ROPE_BASE = 10000.0
QK_WIDTH = 256
POOL_WINDOWS = (2, 4, 8, 16)
POOL_WIDTH = 256
POOL_GROUP_DIM = 64
POOL_HALO = 8
GLA_HEADS = 4
GLA_WIDTH = 256
GLA_KEY = 128
GLA_DK = 32
GLA_DV = 64
GLA_GATE_RANK = 16
GLA_GATE_NORM = 16.0
GLA_SUB = 16
GLA_TILE = 256

MIX_WIDTH = MLA_WIDTH + POOL_WIDTH + GLA_WIDTH
IN_SIZES = (MLA_Q_LORA, MLA_KV_LORA, MLA_ROPE, MLA_WIDTH, POOL_WIDTH, POOL_WIDTH,
            GLA_KEY, GLA_KEY, GLA_WIDTH, GLA_GATE_RANK, GLA_GATE_RANK, GLA_WIDTH)
IN_OFFS = tuple(int(v) for v in np.cumsum((0,) + IN_SIZES))

ZA_WIDTH = MLA_Q_LORA + MLA_KV_LORA + 2 * MLA_ROPE
ZMIX_WIDTH = MLA_WIDTH + 3 * 256
ZGLA_WIDTH = 2 * GLA_KEY + GLA_WIDTH + 128

PROJ_TILE = 512
MIX_TILE = 512
ATTN_TQ = 512
ATTN_TK = 1024
ATTN_VT_ROWS = MLA_VDIM + 16
VMEM_LIMIT = 56 * 1024 * 1024


def _rms(x, g):
    return x * lax.rsqrt(jnp.mean(x * x, axis=-1, keepdims=True) + NORM_EPS) * g


def _dot(a, b):
    return jnp.dot(a, b, preferred_element_type=F32)


def _dot_nt(a, b):
    return lax.dot_general(a, b, (((1,), (1,)), ((), ())), preferred_element_type=F32)


def _proj_kernel(x_ref, ng_ref, wa_ref, wmix_ref, wgla_ref, qg_ref, wuq_ref, kvg_ref, wuk_ref,
                 wuvt_ref, tab_ref, q_ref, k_ref, vt_ref, zmix_ref, zgla_ref):
    x = x_ref[0]
    h = _rms(x, ng_ref[...]).astype(BF16)
    za = _dot(h, wa_ref[...])
    zmix_ref[0] = _dot(h, wmix_ref[...])
    zgla_ref[0] = _dot(h, wgla_ref[...])
    tab = tab_ref[...]
    qscale = MLA_SCALE * math.log2(math.e)
    cqn = _rms(za[:, :MLA_Q_LORA], qg_ref[...]).astype(BF16)
    qall = _dot(cqn, wuq_ref[...])
    for hh in range(MLA_HEADS):
        base = hh * QK_WIDTH
        qn = qall[:, base:base + MLA_NOPE] * qscale
        t = qall[:, base + MLA_NOPE:base + QK_WIDTH] * tab
        r = (t + pltpu.roll(t, MLA_ROPE, 1)) * qscale
        q_ref[0, hh, :, 0:MLA_NOPE] = qn.astype(BF16)
        q_ref[0, hh, :, MLA_NOPE:QK_WIDTH] = r.astype(BF16)
    ckvn = _rms(za[:, MLA_Q_LORA:MLA_Q_LORA + MLA_KV_LORA], kvg_ref[...]).astype(BF16)
    kn = _dot(ckvn, wuk_ref[...])
    t = za[:, MLA_Q_LORA + MLA_KV_LORA:] * tab
    kr = t + pltpu.roll(t, MLA_ROPE, 1)
    lane = lax.broadcasted_iota(jnp.int32, kr.shape, 1)
    kr = jnp.where(lane < MLA_ROPE, kr, 0.0).astype(BF16)
    for hh in range(MLA_HEADS):
        k_ref[0, hh, :, 0:MLA_NOPE] = kn[:, hh * MLA_NOPE:(hh + 1) * MLA_NOPE].astype(BF16)
        k_ref[0, hh, :, MLA_NOPE:QK_WIDTH] = kr
    vt = _dot_nt(wuvt_ref[...], ckvn)
    ones = jnp.ones((ATTN_VT_ROWS - MLA_VDIM, vt.shape[1]), BF16)
    for hh in range(MLA_HEADS):
        vt_ref[0, hh, 0:MLA_VDIM] = vt[hh * MLA_VDIM:(hh + 1) * MLA_VDIM].astype(BF16)
        vt_ref[0, hh, MLA_VDIM:] = ones


def _proj(x, lw, tab):
    B, S, _ = x.shape
    tm = min(PROJ_TILE, S)
    full = lambda a: pl.BlockSpec(a.shape, lambda b, i: (0,) * a.ndim)
    weights = (lw["norm_g"], lw["wa"], lw["wmix"], lw["wgla"], lw["q_norm_g"], lw["wuq"],
               lw["kv_norm_g"], lw["wuk"], lw["wuvt"])
    return pl.pallas_call(
        _proj_kernel,
        grid=(B, S // tm),
        in_specs=[pl.BlockSpec((1, tm, D_MODEL), lambda b, i: (b, i, 0))]
        + [full(w) for w in weights]
        + [pl.BlockSpec((tm, 128), lambda b, i: (i, 0))],
        out_specs=[
            pl.BlockSpec((1, MLA_HEADS, tm, QK_WIDTH), lambda b, i: (b, 0, i, 0)),
            pl.BlockSpec((1, MLA_HEADS, tm, QK_WIDTH), lambda b, i: (b, 0, i, 0)),
            pl.BlockSpec((1, MLA_HEADS, ATTN_VT_ROWS, tm), lambda b, i: (b, 0, 0, i)),
            pl.BlockSpec((1, tm, ZMIX_WIDTH), lambda b, i: (b, i, 0)),
            pl.BlockSpec((1, tm, ZGLA_WIDTH), lambda b, i: (b, i, 0)),
        ],
        out_shape=[
            jax.ShapeDtypeStruct((B, MLA_HEADS, S, QK_WIDTH), BF16),
            jax.ShapeDtypeStruct((B, MLA_HEADS, S, QK_WIDTH), BF16),
            jax.ShapeDtypeStruct((B, MLA_HEADS, ATTN_VT_ROWS, S), BF16),
            jax.ShapeDtypeStruct((B, S, ZMIX_WIDTH), F32),
            jax.ShapeDtypeStruct((B, S, ZGLA_WIDTH), F32),
        ],
        compiler_params=pltpu.CompilerParams(
            dimension_semantics=("parallel", "parallel"), vmem_limit_bytes=VMEM_LIMIT),
        name="proj",
    )(x, *weights, tab)


def _attn_kernel(q_ref, k_ref, vt_ref, o_ref, sa_ref, sb_ref, *, tk, nk):
    q = q_ref[0, 0]
    tq = q.shape[0]

    def scores(i, s_ref):
        start = pl.multiple_of(i * tk, tk)
        s = _dot_nt(k_ref[0, 0, pl.ds(start, tk), :], q)
        s_ref[...] = s
        return jnp.max(s, axis=0, keepdims=True)

    def update(i, s_ref, mx, m, acc):
        start = pl.multiple_of(i * tk, tk)
        m_new = jnp.maximum(m, mx)
        p = jnp.exp2(s_ref[...] - m_new).astype(BF16)
        acc = jnp.exp2(m - m_new) * acc + _dot(vt_ref[0, 0, :, pl.ds(start, tk)], p)
        return m_new, acc

    def body(j, carry):
        m, acc, mx_a = carry
        i = 2 * j
        mx_b = scores(i + 1, sb_ref)
        m, acc = update(i, sa_ref, mx_a, m, acc)
        mx_a = scores(i + 2, sa_ref)
        m, acc = update(i + 1, sb_ref, mx_b, m, acc)
        return m, acc, mx_a

    m = jnp.full((1, tq), -jnp.inf, F32)
    acc = jnp.zeros((ATTN_VT_ROWS, tq), F32)
    mx_a = scores(0, sa_ref)
    if nk > 1:
        m, acc, mx_a = lax.fori_loop(0, nk // 2 - 1, body, (m, acc, mx_a))
        mx_b = scores(nk - 1, sb_ref)
        m, acc = update(nk - 2, sa_ref, mx_a, m, acc)
        m, acc = update(nk - 1, sb_ref, mx_b, m, acc)
    else:
        m, acc = update(0, sa_ref, mx_a, m, acc)
    o_ref[0] = (acc[:MLA_VDIM] / acc[MLA_VDIM:MLA_VDIM + 1]).T


def _attn(q, k, vt):
    B, H, S, _ = q.shape
    tq = min(ATTN_TQ, S)
    tk = min(ATTN_TK, S)
    nk = S // tk
    assert nk == 1 or nk % 2 == 0
    return pl.pallas_call(
        functools.partial(_attn_kernel, tk=tk, nk=nk),
        grid=(B, H, S // tq),
        in_specs=[
            pl.BlockSpec((1, 1, tq, QK_WIDTH), lambda b, h, i: (b, h, i, 0)),
            pl.BlockSpec((1, 1, S, QK_WIDTH), lambda b, h, i: (b, h, 0, 0)),
            pl.BlockSpec((1, 1, ATTN_VT_ROWS, S), lambda b, h, i: (b, h, 0, 0)),
        ],
        out_specs=pl.BlockSpec((1, tq, MLA_VDIM), lambda b, h, i: (b, i, h)),
        out_shape=jax.ShapeDtypeStruct((B, S, MLA_WIDTH), F32),
        scratch_shapes=[pltpu.VMEM((tk, tq), F32), pltpu.VMEM((tk, tq), F32)],
        compiler_params=pltpu.CompilerParams(
            dimension_semantics=("parallel", "parallel", "parallel"), vmem_limit_bytes=VMEM_LIMIT),
        name="attn",
    )(q, k, vt)


def _split3(x):
    hi = x.astype(BF16)
    r1 = x - hi.astype(F32)
    mid = r1.astype(BF16)
    lo = (r1 - mid.astype(F32)).astype(BF16)
    return jnp.concatenate([hi, mid, lo], axis=1)


def _sum3(y, w):
    return y[:, :w] + y[:, w:2 * w] + y[:, 2 * w:]


def _gla_kernel(z_ref, up_ref, bias_ref, pin_ref, pex_ref, o_ref, s_ref, *, reverse):
    T = z_ref.shape[1]
    nsub = T // GLA_SUB

    @pl.when(pl.program_id(1) == 0)
    def _():
        s_ref[...] = jnp.zeros_like(s_ref)

    z = z_ref[0]
    q = z[:, 0:GLA_KEY] * (GLA_DK ** -0.5)
    k = z[:, GLA_KEY:2 * GLA_KEY]
    v = z[:, 2 * GLA_KEY:2 * GLA_KEY + GLA_WIDTH].astype(BF16)
    lr = z[:, 2 * GLA_KEY + GLA_WIDTH:].astype(BF16)
    zg = _dot(lr, up_ref[...]) + bias_ref[...]
    g = (jnp.minimum(zg, 0.0) - jnp.log1p(jnp.exp(-jnp.abs(zg)))) * (1.0 / GLA_GATE_NORM)
    g3 = _split3(g)
    pin = pin_ref[...]
    b_in = _sum3(_dot(pin, g3), GLA_KEY)
    b_ex = _sum3(_dot(pex_ref[...], g3), GLA_KEY)
    qt = q * jnp.exp(b_in)
    kt = (k * jnp.exp(-b_in)).astype(BF16)
    kp = k * jnp.exp(b_ex)
    dd = jnp.exp(b_in + b_ex)

    pmask = pin > 0
    qhead = lax.broadcasted_iota(jnp.int32, (1, GLA_KEY), 1) // GLA_DK
    vhead = lax.broadcasted_iota(jnp.int32, (1, GLA_WIDTH), 1) // GLA_DV
    o = jnp.zeros((T, GLA_WIDTH), F32)
    for hh in range(GLA_HEADS):
        qh = jnp.where(qhead == hh, qt, 0.0).astype(BF16)
        a = jnp.where(pmask, _dot_nt(qh, kt), 0.0).astype(BF16)
        o = o + _dot(a, jnp.where(vhead == hh, v, jnp.zeros_like(v)))

    kpt = kp.T
    ddt = dd.T
    sub_of_lane = lax.broadcasted_iota(jnp.int32, (1, T), 1) // GLA_SUB
    stack = jnp.concatenate(
        [jnp.where(sub_of_lane == c, kpt, 0.0).astype(BF16) for c in range(nsub)], axis=0)
    inc = _dot(stack, v)
    same_head = (lax.broadcasted_iota(jnp.int32, (GLA_KEY, 1), 0) // GLA_DK) == vhead

    state = s_ref[...]
    inter = [None] * nsub
    order = range(nsub - 1, -1, -1) if reverse else range(nsub)
    for c in order:
        rows = slice(c * GLA_SUB, (c + 1) * GLA_SUB)
        inter[c] = _dot(qt[rows].astype(BF16), state.astype(BF16))
        state = state * ddt[:, c * GLA_SUB:c * GLA_SUB + 1] + jnp.where(
            same_head, inc[c * GLA_KEY:(c + 1) * GLA_KEY], 0.0)
    s_ref[...] = state
    o_ref[0] = o + jnp.concatenate(inter, axis=0)


def _gla_masks(T, reverse):
    t = np.arange(T)
    same = (t[:, None] // GLA_SUB) == (t[None, :] // GLA_SUB)
    before = (t[None, :] >= t[:, None]) if reverse else (t[None, :] <= t[:, None])
    pin = same & before
    pex = same & ~before
    return jnp.asarray(pin, BF16), jnp.asarray(pex, BF16)


def _gla(zgla, up, bias, reverse):
    B, S, _ = zgla.shape
    T = min(GLA_TILE, S)
    n = S // T
    pin, pex = _gla_masks(T, reverse)
    tile = (lambda b, i: (b, n - 1 - i, 0)) if reverse else (lambda b, i: (b, i, 0))
    full = lambda a: pl.BlockSpec(a.shape, lambda b, i: (0,) * a.ndim)
    return pl.pallas_call(
        functools.partial(_gla_kernel, reverse=reverse),
        grid=(B, n),
        in_specs=[pl.BlockSpec((1, T, ZGLA_WIDTH), tile), full(up), full(bias), full(pin), full(pex)],
        out_specs=pl.BlockSpec((1, T, GLA_WIDTH), tile),
        out_shape=jax.ShapeDtypeStruct((B, S, GLA_WIDTH), F32),
        scratch_shapes=[pltpu.VMEM((GLA_KEY, GLA_WIDTH), F32)],
        compiler_params=pltpu.CompilerParams(
            dimension_semantics=("parallel", "arbitrary"), vmem_limit_bytes=VMEM_LIMIT),
        name="gla_bwd" if reverse else "gla_fwd",
    )(zgla, up, bias, pin, pex)


def _silu(x):
    return x / (1.0 + jnp.exp(-x))


def _mix_kernel(x_ref, omla_ref, zmix_ref, uprev_ref, unext_ref, gf_ref, gb_ref, wpool_ref, pscale_ref,
                gng_ref, hsum_ref, wout_ref, fng_ref, out_ref, ubuf, *, seq_len, final):
    tm = x_ref.shape[1]
    i = pl.program_id(1)
    zm = zmix_ref[0]
    gate_mla = zm[:, 0:MLA_WIDTH]
    u = zm[:, MLA_WIDTH:MLA_WIDTH + 256]
    gate_pool = zm[:, MLA_WIDTH + 256:MLA_WIDTH + 512]
    gate_gla = zm[:, MLA_WIDTH + 512:]

    H = POOL_HALO
    ubuf[0:H] = jnp.where(i > 0, uprev_ref[0], 0.0)
    ubuf[H:H + tm] = u
    ubuf[H + tm:] = jnp.where(i < pl.num_programs(1) - 1, unext_ref[0], 0.0)
    pos = i * tm + lax.broadcasted_iota(jnp.int32, (tm, 1), 0)
    lane = lax.broadcasted_iota(jnp.int32, (1, 128), 1)
    pooled = []
    for half, (w_small, w_big) in enumerate(((2, 4), (8, 16))):
        cols = slice(half * 128, (half + 1) * 128)
        acc_small = jnp.zeros((tm, 128), F32)
        acc_big = jnp.zeros((tm, 128), F32)
        for off in range(-(w_big // 2), w_big // 2):
            piece = ubuf[H + off:H + off + tm, cols]
            acc_big = acc_big + piece
            if -(w_small // 2) <= off < w_small // 2:
                acc_small = acc_small + piece
        cnt = lambda w: (jnp.minimum(pos + w // 2, seq_len) - jnp.maximum(pos - w // 2, 0)).astype(F32)
        pooled.append(jnp.where(lane < POOL_GROUP_DIM, acc_small / cnt(w_small), acc_big / cnt(w_big)))
    pooled = jnp.concatenate(pooled, axis=1) - u
    o_pool = _dot(pooled.astype(BF16), wpool_ref[...]) * pscale_ref[...]

    og = gf_ref[0] + gb_ref[0]
    sq = og * og
    sq_hi = sq.astype(BF16)
    sq_lo = (sq - sq_hi.astype(F32)).astype(BF16)
    ms = (_dot(sq_hi, hsum_ref[...]) + _dot(sq_lo, hsum_ref[...])) * (1.0 / GLA_DV)
    o_gla = og * lax.rsqrt(ms + NORM_EPS) * gng_ref[...]

    m_mla = (omla_ref[0] * _silu(gate_mla)).astype(BF16)
    m_pool = (o_pool * _silu(gate_pool)).astype(BF16)
    m_gla = (o_gla * _silu(gate_gla)).astype(BF16)
    y = (x_ref[0] + _dot(m_mla, wout_ref[0:MLA_WIDTH])
         + _dot(m_pool, wout_ref[MLA_WIDTH:MLA_WIDTH + POOL_WIDTH])
         + _dot(m_gla, wout_ref[MLA_WIDTH + POOL_WIDTH:]))
    if final:
        y = _rms(y, fng_ref[...])
    out_ref[0] = y


def _mix(x, omla, zmix, gf, gb, lw, fng, final):
    B, S, _ = x.shape
    tm = min(MIX_TILE, S)
    hb = tm // POOL_HALO
    nhalo = S // POOL_HALO
    ucol = MLA_WIDTH // POOL_WIDTH
    full = lambda a: pl.BlockSpec(a.shape, lambda b, i: (0,) * a.ndim)
    tile = lambda w: pl.BlockSpec((1, tm, w), lambda b, i: (b, i, 0))
    weights = (lw["wpool"], lw["pool_scale"], lw["gla_norm_g"], lw["hsum"], lw["wout"], fng)
    return pl.pallas_call(
        functools.partial(_mix_kernel, seq_len=S, final=final),
        grid=(B, S // tm),
        in_specs=[
            tile(D_MODEL), tile(MLA_WIDTH), tile(ZMIX_WIDTH),
            pl.BlockSpec((1, POOL_HALO, POOL_WIDTH), lambda b, i: (b, jnp.maximum(i * hb - 1, 0), ucol)),
            pl.BlockSpec((1, POOL_HALO, POOL_WIDTH), lambda b, i: (b, jnp.minimum((i + 1) * hb, nhalo - 1), ucol)),
            tile(GLA_WIDTH), tile(GLA_WIDTH),
        ] + [full(w) for w in weights],
        out_specs=tile(D_MODEL),
        out_shape=jax.ShapeDtypeStruct((B, S, D_MODEL), F32),
        scratch_shapes=[pltpu.VMEM((tm + 2 * POOL_HALO, POOL_WIDTH), F32)],
        compiler_params=pltpu.CompilerParams(
            dimension_semantics=("parallel", "parallel"), vmem_limit_bytes=VMEM_LIMIT),
        name="mix",
    )(x, omla, zmix, zmix, zmix, gf, gb, *weights)


def _rope_table(seq):
    inv_freq = 1.0 / (ROPE_BASE ** (jnp.arange(0, MLA_ROPE, 2, dtype=F32) / MLA_ROPE))
    ang = jnp.arange(seq, dtype=F32)[:, None] * inv_freq[None, :]
    cos, sin = jnp.cos(ang), jnp.sin(ang)
    return jnp.concatenate([cos, cos, -sin, sin], axis=1)


def _swap_halves(w):
    half = w.shape[-1] // 2
    return jnp.concatenate([w[..., half:], w[..., :half]], axis=-1)


def _layer_weights(l, norm_g, w_in, q_norm_g, w_uq, kv_norm_g, w_ukv, w_pool, pool_scale,
                   gk_up_fwd, gk_bias_fwd, gk_up_bwd, gk_bias_bwd, gla_norm_g, w_out):
    cols = [w_in[l][:, IN_OFFS[j]:IN_OFFS[j + 1]] for j in range(len(IN_SIZES))]
    (c_q, c_kv, k_rope, gate_mla, u_pool, gate_pool, q_gla, k_gla, v_gla, lr_f, lr_b, gate_gla) = cols
    lr_pad = jnp.zeros((D_MODEL, 128 - 2 * GLA_GATE_RANK), F32)
    wa = jnp.concatenate([c_q, c_kv, k_rope, _swap_halves(k_rope)], axis=1)
    wmix = jnp.concatenate([gate_mla, u_pool, gate_pool, gate_gla], axis=1)
    wgla = jnp.concatenate([q_gla, k_gla, v_gla, lr_f, lr_b, lr_pad], axis=1)
    uq = w_uq[l].reshape(MLA_Q_LORA, MLA_HEADS, MLA_NOPE + MLA_ROPE)
    uq_rope = uq[..., MLA_NOPE:]
    wuq = jnp.concatenate([uq, _swap_halves(uq_rope)], axis=-1).reshape(MLA_Q_LORA, MLA_HEADS * QK_WIDTH)
    ukv = w_ukv[l].reshape(MLA_KV_LORA, MLA_HEADS, MLA_NOPE + MLA_VDIM)
    wuk = ukv[..., :MLA_NOPE].reshape(MLA_KV_LORA, MLA_HEADS * MLA_NOPE)
    wuvt = ukv[..., MLA_NOPE:].reshape(MLA_KV_LORA, MLA_HEADS * MLA_VDIM).T
    wpool = jnp.zeros((POOL_WIDTH, POOL_WIDTH), F32)
    for gi in range(len(POOL_WINDOWS)):
        sl = slice(gi * POOL_GROUP_DIM, (gi + 1) * POOL_GROUP_DIM)
        wpool = wpool.at[sl, sl].set(w_pool[l, gi])
    up_pad = jnp.zeros((128 - GLA_GATE_RANK, GLA_KEY), F32)
    up_f = jnp.concatenate([gk_up_fwd[l], up_pad], axis=0)
    up_b = jnp.concatenate([up_pad[:GLA_GATE_RANK], gk_up_bwd[l], up_pad[GLA_GATE_RANK:]], axis=0)
    head = np.arange(GLA_WIDTH) // GLA_DV
    hsum = jnp.asarray(head[:, None] == head[None, :], BF16)
    return {
        "norm_g": norm_g[l][None, :],
        "wa": wa.astype(BF16), "wmix": wmix.astype(BF16), "wgla": wgla.astype(BF16),
        "q_norm_g": q_norm_g[l][None, :], "wuq": wuq.astype(BF16),
        "kv_norm_g": kv_norm_g[l][None, :], "wuk": wuk.astype(BF16), "wuvt": wuvt.astype(BF16),
        "wpool": wpool.astype(BF16), "pool_scale": pool_scale[l][None, :],
        "up_f": up_f.astype(BF16), "bias_f": gk_bias_fwd[l][None, :],
        "up_b": up_b.astype(BF16), "bias_b": gk_bias_bwd[l][None, :],
        "gla_norm_g": jnp.tile(gla_norm_g[l], GLA_HEADS)[None, :], "hsum": hsum,
        "wout": w_out[l].astype(BF16),
    }


def _trunk(x, layers, fng):
    tab = _rope_table(x.shape[1])
    for l, lw in enumerate(layers):
        q, k, vt, zmix, zgla = _proj(x, lw, tab)
        omla = _attn(q, k, vt)
        gf = _gla(zgla, lw["up_f"], lw["bias_f"], reverse=False)
        gb = _gla(zgla, lw["up_b"], lw["bias_b"], reverse=True)
        x = _mix(x, omla, zmix, gf, gb, lw, fng, final=(l == len(layers) - 1))
    return x


def kernel(x_prompt, x_sample, norm_g, w_in, q_norm_g, w_uq, kv_norm_g, w_ukv, w_pool, pool_scale,
           gk_up_fwd, gk_bias_fwd, gk_up_bwd, gk_bias_bwd, gla_norm_g, w_out, final_norm_g):
    depth = w_in.shape[0]
    layers = [_layer_weights(l, norm_g, w_in, q_norm_g, w_uq, kv_norm_g, w_ukv, w_pool, pool_scale,
                             gk_up_fwd, gk_bias_fwd, gk_up_bwd, gk_bias_bwd, gla_norm_g, w_out)
              for l in range(depth)]
    fng = final_norm_g[None, :]
    return (_trunk(x_prompt, layers, fng), _trunk(x_sample, layers, fng))
```

```python
import functools
import math

import jax
import jax.numpy as jnp
import numpy as np
from jax import lax
from jax.experimental import pallas as pl
from jax.experimental.pallas import tpu as pltpu

F32 = jnp.float32
BF16 = jnp.bfloat16

D_MODEL = 1024
NORM_EPS = 1e-6
MLA_HEADS = 4
MLA_NOPE = 128
MLA_ROPE = 64
MLA_VDIM = 128
MLA_Q_LORA = 256
MLA_KV_LORA = 128
MLA_WIDTH = MLA_HEADS * MLA_VDIM
MLA_SCALE = (MLA_NOPE + MLA_ROPE) ** -0.5
ROPE_BASE = 10000.0
QK_WIDTH = 256
POOL_WINDOWS = (2, 4, 8, 16)
POOL_WIDTH = 256
POOL_GROUP_DIM = 64
POOL_HALO = 8
GLA_HEADS = 4
GLA_WIDTH = 256
GLA_KEY = 128
GLA_DK = 32
GLA_DV = 64
GLA_GATE_RANK = 16
GLA_GATE_NORM = 16.0
GLA_SUB = 16
GLA_TILE = 256

MIX_WIDTH = MLA_WIDTH + POOL_WIDTH + GLA_WIDTH
IN_SIZES = (MLA_Q_LORA, MLA_KV_LORA, MLA_ROPE, MLA_WIDTH, POOL_WIDTH, POOL_WIDTH,
            GLA_KEY, GLA_KEY, GLA_WIDTH, GLA_GATE_RANK, GLA_GATE_RANK, GLA_WIDTH)
IN_OFFS = tuple(int(v) for v in np.cumsum((0,) + IN_SIZES))

ZA_WIDTH = MLA_Q_LORA + MLA_KV_LORA + 2 * MLA_ROPE
ZGATE_WIDTH = MLA_WIDTH + POOL_WIDTH + GLA_WIDTH
ZGLA_WIDTH = 2 * GLA_KEY + GLA_WIDTH + 128

PROJ_TILE = 512
MIX_TILE = 512
ATTN_TQ = 512
ATTN_TK = 1024
ATTN_VT_ROWS = MLA_VDIM + 16
VMEM_LIMIT = 56 * 1024 * 1024


def _rms(x, g):
    return x * lax.rsqrt(jnp.mean(x * x, axis=-1, keepdims=True) + NORM_EPS) * g


def _dot(a, b):
    return jnp.dot(a, b, preferred_element_type=F32)


def _dot_nt(a, b):
    return lax.dot_general(a, b, (((1,), (1,)), ((), ())), preferred_element_type=F32)


def _proj_kernel(x_ref, ng_ref, wa_ref, wgate_ref, wpu_ref, wgla_ref, qg_ref, wuq_ref, kvg_ref, wuk_ref,
                 wuvt_ref, tab_ref, q_ref, k_ref, vt_ref, zgate_ref, upool_ref, zgla_ref):
    x = x_ref[0]
    h = _rms(x, ng_ref[...]).astype(BF16)
    za = _dot(h, wa_ref[...])
    zgate_ref[0] = _dot(h, wgate_ref[...]).astype(BF16)
    upool_ref[0] = _dot(h, wpu_ref[...])
    zgla_ref[0] = _dot(h, wgla_ref[...]).astype(BF16)
    tab = tab_ref[...]
    qscale = MLA_SCALE * math.log2(math.e)
    cqn = _rms(za[:, :MLA_Q_LORA], qg_ref[...]).astype(BF16)
    qall = _dot(cqn, wuq_ref[...])
    for hh in range(MLA_HEADS):
        base = hh * QK_WIDTH
        qn = qall[:, base:base + MLA_NOPE] * qscale
        t = qall[:, base + MLA_NOPE:base + QK_WIDTH] * tab
        r = (t + pltpu.roll(t, MLA_ROPE, 1)) * qscale
        q_ref[0, hh, :, 0:MLA_NOPE] = qn.astype(BF16)
        q_ref[0, hh, :, MLA_NOPE:QK_WIDTH] = r.astype(BF16)
    ckvn = _rms(za[:, MLA_Q_LORA:MLA_Q_LORA + MLA_KV_LORA], kvg_ref[...]).astype(BF16)
    kn = _dot(ckvn, wuk_ref[...])
    t = za[:, MLA_Q_LORA + MLA_KV_LORA:] * tab
    kr = t + pltpu.roll(t, MLA_ROPE, 1)
    lane = lax.broadcasted_iota(jnp.int32, kr.shape, 1)
    kr = jnp.where(lane < MLA_ROPE, kr, 0.0).astype(BF16)
    for hh in range(MLA_HEADS):
        k_ref[0, hh, :, 0:MLA_NOPE] = kn[:, hh * MLA_NOPE:(hh + 1) * MLA_NOPE].astype(BF16)
        k_ref[0, hh, :, MLA_NOPE:QK_WIDTH] = kr
    vt = _dot_nt(wuvt_ref[...], ckvn)
    ones = jnp.ones((ATTN_VT_ROWS - MLA_VDIM, vt.shape[1]), BF16)
    for hh in range(MLA_HEADS):
        vt_ref[0, hh, 0:MLA_VDIM] = vt[hh * MLA_VDIM:(hh + 1) * MLA_VDIM].astype(BF16)
        vt_ref[0, hh, MLA_VDIM:] = ones


def _proj(x, lw, tab):
    B, S, _ = x.shape
    tm = min(PROJ_TILE, S)
    full = lambda a: pl.BlockSpec(a.shape, lambda b, i: (0,) * a.ndim)
    weights = (lw["norm_g"], lw["wa"], lw["wgate"], lw["wpu"], lw["wgla"], lw["q_norm_g"], lw["wuq"],
               lw["kv_norm_g"], lw["wuk"], lw["wuvt"])
    return pl.pallas_call(
        _proj_kernel,
        grid=(B, S // tm),
        in_specs=[pl.BlockSpec((1, tm, D_MODEL), lambda b, i: (b, i, 0))]
        + [full(w) for w in weights]
        + [pl.BlockSpec((tm, 128), lambda b, i: (i, 0))],
        out_specs=[
            pl.BlockSpec((1, MLA_HEADS, tm, QK_WIDTH), lambda b, i: (b, 0, i, 0)),
            pl.BlockSpec((1, MLA_HEADS, tm, QK_WIDTH), lambda b, i: (b, 0, i, 0)),
            pl.BlockSpec((1, MLA_HEADS, ATTN_VT_ROWS, tm), lambda b, i: (b, 0, 0, i)),
            pl.BlockSpec((1, tm, ZGATE_WIDTH), lambda b, i: (b, i, 0)),
            pl.BlockSpec((1, tm, POOL_WIDTH), lambda b, i: (b, i, 0)),
            pl.BlockSpec((1, tm, ZGLA_WIDTH), lambda b, i: (b, i, 0)),
        ],
        out_shape=[
            jax.ShapeDtypeStruct((B, MLA_HEADS, S, QK_WIDTH), BF16),
            jax.ShapeDtypeStruct((B, MLA_HEADS, S, QK_WIDTH), BF16),
            jax.ShapeDtypeStruct((B, MLA_HEADS, ATTN_VT_ROWS, S), BF16),
            jax.ShapeDtypeStruct((B, S, ZGATE_WIDTH), BF16),
            jax.ShapeDtypeStruct((B, S, POOL_WIDTH), F32),
            jax.ShapeDtypeStruct((B, S, ZGLA_WIDTH), BF16),
        ],
        compiler_params=pltpu.CompilerParams(
            dimension_semantics=("parallel", "parallel"), vmem_limit_bytes=VMEM_LIMIT),
        name="proj",
    )(x, *weights, tab)


def _attn_kernel(q_ref, k_ref, vt_ref, o_ref, sa_ref, sb_ref, *, tk, nk):
    q = q_ref[0, 0]
    tq = q.shape[0]

    def scores(i, s_ref):
        start = pl.multiple_of(i * tk, tk)
        s = _dot_nt(k_ref[0, 0, pl.ds(start, tk), :], q)
        s_ref[...] = s
        return jnp.max(s, axis=0, keepdims=True)

    def update(i, s_ref, mx, m, acc):
        start = pl.multiple_of(i * tk, tk)
        m_new = jnp.maximum(m, mx)
        p = jnp.exp2(s_ref[...] - m_new).astype(BF16)
        acc = jnp.exp2(m - m_new) * acc + _dot(vt_ref[0, 0, :, pl.ds(start, tk)], p)
        return m_new, acc

    def body(j, carry):
        m, acc, mx_a = carry
        i = 2 * j
        mx_b = scores(i + 1, sb_ref)
        m, acc = update(i, sa_ref, mx_a, m, acc)
        mx_a = scores(i + 2, sa_ref)
        m, acc = update(i + 1, sb_ref, mx_b, m, acc)
        return m, acc, mx_a

    m = jnp.full((1, tq), -jnp.inf, F32)
    acc = jnp.zeros((ATTN_VT_ROWS, tq), F32)
    mx_a = scores(0, sa_ref)
    if nk > 1:
        m, acc, mx_a = lax.fori_loop(0, nk // 2 - 1, body, (m, acc, mx_a))
        mx_b = scores(nk - 1, sb_ref)
        m, acc = update(nk - 2, sa_ref, mx_a, m, acc)
        m, acc = update(nk - 1, sb_ref, mx_b, m, acc)
    else:
        m, acc = update(0, sa_ref, mx_a, m, acc)
    o_ref[0] = (acc[:MLA_VDIM] / acc[MLA_VDIM:MLA_VDIM + 1]).T.astype(o_ref.dtype)


def _attn(q, k, vt):
    B, H, S, _ = q.shape
    tq = min(ATTN_TQ, S)
    tk = min(ATTN_TK, S)
    nk = S // tk
    assert nk == 1 or nk % 2 == 0
    return pl.pallas_call(
        functools.partial(_attn_kernel, tk=tk, nk=nk),
        grid=(B, H, S // tq),
        in_specs=[
            pl.BlockSpec((1, 1, tq, QK_WIDTH), lambda b, h, i: (b, h, i, 0)),
            pl.BlockSpec((1, 1, S, QK_WIDTH), lambda b, h, i: (b, h, 0, 0)),
            pl.BlockSpec((1, 1, ATTN_VT_ROWS, S), lambda b, h, i: (b, h, 0, 0)),
        ],
        out_specs=pl.BlockSpec((1, tq, MLA_VDIM), lambda b, h, i: (b, i, h)),
        out_shape=jax.ShapeDtypeStruct((B, S, MLA_WIDTH), BF16),
        scratch_shapes=[pltpu.VMEM((tk, tq), F32), pltpu.VMEM((tk, tq), F32)],
        compiler_params=pltpu.CompilerParams(
            dimension_semantics=("parallel", "parallel", "parallel"), vmem_limit_bytes=VMEM_LIMIT),
        name="attn",
    )(q, k, vt)


def _split2(x):
    hi = x.astype(BF16)
    lo = (x - hi.astype(F32)).astype(BF16)
    return jnp.concatenate([hi, lo], axis=1)


def _sum2(y, w):
    return y[:, :w] + y[:, w:]


def _gla_tile(z, up, bias, pin, pex, state, reverse):
    T = z.shape[0]
    nsub = T // GLA_SUB
    q = z[:, 0:GLA_KEY].astype(F32) * (GLA_DK ** -0.5)
    k = z[:, GLA_KEY:2 * GLA_KEY].astype(F32)
    v = z[:, 2 * GLA_KEY:2 * GLA_KEY + GLA_WIDTH]
    lr = z[:, 2 * GLA_KEY + GLA_WIDTH:]
    zg = _dot(lr, up) + bias
    yield
    g = (jnp.minimum(zg, 0.0) - jnp.log1p(jnp.exp(-jnp.abs(zg)))) * (1.0 / GLA_GATE_NORM)
    g2 = _split2(g)
    b_in = _sum2(_dot(pin, g2), GLA_KEY)
    yield
    b_ex = _sum2(_dot(pex, g2), GLA_KEY)
    yield
    qt = q * jnp.exp(b_in)
    kt = (k * jnp.exp(-b_in)).astype(BF16)
    kp = k * jnp.exp(b_ex)
    dd = jnp.exp(b_in + b_ex)

    pmask = pin > 0
    qhead = lax.broadcasted_iota(jnp.int32, (1, GLA_KEY), 1) // GLA_DK
    vhead = lax.broadcasted_iota(jnp.int32, (1, GLA_WIDTH), 1) // GLA_DV
    a = []
    for hh in range(GLA_HEADS):
        qh = jnp.where(qhead == hh, qt, 0.0).astype(BF16)
        a.append(jnp.where(pmask, _dot_nt(qh, kt), 0.0).astype(BF16))
        yield

    kpt = kp.T
    ddt = dd.T
    sub_of_lane = lax.broadcasted_iota(jnp.int32, (1, T), 1) // GLA_SUB
    stack = jnp.concatenate(
        [jnp.where(sub_of_lane == c, kpt, 0.0).astype(BF16) for c in range(nsub)], axis=0)
    inc = _dot(stack, v)
    yield
    same_head = (lax.broadcasted_iota(jnp.int32, (GLA_KEY, 1), 0) // GLA_DK) == vhead

    o = jnp.zeros((T, GLA_WIDTH), F32)
    for hh in range(GLA_HEADS):
        o = o + _dot(a[hh], jnp.where(vhead == hh, v, jnp.zeros_like(v)))
        yield

    inter = [None] * nsub
    order = range(nsub - 1, -1, -1) if reverse else range(nsub)
    for c in order:
        rows = slice(c * GLA_SUB, (c + 1) * GLA_SUB)
        inter[c] = _dot(qt[rows].astype(BF16), state.astype(BF16))
        state = state * ddt[:, c * GLA_SUB:c * GLA_SUB + 1] + jnp.where(
            same_head, inc[c * GLA_KEY:(c + 1) * GLA_KEY], 0.0)
        yield
    return o + jnp.concatenate(inter, axis=0), state


def _lockstep(gens):
    results = [None] * len(gens)
    live = list(range(len(gens)))
    while live:
        for idx in list(live):
            try:
                next(gens[idx])
            except StopIteration as done:
                results[idx] = done.value
                live.remove(idx)
    return results


def _gla_kernel(zf_ref, zb_ref, upf_ref, biasf_ref, upb_ref, biasb_ref, pinf_ref, pexf_ref, pinb_ref, pexb_ref,
                of_ref, ob_ref, sf_ref, sb_ref):
    @pl.when(pl.program_id(1) == 0)
    def _():
        sf_ref[...] = jnp.zeros_like(sf_ref)
        sb_ref[...] = jnp.zeros_like(sb_ref)

    (of, sf), (ob, sb) = _lockstep([
        _gla_tile(zf_ref[0], upf_ref[...], biasf_ref[...], pinf_ref[...], pexf_ref[...], sf_ref[...], False),
        _gla_tile(zb_ref[0], upb_ref[...], biasb_ref[...], pinb_ref[...], pexb_ref[...], sb_ref[...], True)])
    of_ref[0] = of.astype(of_ref.dtype)
    ob_ref[0] = ob.astype(ob_ref.dtype)
    sf_ref[...] = sf
    sb_ref[...] = sb


def _gla_masks(T, reverse):
    t = np.arange(T)
    same = (t[:, None] // GLA_SUB) == (t[None, :] // GLA_SUB)
    before = (t[None, :] >= t[:, None]) if reverse else (t[None, :] <= t[:, None])
    pin = same & before
    pex = same & ~before
    return jnp.asarray(pin, BF16), jnp.asarray(pex, BF16)


def _gla(zgla, lw):
    B, S, _ = zgla.shape
    T = min(GLA_TILE, S)
    n = S // T
    consts = (lw["up_f"], lw["bias_f"], lw["up_b"], lw["bias_b"]) + _gla_masks(T, False) + _gla_masks(T, True)
    fwd = lambda b, i: (b, i, 0)
    bwd = lambda b, i: (b, n - 1 - i, 0)
    full = lambda a: pl.BlockSpec(a.shape, lambda b, i: (0,) * a.ndim)
    return pl.pallas_call(
        _gla_kernel,
        grid=(B, n),
        in_specs=[pl.BlockSpec((1, T, ZGLA_WIDTH), fwd), pl.BlockSpec((1, T, ZGLA_WIDTH), bwd)]
        + [full(c) for c in consts],
        out_specs=[pl.BlockSpec((1, T, GLA_WIDTH), fwd), pl.BlockSpec((1, T, GLA_WIDTH), bwd)],
        out_shape=[jax.ShapeDtypeStruct((B, S, GLA_WIDTH), BF16)] * 2,
        scratch_shapes=[pltpu.VMEM((GLA_KEY, GLA_WIDTH), F32)] * 2,
        compiler_params=pltpu.CompilerParams(
            dimension_semantics=("parallel", "arbitrary"), vmem_limit_bytes=VMEM_LIMIT),
        name="gla",
    )(zgla, zgla, *consts)


def _silu(x):
    return x / (1.0 + jnp.exp(-x))


def _mix_kernel(x_ref, omla_ref, zgate_ref, u_ref, uprev_ref, unext_ref, gf_ref, gb_ref, wpool_ref, pscale_ref,
                gng_ref, hsum_ref, wout_ref, fng_ref, out_ref, ubuf, *, seq_len, final):
    tm = x_ref.shape[1]
    i = pl.program_id(1)
    zg = zgate_ref[0].astype(F32)
    gate_mla = zg[:, 0:MLA_WIDTH]
    gate_pool = zg[:, MLA_WIDTH:MLA_WIDTH + POOL_WIDTH]
    gate_gla = zg[:, MLA_WIDTH + POOL_WIDTH:]
    u = u_ref[0]

    H = POOL_HALO
    ubuf[0:H] = jnp.where(i > 0, uprev_ref[0], 0.0)
    ubuf[H:H + tm] = u
    ubuf[H + tm:] = jnp.where(i < pl.num_programs(1) - 1, unext_ref[0], 0.0)
    pos = i * tm + lax.broadcasted_iota(jnp.int32, (tm, 1), 0)
    lane = lax.broadcasted_iota(jnp.int32, (1, 128), 1)
    pooled = []
    for half, (w_small, w_big) in enumerate(((2, 4), (8, 16))):
        cols = slice(half * 128, (half + 1) * 128)
        acc_small = jnp.zeros((tm, 128), F32)
        acc_big = jnp.zeros((tm, 128), F32)
        for off in range(-(w_big // 2), w_big // 2):
            piece = ubuf[H + off:H + off + tm, cols]
            acc_big = acc_big + piece
            if -(w_small // 2) <= off < w_small // 2:
                acc_small = acc_small + piece
        cnt = lambda w: (jnp.minimum(pos + w // 2, seq_len) - jnp.maximum(pos - w // 2, 0)).astype(F32)
        pooled.append(jnp.where(lane < POOL_GROUP_DIM, acc_small / cnt(w_small), acc_big / cnt(w_big)))
    pooled = jnp.concatenate(pooled, axis=1) - u
    o_pool = _dot(pooled.astype(BF16), wpool_ref[...]) * pscale_ref[...]

    og = gf_ref[0].astype(F32) + gb_ref[0].astype(F32)
    sq = og * og
    sq_hi = sq.astype(BF16)
    sq_lo = (sq - sq_hi.astype(F32)).astype(BF16)
    ms = (_dot(sq_hi, hsum_ref[...]) + _dot(sq_lo, hsum_ref[...])) * (1.0 / GLA_DV)
    o_gla = og * lax.rsqrt(ms + NORM_EPS) * gng_ref[...]

    m_mla = (omla_ref[0].astype(F32) * _silu(gate_mla)).astype(BF16)
    m_pool = (o_pool * _silu(gate_pool)).astype(BF16)
    m_gla = (o_gla * _silu(gate_gla)).astype(BF16)
    y = (x_ref[0] + _dot(m_mla, wout_ref[0:MLA_WIDTH])
         + _dot(m_pool, wout_ref[MLA_WIDTH:MLA_WIDTH + POOL_WIDTH])
         + _dot(m_gla, wout_ref[MLA_WIDTH + POOL_WIDTH:]))
    if final:
        y = _rms(y, fng_ref[...])
    out_ref[0] = y


def _mix(x, omla, zgate, upool, gf, gb, lw, fng, final):
    B, S, _ = x.shape
    tm = min(MIX_TILE, S)
    hb = tm // POOL_HALO
    nhalo = S // POOL_HALO
    full = lambda a: pl.BlockSpec(a.shape, lambda b, i: (0,) * a.ndim)
    tile = lambda w: pl.BlockSpec((1, tm, w), lambda b, i: (b, i, 0))
    weights = (lw["wpool"], lw["pool_scale"], lw["gla_norm_g"], lw["hsum"], lw["wout"], fng)
    return pl.pallas_call(
        functools.partial(_mix_kernel, seq_len=S, final=final),
        grid=(B, S // tm),
        in_specs=[
            tile(D_MODEL), tile(MLA_WIDTH), tile(ZGATE_WIDTH), tile(POOL_WIDTH),
            pl.BlockSpec((1, POOL_HALO, POOL_WIDTH), lambda b, i: (b, jnp.maximum(i * hb - 1, 0), 0)),
            pl.BlockSpec((1, POOL_HALO, POOL_WIDTH), lambda b, i: (b, jnp.minimum((i + 1) * hb, nhalo - 1), 0)),
            tile(GLA_WIDTH), tile(GLA_WIDTH),
        ] + [full(w) for w in weights],
        out_specs=tile(D_MODEL),
        out_shape=jax.ShapeDtypeStruct((B, S, D_MODEL), F32),
        scratch_shapes=[pltpu.VMEM((tm + 2 * POOL_HALO, POOL_WIDTH), F32)],
        compiler_params=pltpu.CompilerParams(
            dimension_semantics=("parallel", "parallel"), vmem_limit_bytes=VMEM_LIMIT),
        name="mix",
    )(x, omla, zgate, upool, upool, upool, gf, gb, *weights)


def _rope_table(seq):
    inv_freq = 1.0 / (ROPE_BASE ** (jnp.arange(0, MLA_ROPE, 2, dtype=F32) / MLA_ROPE))
    ang = jnp.arange(seq, dtype=F32)[:, None] * inv_freq[None, :]
    cos, sin = jnp.cos(ang), jnp.sin(ang)
    return jnp.concatenate([cos, cos, -sin, sin], axis=1)


def _swap_halves(w):
    half = w.shape[-1] // 2
    return jnp.concatenate([w[..., half:], w[..., :half]], axis=-1)


def _layer_weights(l, norm_g, w_in, q_norm_g, w_uq, kv_norm_g, w_ukv, w_pool, pool_scale,
                   gk_up_fwd, gk_bias_fwd, gk_up_bwd, gk_bias_bwd, gla_norm_g, w_out):
    cols = [w_in[l][:, IN_OFFS[j]:IN_OFFS[j + 1]] for j in range(len(IN_SIZES))]
    (c_q, c_kv, k_rope, gate_mla, u_pool, gate_pool, q_gla, k_gla, v_gla, lr_f, lr_b, gate_gla) = cols
    lr_pad = jnp.zeros((D_MODEL, 128 - 2 * GLA_GATE_RANK), F32)
    wa = jnp.concatenate([c_q, c_kv, k_rope, _swap_halves(k_rope)], axis=1)
    wgate = jnp.concatenate([gate_mla, gate_pool, gate_gla], axis=1)
    wgla = jnp.concatenate([q_gla, k_gla, v_gla, lr_f, lr_b, lr_pad], axis=1)
    uq = w_uq[l].reshape(MLA_Q_LORA, MLA_HEADS, MLA_NOPE + MLA_ROPE)
    uq_rope = uq[..., MLA_NOPE:]
    wuq = jnp.concatenate([uq, _swap_halves(uq_rope)], axis=-1).reshape(MLA_Q_LORA, MLA_HEADS * QK_WIDTH)
    ukv = w_ukv[l].reshape(MLA_KV_LORA, MLA_HEADS, MLA_NOPE + MLA_VDIM)
    wuk = ukv[..., :MLA_NOPE].reshape(MLA_KV_LORA, MLA_HEADS * MLA_NOPE)
    wuvt = ukv[..., MLA_NOPE:].reshape(MLA_KV_LORA, MLA_HEADS * MLA_VDIM).T
    wpool = jnp.zeros((POOL_WIDTH, POOL_WIDTH), F32)
    for gi in range(len(POOL_WINDOWS)):
        sl = slice(gi * POOL_GROUP_DIM, (gi + 1) * POOL_GROUP_DIM)
        wpool = wpool.at[sl, sl].set(w_pool[l, gi])
    up_pad = jnp.zeros((128 - GLA_GATE_RANK, GLA_KEY), F32)
    up_f = jnp.concatenate([gk_up_fwd[l], up_pad], axis=0)
    up_b = jnp.concatenate([up_pad[:GLA_GATE_RANK], gk_up_bwd[l], up_pad[GLA_GATE_RANK:]], axis=0)
    head = np.arange(GLA_WIDTH) // GLA_DV
    hsum = jnp.asarray(head[:, None] == head[None, :], BF16)
    return {
        "norm_g": norm_g[l][None, :],
        "wa": wa.astype(BF16), "wgate": wgate.astype(BF16), "wpu": u_pool.astype(BF16),"wgla": wgla.astype(BF16),
        "q_norm_g": q_norm_g[l][None, :], "wuq": wuq.astype(BF16),
        "kv_norm_g": kv_norm_g[l][None, :], "wuk": wuk.astype(BF16), "wuvt": wuvt.astype(BF16),
        "wpool": wpool.astype(BF16), "pool_scale": pool_scale[l][None, :],
        "up_f": up_f.astype(BF16), "bias_f": gk_bias_fwd[l][None, :],
        "up_b": up_b.astype(BF16), "bias_b": gk_bias_bwd[l][None, :],
        "gla_norm_g": jnp.tile(gla_norm_g[l], GLA_HEADS)[None, :], "hsum": hsum,
        "wout": w_out[l].astype(BF16),
    }


def _trunk(x, layers, fng):
    tab = _rope_table(x.shape[1])
    for l, lw in enumerate(layers):
        q, k, vt, zgate, upool, zgla = _proj(x, lw, tab)
        omla = _attn(q, k, vt)
        gf, gb = _gla(zgla, lw)
        x = _mix(x, omla, zgate, upool, gf, gb, lw, fng, final=(l == len(layers) - 1))
    return x


def kernel(x_prompt, x_sample, norm_g, w_in, q_norm_g, w_uq, kv_norm_g, w_ukv, w_pool, pool_scale,
           gk_up_fwd, gk_bias_fwd, gk_up_bwd, gk_bias_bwd, gla_norm_g, w_out, final_norm_g):
    depth = w_in.shape[0]
    layers = [_layer_weights(l, norm_g, w_in, q_norm_g, w_uq, kv_norm_g, w_ukv, w_pool, pool_scale,
                             gk_up_fwd, gk_bias_fwd, gk_up_bwd, gk_bias_bwd, gla_norm_g, w_out)
              for l in range(depth)]
    fng = final_norm_g[None, :]
    return (_trunk(x_prompt, layers, fng), _trunk(x_sample, layers, fng))
```

```python
import functools
import math

import jax
import jax.numpy as jnp
import numpy as np
from jax import lax
from jax.experimental import pallas as pl
from jax.experimental.pallas import tpu as pltpu

F32 = jnp.float32
BF16 = jnp.bfloat16

D_MODEL = 1024
NORM_EPS = 1e-6
MLA_HEADS = 4
MLA_NOPE = 128
MLA_ROPE = 64
MLA_VDIM = 128
MLA_Q_LORA = 256
MLA_KV_LORA = 128
MLA_WIDTH = MLA_HEADS * MLA_VDIM
MLA_SCALE = (MLA_NOPE + MLA_ROPE) ** -0.5
ROPE_BASE = 10000.0
QK_WIDTH = 256
POOL_WINDOWS = (2, 4, 8, 16)
POOL_WIDTH = 256
POOL_GROUP_DIM = 64
POOL_HALO = 8
GLA_HEADS = 4
GLA_WIDTH = 256
GLA_KEY = 128
GLA_DK = 32
GLA_DV = 64
GLA_GATE_RANK = 16
GLA_GATE_NORM = 16.0
GLA_SUB = 16
GLA_TILE = 256

MIX_WIDTH = MLA_WIDTH + POOL_WIDTH + GLA_WIDTH
IN_SIZES = (MLA_Q_LORA, MLA_KV_LORA, MLA_ROPE, MLA_WIDTH, POOL_WIDTH, POOL_WIDTH,
            GLA_KEY, GLA_KEY, GLA_WIDTH, GLA_GATE_RANK, GLA_GATE_RANK, GLA_WIDTH)
IN_OFFS = tuple(int(v) for v in np.cumsum((0,) + IN_SIZES))

ZA_WIDTH = MLA_Q_LORA + MLA_KV_LORA + 2 * MLA_ROPE
ZGATE_WIDTH = MLA_WIDTH + POOL_WIDTH + GLA_WIDTH
ZGLA_WIDTH = 2 * GLA_KEY + GLA_WIDTH + 128

PROJ_TILE = 512
MIX_TILE = 512
ATTN_TQ = 512
ATTN_TK = 1024
ATTN_VT_ROWS = MLA_VDIM + 16
VMEM_LIMIT = 56 * 1024 * 1024


def _rms(x, g):
    return x * lax.rsqrt(jnp.mean(x * x, axis=-1, keepdims=True) + NORM_EPS) * g


def _dot(a, b):
    return jnp.dot(a, b, preferred_element_type=F32)


def _dot_nt(a, b):
    return lax.dot_general(a, b, (((1,), (1,)), ((), ())), preferred_element_type=F32)


def _proj_kernel(x_ref, ng_ref, wa_ref, wgate_ref, wpu_ref, wgla_ref, qg_ref, wuq_ref, kvg_ref, wuk_ref,
                 wuvt_ref, tab_ref, q_ref, k_ref, vt_ref, zgate_ref, upool_ref, zgla_ref):
    x = x_ref[0]
    h = _rms(x, ng_ref[...]).astype(BF16)
    za = _dot(h, wa_ref[...])
    zgate_ref[0] = _dot(h, wgate_ref[...]).astype(BF16)
    upool_ref[0] = _dot(h, wpu_ref[...])
    zgla_ref[0] = _dot(h, wgla_ref[...]).astype(BF16)
    tab = tab_ref[...]
    qscale = MLA_SCALE * math.log2(math.e)
    cqn = _rms(za[:, :MLA_Q_LORA], qg_ref[...]).astype(BF16)
    qall = _dot(cqn, wuq_ref[...])
    for hh in range(MLA_HEADS):
        base = hh * QK_WIDTH
        qn = qall[:, base:base + MLA_NOPE] * qscale
        t = qall[:, base + MLA_NOPE:base + QK_WIDTH] * tab
        r = (t + pltpu.roll(t, MLA_ROPE, 1)) * qscale
        q_ref[0, hh, :, 0:MLA_NOPE] = qn.astype(BF16)
        q_ref[0, hh, :, MLA_NOPE:QK_WIDTH] = r.astype(BF16)
    ckvn = _rms(za[:, MLA_Q_LORA:MLA_Q_LORA + MLA_KV_LORA], kvg_ref[...]).astype(BF16)
    kn = _dot(ckvn, wuk_ref[...])
    t = za[:, MLA_Q_LORA + MLA_KV_LORA:] * tab
    kr = t + pltpu.roll(t, MLA_ROPE, 1)
    lane = lax.broadcasted_iota(jnp.int32, kr.shape, 1)
    kr = jnp.where(lane < MLA_ROPE, kr, 0.0).astype(BF16)
    for hh in range(MLA_HEADS):
        k_ref[0, hh, :, 0:MLA_NOPE] = kn[:, hh * MLA_NOPE:(hh + 1) * MLA_NOPE].astype(BF16)
        k_ref[0, hh, :, MLA_NOPE:QK_WIDTH] = kr
    vt = _dot_nt(wuvt_ref[...], ckvn)
    ones = jnp.ones((ATTN_VT_ROWS - MLA_VDIM, vt.shape[1]), BF16)
    for hh in range(MLA_HEADS):
        vt_ref[0, hh, 0:MLA_VDIM] = vt[hh * MLA_VDIM:(hh + 1) * MLA_VDIM].astype(BF16)
        vt_ref[0, hh, MLA_VDIM:] = ones


def _proj(x, lw, tab):
    B, S, _ = x.shape
    tm = min(PROJ_TILE, S)
    full = lambda a: pl.BlockSpec(a.shape, lambda b, i: (0,) * a.ndim)
    weights = (lw["norm_g"], lw["wa"], lw["wgate"], lw["wpu"], lw["wgla"], lw["q_norm_g"], lw["wuq"],
               lw["kv_norm_g"], lw["wuk"], lw["wuvt"])
    return pl.pallas_call(
        _proj_kernel,
        grid=(B, S // tm),
        in_specs=[pl.BlockSpec((1, tm, D_MODEL), lambda b, i: (b, i, 0))]
        + [full(w) for w in weights]
        + [pl.BlockSpec((tm, 128), lambda b, i: (i, 0))],
        out_specs=[
            pl.BlockSpec((1, MLA_HEADS, tm, QK_WIDTH), lambda b, i: (b, 0, i, 0)),
            pl.BlockSpec((1, MLA_HEADS, tm, QK_WIDTH), lambda b, i: (b, 0, i, 0)),
            pl.BlockSpec((1, MLA_HEADS, ATTN_VT_ROWS, tm), lambda b, i: (b, 0, 0, i)),
            pl.BlockSpec((1, tm, ZGATE_WIDTH), lambda b, i: (b, i, 0)),
            pl.BlockSpec((1, tm, POOL_WIDTH), lambda b, i: (b, i, 0)),
            pl.BlockSpec((1, tm, ZGLA_WIDTH), lambda b, i: (b, i, 0)),
        ],
        out_shape=[
            jax.ShapeDtypeStruct((B, MLA_HEADS, S, QK_WIDTH), BF16),
            jax.ShapeDtypeStruct((B, MLA_HEADS, S, QK_WIDTH), BF16),
            jax.ShapeDtypeStruct((B, MLA_HEADS, ATTN_VT_ROWS, S), BF16),
            jax.ShapeDtypeStruct((B, S, ZGATE_WIDTH), BF16),
            jax.ShapeDtypeStruct((B, S, POOL_WIDTH), F32),
            jax.ShapeDtypeStruct((B, S, ZGLA_WIDTH), BF16),
        ],
        compiler_params=pltpu.CompilerParams(
            dimension_semantics=("parallel", "parallel"), vmem_limit_bytes=VMEM_LIMIT),
        name="proj",
    )(x, *weights, tab)


def _attn_kernel(q_ref, k_ref, vt_ref, o_ref, sa_ref, sb_ref, *, tk, nk):
    q = q_ref[0, 0]
    tq = q.shape[0]

    def scores(i, s_ref):
        start = pl.multiple_of(i * tk, tk)
        s = _dot_nt(k_ref[0, 0, pl.ds(start, tk), :], q)
        s_ref[:, 0:tq] = s
        return jnp.max(s, axis=0, keepdims=True)

    def update(i, s_ref, mx, m, acc):
        start = pl.multiple_of(i * tk, tk)
        m_new = jnp.maximum(m, mx)
        p = jnp.exp2(s_ref[:, 0:tq] - m_new).astype(BF16)
        acc = jnp.exp2(m - m_new) * acc + _dot(vt_ref[0, 0, :, pl.ds(start, tk)], p)
        return m_new, acc

    def body(j, carry):
        m, acc, mx_a = carry
        i = 2 * j
        mx_b = scores(i + 1, sb_ref)
        m, acc = update(i, sa_ref, mx_a, m, acc)
        mx_a = scores(i + 2, sa_ref)
        m, acc = update(i + 1, sb_ref, mx_b, m, acc)
        return m, acc, mx_a

    m = jnp.full((1, tq), -jnp.inf, F32)
    acc = jnp.zeros((ATTN_VT_ROWS, tq), F32)
    mx_a = scores(0, sa_ref)
    if nk > 1:
        m, acc, mx_a = lax.fori_loop(0, nk // 2 - 1, body, (m, acc, mx_a))
        mx_b = scores(nk - 1, sb_ref)
        m, acc = update(nk - 2, sa_ref, mx_a, m, acc)
        m, acc = update(nk - 1, sb_ref, mx_b, m, acc)
    else:
        m, acc = update(0, sa_ref, mx_a, m, acc)
    o_ref[0] = (acc[:MLA_VDIM] / acc[MLA_VDIM:MLA_VDIM + 1]).T.astype(o_ref.dtype)


def _attn(q, k, vt):
    B, H, S, _ = q.shape
    tq = min(ATTN_TQ, S)
    tk = min(ATTN_TK, S)
    nk = S // tk
    assert nk == 1 or nk % 2 == 0
    return pl.pallas_call(
        functools.partial(_attn_kernel, tk=tk, nk=nk),
        grid=(B, H, S // tq),
        in_specs=[
            pl.BlockSpec((1, 1, tq, QK_WIDTH), lambda b, h, i: (b, h, i, 0)),
            pl.BlockSpec((1, 1, S, QK_WIDTH), lambda b, h, i: (b, h, 0, 0)),
            pl.BlockSpec((1, 1, ATTN_VT_ROWS, S), lambda b, h, i: (b, h, 0, 0)),
        ],
        out_specs=pl.BlockSpec((1, tq, MLA_VDIM), lambda b, h, i: (b, i, h)),
        out_shape=jax.ShapeDtypeStruct((B, S, MLA_WIDTH), BF16),
        scratch_shapes=[pltpu.VMEM((tk, tq + 128), F32), pltpu.VMEM((tk, tq + 128), F32)],
        compiler_params=pltpu.CompilerParams(
            dimension_semantics=("parallel", "parallel", "parallel"), vmem_limit_bytes=VMEM_LIMIT),
        name="attn",
    )(q, k, vt)


def _split2(x):
    hi = x.astype(BF16)
    lo = (x - hi.astype(F32)).astype(BF16)
    return jnp.concatenate([hi, lo], axis=1)


def _sum2(y, w):
    return y[:, :w] + y[:, w:]


def _gla_tile(z, up, bias, pin, pex, state, reverse):
    T = z.shape[0]
    nsub = T // GLA_SUB
    q = z[:, 0:GLA_KEY].astype(F32) * (GLA_DK ** -0.5)
    k = z[:, GLA_KEY:2 * GLA_KEY].astype(F32)
    v = z[:, 2 * GLA_KEY:2 * GLA_KEY + GLA_WIDTH]
    lr = z[:, 2 * GLA_KEY + GLA_WIDTH:]
    zg = _dot(lr, up) + bias
    yield
    g = (jnp.minimum(zg, 0.0) - jnp.log1p(jnp.exp(-jnp.abs(zg)))) * (1.0 / GLA_GATE_NORM)
    g2 = _split2(g)
    b_in = _sum2(_dot(pin, g2), GLA_KEY)
    yield
    b_ex = _sum2(_dot(pex, g2), GLA_KEY)
    yield
    qt = q * jnp.exp(b_in)
    kt = (k * jnp.exp(-b_in)).astype(BF16)
    kp = k * jnp.exp(b_ex)
    dd = jnp.exp(b_in + b_ex)

    pmask = pin > 0
    qhead = lax.broadcasted_iota(jnp.int32, (1, GLA_KEY), 1) // GLA_DK
    vhead = lax.broadcasted_iota(jnp.int32, (1, GLA_WIDTH), 1) // GLA_DV
    a = []
    for hh in range(GLA_HEADS):
        qh = jnp.where(qhead == hh, qt, 0.0).astype(BF16)
        a.append(jnp.where(pmask, _dot_nt(qh, kt), 0.0).astype(BF16))
        yield

    kpt = kp.T
    ddt = dd.T
    sub_of_lane = lax.broadcasted_iota(jnp.int32, (1, T), 1) // GLA_SUB
    stack = jnp.concatenate(
        [jnp.where(sub_of_lane == c, kpt, 0.0).astype(BF16) for c in range(nsub)], axis=0)
    inc = _dot(stack, v)
    yield
    same_head = (lax.broadcasted_iota(jnp.int32, (GLA_KEY, 1), 0) // GLA_DK) == vhead

    o = jnp.zeros((T, GLA_WIDTH), F32)
    for hh in range(GLA_HEADS):
        o = o + _dot(a[hh], jnp.where(vhead == hh, v, jnp.zeros_like(v)))
        yield

    inter = [None] * nsub
    order = range(nsub - 1, -1, -1) if reverse else range(nsub)
    for c in order:
        rows = slice(c * GLA_SUB, (c + 1) * GLA_SUB)
        inter[c] = _dot(qt[rows].astype(BF16), state.astype(BF16))
        state = state * ddt[:, c * GLA_SUB:c * GLA_SUB + 1] + jnp.where(
            same_head, inc[c * GLA_KEY:(c + 1) * GLA_KEY], 0.0)
        yield
    return o + jnp.concatenate(inter, axis=0), state


def _lockstep(gens):
    results = [None] * len(gens)
    live = list(range(len(gens)))
    while live:
        for idx in list(live):
            try:
                next(gens[idx])
            except StopIteration as done:
                results[idx] = done.value
                live.remove(idx)
    return results


def _gla_kernel(zf_ref, zb_ref, upf_ref, biasf_ref, upb_ref, biasb_ref, pinf_ref, pexf_ref, pinb_ref, pexb_ref,
                of_ref, ob_ref, sf_ref, sb_ref):
    @pl.when(pl.program_id(1) == 0)
    def _():
        sf_ref[...] = jnp.zeros_like(sf_ref)
        sb_ref[...] = jnp.zeros_like(sb_ref)

    (of, sf), (ob, sb) = _lockstep([
        _gla_tile(zf_ref[0], upf_ref[...], biasf_ref[...], pinf_ref[...], pexf_ref[...], sf_ref[...], False),
        _gla_tile(zb_ref[0], upb_ref[...], biasb_ref[...], pinb_ref[...], pexb_ref[...], sb_ref[...], True)])
    of_ref[0] = of.astype(of_ref.dtype)
    ob_ref[0] = ob.astype(ob_ref.dtype)
    sf_ref[...] = sf
    sb_ref[...] = sb


def _gla_masks(T, reverse):
    t = np.arange(T)
    same = (t[:, None] // GLA_SUB) == (t[None, :] // GLA_SUB)
    before = (t[None, :] >= t[:, None]) if reverse else (t[None, :] <= t[:, None])
    pin = same & before
    pex = same & ~before
    return jnp.asarray(pin, BF16), jnp.asarray(pex, BF16)


def _gla(zgla, lw):
    B, S, _ = zgla.shape
    T = min(GLA_TILE, S)
    n = S // T
    consts = (lw["up_f"], lw["bias_f"], lw["up_b"], lw["bias_b"]) + _gla_masks(T, False) + _gla_masks(T, True)
    fwd = lambda b, i: (b, i, 0)
    bwd = lambda b, i: (b, n - 1 - i, 0)
    full = lambda a: pl.BlockSpec(a.shape, lambda b, i: (0,) * a.ndim)
    return pl.pallas_call(
        _gla_kernel,
        grid=(B, n),
        in_specs=[pl.BlockSpec((1, T, ZGLA_WIDTH), fwd), pl.BlockSpec((1, T, ZGLA_WIDTH), bwd)]
        + [full(c) for c in consts],
        out_specs=[pl.BlockSpec((1, T, GLA_WIDTH), fwd), pl.BlockSpec((1, T, GLA_WIDTH), bwd)],
        out_shape=[jax.ShapeDtypeStruct((B, S, GLA_WIDTH), BF16)] * 2,
        scratch_shapes=[pltpu.VMEM((GLA_KEY, GLA_WIDTH), F32)] * 2,
        compiler_params=pltpu.CompilerParams(
            dimension_semantics=("parallel", "arbitrary"), vmem_limit_bytes=VMEM_LIMIT),
        name="gla",
    )(zgla, zgla, *consts)


def _silu(x):
    h = 0.5 * x
    return h + h * jnp.tanh(h)


def _mix_kernel(x_ref, omla_ref, zgate_ref, u_ref, uprev_ref, unext_ref, gf_ref, gb_ref, wpool_ref, pscale_ref,
                gng_ref, hsum_ref, wout_ref, fng_ref, out_ref, ubuf, lvl_a, lvl_b, *, seq_len, final):
    tm = x_ref.shape[1]
    i = pl.program_id(1)
    zg = zgate_ref[0].astype(F32)
    gate_mla = zg[:, 0:MLA_WIDTH]
    gate_pool = zg[:, MLA_WIDTH:MLA_WIDTH + POOL_WIDTH]
    gate_gla = zg[:, MLA_WIDTH + POOL_WIDTH:]
    u = u_ref[0]

    H = POOL_HALO
    lo, hi = H, tm + 3 * H
    pad = jnp.zeros((H, POOL_WIDTH), F32)
    for buf in (ubuf, lvl_a, lvl_b):
        buf[0:H] = pad[:, 0:buf.shape[1]]
        buf[hi:] = pad[:, 0:buf.shape[1]]
    ubuf[lo:2 * H] = jnp.where(i > 0, uprev_ref[0], 0.0)
    ubuf[2 * H:2 * H + tm] = u
    ubuf[2 * H + tm:hi] = jnp.where(i < pl.num_programs(1) - 1, unext_ref[0], 0.0)

    def level(src, cols, before, after):
        return src[lo - before:hi - before, cols] + src[lo + after:hi + after, cols]

    tile_rows = slice(H, H + tm)
    all_cols = slice(0, 128)
    p2 = level(ubuf, slice(0, 128), 1, 0)
    lvl_a[lo:hi] = p2
    p4 = level(lvl_a, all_cols, 1, 1)
    sums = [(p2[tile_rows], p4[tile_rows])]
    p2 = level(ubuf, slice(128, 256), 1, 0)
    lvl_a[lo:hi] = p2
    lvl_b[lo:hi] = level(lvl_a, all_cols, 1, 1)
    p8 = level(lvl_b, all_cols, 2, 2)
    lvl_a[lo:hi] = p8
    p16 = level(lvl_a, all_cols, 4, 4)
    sums.append((p8[tile_rows], p16[tile_rows]))

    pos = i * tm + lax.broadcasted_iota(jnp.int32, (tm, 128), 0)
    lane = lax.broadcasted_iota(jnp.int32, (1, 128), 1)
    cnt = lambda w: (jnp.minimum(pos + w // 2, seq_len) - jnp.maximum(pos - w // 2, 0)).astype(F32)
    pooled = [jnp.where(lane < POOL_GROUP_DIM, small / cnt(ws), big / cnt(wb))
              for (small, big), (ws, wb) in zip(sums, ((2, 4), (8, 16)))]
    pooled = jnp.concatenate(pooled, axis=1) - u
    o_pool = _dot(pooled.astype(BF16), wpool_ref[...]) * pscale_ref[...]

    og = gf_ref[0].astype(F32) + gb_ref[0].astype(F32)
    sq = og * og
    sq_hi = sq.astype(BF16)
    sq_lo = (sq - sq_hi.astype(F32)).astype(BF16)
    ms = (_dot(sq_hi, hsum_ref[...]) + _dot(sq_lo, hsum_ref[...])) * (1.0 / GLA_DV)
    o_gla = og * lax.rsqrt(ms + NORM_EPS) * gng_ref[...]

    m_mla = (omla_ref[0].astype(F32) * _silu(gate_mla)).astype(BF16)
    m_pool = (o_pool * _silu(gate_pool)).astype(BF16)
    m_gla = (o_gla * _silu(gate_gla)).astype(BF16)
    y = (x_ref[0] + _dot(m_mla, wout_ref[0:MLA_WIDTH])
         + _dot(m_pool, wout_ref[MLA_WIDTH:MLA_WIDTH + POOL_WIDTH])
         + _dot(m_gla, wout_ref[MLA_WIDTH + POOL_WIDTH:]))
    if final:
        y = _rms(y, fng_ref[...])
    out_ref[0] = y


def _mix(x, omla, zgate, upool, gf, gb, lw, fng, final):
    B, S, _ = x.shape
    tm = min(MIX_TILE, S)
    hb = tm // POOL_HALO
    nhalo = S // POOL_HALO
    full = lambda a: pl.BlockSpec(a.shape, lambda b, i: (0,) * a.ndim)
    tile = lambda w: pl.BlockSpec((1, tm, w), lambda b, i: (b, i, 0))
    weights = (lw["wpool"], lw["pool_scale"], lw["gla_norm_g"], lw["hsum"], lw["wout"], fng)
    return pl.pallas_call(
        functools.partial(_mix_kernel, seq_len=S, final=final),
        grid=(B, S // tm),
        in_specs=[
            tile(D_MODEL), tile(MLA_WIDTH), tile(ZGATE_WIDTH), tile(POOL_WIDTH),
            pl.BlockSpec((1, POOL_HALO, POOL_WIDTH), lambda b, i: (b, jnp.maximum(i * hb - 1, 0), 0)),
            pl.BlockSpec((1, POOL_HALO, POOL_WIDTH), lambda b, i: (b, jnp.minimum((i + 1) * hb, nhalo - 1), 0)),
            tile(GLA_WIDTH), tile(GLA_WIDTH),
        ] + [full(w) for w in weights],
        out_specs=tile(D_MODEL),
        out_shape=jax.ShapeDtypeStruct((B, S, D_MODEL), F32),
        scratch_shapes=[pltpu.VMEM((tm + 4 * POOL_HALO, POOL_WIDTH), F32),
                        pltpu.VMEM((tm + 4 * POOL_HALO, 128), F32),
                        pltpu.VMEM((tm + 4 * POOL_HALO, 128), F32)],
        compiler_params=pltpu.CompilerParams(
            dimension_semantics=("parallel", "parallel"), vmem_limit_bytes=VMEM_LIMIT),
        name="mix",
    )(x, omla, zgate, upool, upool, upool, gf, gb, *weights)


def _rope_table(seq):
    inv_freq = 1.0 / (ROPE_BASE ** (jnp.arange(0, MLA_ROPE, 2, dtype=F32) / MLA_ROPE))
    ang = jnp.arange(seq, dtype=F32)[:, None] * inv_freq[None, :]
    cos, sin = jnp.cos(ang), jnp.sin(ang)
    return jnp.concatenate([cos, cos, -sin, sin], axis=1)


def _swap_halves(w):
    half = w.shape[-1] // 2
    return jnp.concatenate([w[..., half:], w[..., :half]], axis=-1)


def _layer_weights(l, norm_g, w_in, q_norm_g, w_uq, kv_norm_g, w_ukv, w_pool, pool_scale,
                   gk_up_fwd, gk_bias_fwd, gk_up_bwd, gk_bias_bwd, gla_norm_g, w_out):
    cols = [w_in[l][:, IN_OFFS[j]:IN_OFFS[j + 1]] for j in range(len(IN_SIZES))]
    (c_q, c_kv, k_rope, gate_mla, u_pool, gate_pool, q_gla, k_gla, v_gla, lr_f, lr_b, gate_gla) = cols
    lr_pad = jnp.zeros((D_MODEL, 128 - 2 * GLA_GATE_RANK), F32)
    wa = jnp.concatenate([c_q, c_kv, k_rope, _swap_halves(k_rope)], axis=1)
    wgate = jnp.concatenate([gate_mla, gate_pool, gate_gla], axis=1)
    wgla = jnp.concatenate([q_gla, k_gla, v_gla, lr_f, lr_b, lr_pad], axis=1)
    uq = w_uq[l].reshape(MLA_Q_LORA, MLA_HEADS, MLA_NOPE + MLA_ROPE)
    uq_rope = uq[..., MLA_NOPE:]
    wuq = jnp.concatenate([uq, _swap_halves(uq_rope)], axis=-1).reshape(MLA_Q_LORA, MLA_HEADS * QK_WIDTH)
    ukv = w_ukv[l].reshape(MLA_KV_LORA, MLA_HEADS, MLA_NOPE + MLA_VDIM)
    wuk = ukv[..., :MLA_NOPE].reshape(MLA_KV_LORA, MLA_HEADS * MLA_NOPE)
    wuvt = ukv[..., MLA_NOPE:].reshape(MLA_KV_LORA, MLA_HEADS * MLA_VDIM).T
    wpool = jnp.zeros((POOL_WIDTH, POOL_WIDTH), F32)
    for gi in range(len(POOL_WINDOWS)):
        sl = slice(gi * POOL_GROUP_DIM, (gi + 1) * POOL_GROUP_DIM)
        wpool = wpool.at[sl, sl].set(w_pool[l, gi])
    up_pad = jnp.zeros((128 - GLA_GATE_RANK, GLA_KEY), F32)
    up_f = jnp.concatenate([gk_up_fwd[l], up_pad], axis=0)
    up_b = jnp.concatenate([up_pad[:GLA_GATE_RANK], gk_up_bwd[l], up_pad[GLA_GATE_RANK:]], axis=0)
    head = np.arange(GLA_WIDTH) // GLA_DV
    hsum = jnp.asarray(head[:, None] == head[None, :], BF16)
    return {
        "norm_g": norm_g[l][None, :],
        "wa": wa.astype(BF16), "wgate": wgate.astype(BF16), "wpu": u_pool.astype(BF16),"wgla": wgla.astype(BF16),
        "q_norm_g": q_norm_g[l][None, :], "wuq": wuq.astype(BF16),
        "kv_norm_g": kv_norm_g[l][None, :], "wuk": wuk.astype(BF16), "wuvt": wuvt.astype(BF16),
        "wpool": wpool.astype(BF16), "pool_scale": pool_scale[l][None, :],
        "up_f": up_f.astype(BF16), "bias_f": gk_bias_fwd[l][None, :],
        "up_b": up_b.astype(BF16), "bias_b": gk_bias_bwd[l][None, :],
        "gla_norm_g": jnp.tile(gla_norm_g[l], GLA_HEADS)[None, :], "hsum": hsum,
        "wout": w_out[l].astype(BF16),
    }


def _trunk(x, layers, fng):
    tab = _rope_table(x.shape[1])
    for l, lw in enumerate(layers):
        q, k, vt, zgate, upool, zgla = _proj(x, lw, tab)
        omla = _attn(q, k, vt)
        gf, gb = _gla(zgla, lw)
        x = _mix(x, omla, zgate, upool, gf, gb, lw, fng, final=(l == len(layers) - 1))
    return x


def kernel(x_prompt, x_sample, norm_g, w_in, q_norm_g, w_uq, kv_norm_g, w_ukv, w_pool, pool_scale,
           gk_up_fwd, gk_bias_fwd, gk_up_bwd, gk_bias_bwd, gla_norm_g, w_out, final_norm_g):
    depth = w_in.shape[0]
    layers = [_layer_weights(l, norm_g, w_in, q_norm_g, w_uq, kv_norm_g, w_ukv, w_pool, pool_scale,
                             gk_up_fwd, gk_bias_fwd, gk_up_bwd, gk_bias_bwd, gla_norm_g, w_out)
              for l in range(depth)]
    fng = final_norm_g[None, :]
    return (_trunk(x_prompt, layers, fng), _trunk(x_sample, layers, fng))
```

```python
import functools
import math

import jax
import jax.numpy as jnp
import numpy as np
from jax import lax
from jax.experimental import pallas as pl
from jax.experimental.pallas import tpu as pltpu

F32 = jnp.float32
BF16 = jnp.bfloat16

D_MODEL = 1024
NORM_EPS = 1e-6
MLA_HEADS = 4
MLA_NOPE = 128
MLA_ROPE = 64
MLA_VDIM = 128
MLA_Q_LORA = 256
MLA_KV_LORA = 128
MLA_WIDTH = MLA_HEADS * MLA_VDIM
MLA_SCALE = (MLA_NOPE + MLA_ROPE) ** -0.5
ROPE_BASE = 10000.0
QK_WIDTH = 256
POOL_WINDOWS = (2, 4, 8, 16)
POOL_WIDTH = 256
POOL_GROUP_DIM = 64
POOL_HALO = 8
GLA_HEADS = 4
GLA_WIDTH = 256
GLA_KEY = 128
GLA_DK = 32
GLA_DV = 64
GLA_GATE_RANK = 16
GLA_GATE_NORM = 16.0
GLA_SUB = 32
GLA_TILE = 256

MIX_WIDTH = MLA_WIDTH + POOL_WIDTH + GLA_WIDTH
IN_SIZES = (MLA_Q_LORA, MLA_KV_LORA, MLA_ROPE, MLA_WIDTH, POOL_WIDTH, POOL_WIDTH,
            GLA_KEY, GLA_KEY, GLA_WIDTH, GLA_GATE_RANK, GLA_GATE_RANK, GLA_WIDTH)
IN_OFFS = tuple(int(v) for v in np.cumsum((0,) + IN_SIZES))

ZA_WIDTH = MLA_Q_LORA + MLA_KV_LORA + 2 * MLA_ROPE
ZGATE_WIDTH = MLA_WIDTH + POOL_WIDTH + GLA_WIDTH
ZGLA_WIDTH = 2 * GLA_KEY + GLA_WIDTH + 128

ROW_TILE = 512
ATTN_TQ = 512
ATTN_TK = 1024
ATTN_VT_ROWS = MLA_VDIM + 16
VMEM_LIMIT = 56 * 1024 * 1024


def _rms(x, g):
    return x * lax.rsqrt(jnp.mean(x * x, axis=-1, keepdims=True) + NORM_EPS) * g


def _dot(a, b):
    return jnp.dot(a, b, preferred_element_type=F32)


def _dot_nt(a, b):
    return lax.dot_general(a, b, (((1,), (1,)), ((), ())), preferred_element_type=F32)


N_PROJ_WEIGHTS = 10
N_PROJ_OUTS = 6


def _proj_kernel(x_ref, *refs):
    _proj_body(x_ref[0], *refs)


def _proj_body(x, ng_ref, wa_ref, wgate_ref, wpu_ref, wgla_ref, qg_ref, wuq_ref, kvg_ref, wuk_ref,
               wuvt_ref, tab_ref, q_ref, k_ref, vt_ref, zgate_ref, upool_ref, zgla_ref):
    h = _rms(x, ng_ref[...]).astype(BF16)
    za = _dot(h, wa_ref[...])
    zgate_ref[0] = _dot(h, wgate_ref[...]).astype(BF16)
    upool_ref[0] = _dot(h, wpu_ref[...])
    zgla_ref[0] = _dot(h, wgla_ref[...]).astype(BF16)
    tab = tab_ref[...]
    qscale = MLA_SCALE * math.log2(math.e)
    cqn = _rms(za[:, :MLA_Q_LORA], qg_ref[...]).astype(BF16)
    qall = _dot(cqn, wuq_ref[...])
    for hh in range(MLA_HEADS):
        base = hh * QK_WIDTH
        qn = qall[:, base:base + MLA_NOPE] * qscale
        t = qall[:, base + MLA_NOPE:base + QK_WIDTH] * tab
        r = (t + pltpu.roll(t, MLA_ROPE, 1)) * qscale
        q_ref[0, hh, :, 0:MLA_NOPE] = qn.astype(BF16)
        q_ref[0, hh, :, MLA_NOPE:QK_WIDTH] = r.astype(BF16)
    ckvn = _rms(za[:, MLA_Q_LORA:MLA_Q_LORA + MLA_KV_LORA], kvg_ref[...]).astype(BF16)
    kn = _dot(ckvn, wuk_ref[...])
    t = za[:, MLA_Q_LORA + MLA_KV_LORA:] * tab
    kr = t + pltpu.roll(t, MLA_ROPE, 1)
    lane = lax.broadcasted_iota(jnp.int32, kr.shape, 1)
    kr = jnp.where(lane < MLA_ROPE, kr, 0.0).astype(BF16)
    for hh in range(MLA_HEADS):
        k_ref[0, hh, :, 0:MLA_NOPE] = kn[:, hh * MLA_NOPE:(hh + 1) * MLA_NOPE].astype(BF16)
        k_ref[0, hh, :, MLA_NOPE:QK_WIDTH] = kr
    vt = _dot_nt(wuvt_ref[...], ckvn)
    ones = jnp.ones((ATTN_VT_ROWS - MLA_VDIM, vt.shape[1]), BF16)
    for hh in range(MLA_HEADS):
        vt_ref[0, hh, 0:MLA_VDIM] = vt[hh * MLA_VDIM:(hh + 1) * MLA_VDIM].astype(BF16)
        vt_ref[0, hh, MLA_VDIM:] = ones


def _full_spec(a):
    return pl.BlockSpec(a.shape, lambda b, i: (0,) * a.ndim)


def _proj_operands(lw, tab, tm):
    weights = (lw["norm_g"], lw["wa"], lw["wgate"], lw["wpu"], lw["wgla"], lw["q_norm_g"], lw["wuq"],
               lw["kv_norm_g"], lw["wuk"], lw["wuvt"])
    assert len(weights) == N_PROJ_WEIGHTS
    specs = [_full_spec(w) for w in weights] + [pl.BlockSpec((tm, 128), lambda b, i: (i, 0))]
    return weights + (tab,), specs


def _proj_outputs(B, S, tm):
    specs = [
        pl.BlockSpec((1, MLA_HEADS, tm, QK_WIDTH), lambda b, i: (b, 0, i, 0)),
        pl.BlockSpec((1, MLA_HEADS, tm, QK_WIDTH), lambda b, i: (b, 0, i, 0)),
        pl.BlockSpec((1, MLA_HEADS, ATTN_VT_ROWS, tm), lambda b, i: (b, 0, 0, i)),
        pl.BlockSpec((1, tm, ZGATE_WIDTH), lambda b, i: (b, i, 0)),
        pl.BlockSpec((1, tm, POOL_WIDTH), lambda b, i: (b, i, 0)),
        pl.BlockSpec((1, tm, ZGLA_WIDTH), lambda b, i: (b, i, 0)),
    ]
    shapes = [
        jax.ShapeDtypeStruct((B, MLA_HEADS, S, QK_WIDTH), BF16),
        jax.ShapeDtypeStruct((B, MLA_HEADS, S, QK_WIDTH), BF16),
        jax.ShapeDtypeStruct((B, MLA_HEADS, ATTN_VT_ROWS, S), BF16),
        jax.ShapeDtypeStruct((B, S, ZGATE_WIDTH), BF16),
        jax.ShapeDtypeStruct((B, S, POOL_WIDTH), F32),
        jax.ShapeDtypeStruct((B, S, ZGLA_WIDTH), BF16),
    ]
    assert len(specs) == N_PROJ_OUTS
    return specs, shapes


def _proj(x, lw, tab):
    B, S, _ = x.shape
    tm = min(ROW_TILE, S)
    operands, op_specs = _proj_operands(lw, tab, tm)
    out_specs, out_shape = _proj_outputs(B, S, tm)
    return pl.pallas_call(
        _proj_kernel,
        grid=(B, S // tm),
        in_specs=[pl.BlockSpec((1, tm, D_MODEL), lambda b, i: (b, i, 0))] + op_specs,
        out_specs=out_specs,
        out_shape=out_shape,
        compiler_params=pltpu.CompilerParams(
            dimension_semantics=("parallel", "parallel"), vmem_limit_bytes=VMEM_LIMIT),
        name="proj",
    )(x, *operands)


def _attn_kernel(q_ref, k_ref, vt_ref, o_ref, sa_ref, sb_ref, *, tk, nk):
    q = q_ref[0, 0]
    tq = q.shape[0]

    def scores(i, s_ref):
        start = pl.multiple_of(i * tk, tk)
        s = _dot_nt(k_ref[0, 0, pl.ds(start, tk), :], q)
        s_ref[:, 0:tq] = s
        return jnp.max(s, axis=0, keepdims=True)

    def update(i, s_ref, mx, m, acc):
        start = pl.multiple_of(i * tk, tk)
        m_new = jnp.maximum(m, mx)
        p = jnp.exp2(s_ref[:, 0:tq] - m_new).astype(BF16)
        acc = jnp.exp2(m - m_new) * acc + _dot(vt_ref[0, 0, :, pl.ds(start, tk)], p)
        return m_new, acc

    def body(j, carry):
        m, acc, mx_a = carry
        i = 2 * j
        mx_b = scores(i + 1, sb_ref)
        m, acc = update(i, sa_ref, mx_a, m, acc)
        mx_a = scores(i + 2, sa_ref)
        m, acc = update(i + 1, sb_ref, mx_b, m, acc)
        return m, acc, mx_a

    m = jnp.full((1, tq), -jnp.inf, F32)
    acc = jnp.zeros((ATTN_VT_ROWS, tq), F32)
    mx_a = scores(0, sa_ref)
    if nk > 1:
        m, acc, mx_a = lax.fori_loop(0, nk // 2 - 1, body, (m, acc, mx_a))
        mx_b = scores(nk - 1, sb_ref)
        m, acc = update(nk - 2, sa_ref, mx_a, m, acc)
        m, acc = update(nk - 1, sb_ref, mx_b, m, acc)
    else:
        m, acc = update(0, sa_ref, mx_a, m, acc)
    o_ref[0] = (acc[:MLA_VDIM] / acc[MLA_VDIM:MLA_VDIM + 1]).T.astype(o_ref.dtype)


def _attn(q, k, vt):
    B, H, S, _ = q.shape
    tq = min(ATTN_TQ, S)
    tk = min(ATTN_TK, S)
    nk = S // tk
    assert nk == 1 or nk % 2 == 0
    return pl.pallas_call(
        functools.partial(_attn_kernel, tk=tk, nk=nk),
        grid=(B, H, S // tq),
        in_specs=[
            pl.BlockSpec((1, 1, tq, QK_WIDTH), lambda b, h, i: (b, h, i, 0)),
            pl.BlockSpec((1, 1, S, QK_WIDTH), lambda b, h, i: (b, h, 0, 0)),
            pl.BlockSpec((1, 1, ATTN_VT_ROWS, S), lambda b, h, i: (b, h, 0, 0)),
        ],
        out_specs=pl.BlockSpec((1, tq, MLA_VDIM), lambda b, h, i: (b, i, h)),
        out_shape=jax.ShapeDtypeStruct((B, S, MLA_WIDTH), BF16),
        scratch_shapes=[pltpu.VMEM((tk, tq + 128), F32), pltpu.VMEM((tk, tq + 128), F32)],
        compiler_params=pltpu.CompilerParams(
            dimension_semantics=("parallel", "parallel", "parallel"), vmem_limit_bytes=VMEM_LIMIT),
        name="attn",
    )(q, k, vt)


def _split2(x):
    hi = x.astype(BF16)
    lo = (x - hi.astype(F32)).astype(BF16)
    return jnp.concatenate([hi, lo], axis=1)


def _sum2(y, w):
    return y[:, :w] + y[:, w:]


def _gla_tile(z, up, bias, pin, pmid, pex, state, reverse):
    T = z.shape[0]
    nsub = T // GLA_SUB
    q = z[:, 0:GLA_KEY].astype(F32) * (GLA_DK ** -0.5)
    k = z[:, GLA_KEY:2 * GLA_KEY].astype(F32)
    v = z[:, 2 * GLA_KEY:2 * GLA_KEY + GLA_WIDTH]
    lr = z[:, 2 * GLA_KEY + GLA_WIDTH:]
    zg = _dot(lr, up) + bias
    yield
    g = (jnp.minimum(zg, 0.0) - jnp.log1p(jnp.exp(-jnp.abs(zg)))) * (1.0 / GLA_GATE_NORM)
    g2 = _split2(g)
    b_in = _sum2(_dot(pin, g2), GLA_KEY)
    yield
    b_mid = _sum2(_dot(pmid, g2), GLA_KEY)
    yield
    b_ex = _sum2(_dot(pex, g2), GLA_KEY)
    yield
    qt = q * jnp.exp(b_in)
    qa = q * jnp.exp(b_in - b_mid)
    kt = (k * jnp.exp(b_mid - b_in)).astype(BF16)
    kp = k * jnp.exp(b_ex)
    dd = jnp.exp(b_in + b_ex)

    pmask = pin > 0
    qhead = lax.broadcasted_iota(jnp.int32, (1, GLA_KEY), 1) // GLA_DK
    vhead = lax.broadcasted_iota(jnp.int32, (1, GLA_WIDTH), 1) // GLA_DV
    a = []
    for hh in range(GLA_HEADS):
        qh = jnp.where(qhead == hh, qa, 0.0).astype(BF16)
        a.append(jnp.where(pmask, _dot_nt(qh, kt), 0.0).astype(BF16))
        yield

    kpt = kp.T
    ddt = dd.T
    sub_of_lane = lax.broadcasted_iota(jnp.int32, (1, T), 1) // GLA_SUB
    stack = jnp.concatenate(
        [jnp.where(sub_of_lane == c, kpt, 0.0).astype(BF16) for c in range(nsub)], axis=0)
    inc = _dot(stack, v)
    yield
    same_head = (lax.broadcasted_iota(jnp.int32, (GLA_KEY, 1), 0) // GLA_DK) == vhead

    o = jnp.zeros((T, GLA_WIDTH), F32)
    for hh in range(GLA_HEADS):
        o = o + _dot(a[hh], jnp.where(vhead == hh, v, jnp.zeros_like(v)))
        yield

    inter = [None] * nsub
    order = range(nsub - 1, -1, -1) if reverse else range(nsub)
    for c in order:
        rows = slice(c * GLA_SUB, (c + 1) * GLA_SUB)
        inter[c] = _dot(qt[rows].astype(BF16), state.astype(BF16))
        state = state * ddt[:, c * GLA_SUB:c * GLA_SUB + 1] + jnp.where(
            same_head, inc[c * GLA_KEY:(c + 1) * GLA_KEY], 0.0)
        yield
    return o + jnp.concatenate(inter, axis=0), state


def _lockstep(gens):
    results = [None] * len(gens)
    live = list(range(len(gens)))
    while live:
        for idx in list(live):
            try:
                next(gens[idx])
            except StopIteration as done:
                results[idx] = done.value
                live.remove(idx)
    return results


def _gla_kernel(zf_ref, zb_ref, upf_ref, biasf_ref, upb_ref, biasb_ref, pinf_ref, pmidf_ref, pexf_ref,
                pinb_ref, pmidb_ref, pexb_ref, of_ref, ob_ref, sf_ref, sb_ref):
    @pl.when(pl.program_id(1) == 0)
    def _():
        sf_ref[...] = jnp.zeros_like(sf_ref)
        sb_ref[...] = jnp.zeros_like(sb_ref)

    (of, sf), (ob, sb) = _lockstep([
        _gla_tile(zf_ref[0], upf_ref[...], biasf_ref[...], pinf_ref[...], pmidf_ref[...], pexf_ref[...],
                  sf_ref[...], False),
        _gla_tile(zb_ref[0], upb_ref[...], biasb_ref[...], pinb_ref[...], pmidb_ref[...], pexb_ref[...],
                  sb_ref[...], True)])
    of_ref[0] = of.astype(of_ref.dtype)
    ob_ref[0] = ob.astype(ob_ref.dtype)
    sf_ref[...] = sf
    sb_ref[...] = sb


def _gla_masks(T, reverse):
    t = np.arange(T)
    same = (t[:, None] // GLA_SUB) == (t[None, :] // GLA_SUB)
    before = (t[None, :] >= t[:, None]) if reverse else (t[None, :] <= t[:, None])
    pin = same & before
    pex = same & ~before
    in_sub = t[None, :] % GLA_SUB
    first_half = (in_sub >= GLA_SUB // 2) if reverse else (in_sub < GLA_SUB // 2)
    pmid = same & first_half
    return jnp.asarray(pin, BF16), jnp.asarray(pmid, BF16), jnp.asarray(pex, BF16)


def _gla(zgla, lw):
    B, S, _ = zgla.shape
    T = min(GLA_TILE, S)
    n = S // T
    consts = (lw["up_f"], lw["bias_f"], lw["up_b"], lw["bias_b"]) + _gla_masks(T, False) + _gla_masks(T, True)
    fwd = lambda b, i: (b, i, 0)
    bwd = lambda b, i: (b, n - 1 - i, 0)
    full = lambda a: pl.BlockSpec(a.shape, lambda b, i: (0,) * a.ndim)
    return pl.pallas_call(
        _gla_kernel,
        grid=(B, n),
        in_specs=[pl.BlockSpec((1, T, ZGLA_WIDTH), fwd), pl.BlockSpec((1, T, ZGLA_WIDTH), bwd)]
        + [full(c) for c in consts],
        out_specs=[pl.BlockSpec((1, T, GLA_WIDTH), fwd), pl.BlockSpec((1, T, GLA_WIDTH), bwd)],
        out_shape=[jax.ShapeDtypeStruct((B, S, GLA_WIDTH), BF16)] * 2,
        scratch_shapes=[pltpu.VMEM((GLA_KEY, GLA_WIDTH), F32)] * 2,
        compiler_params=pltpu.CompilerParams(
            dimension_semantics=("parallel", "arbitrary"), vmem_limit_bytes=VMEM_LIMIT),
        name="gla",
    )(zgla, zgla, *consts)


def _silu(x):
    h = 0.5 * x
    return h + h * jnp.tanh(h)


N_MIX_INPUTS = 13


def _mix_final_kernel(*refs, seq_len):
    ins, (fng_ref, out_ref), scratch = refs[:N_MIX_INPUTS], refs[N_MIX_INPUTS:N_MIX_INPUTS + 2], refs[-3:]
    out_ref[0] = _rms(_mix_body(*ins, *scratch, seq_len=seq_len), fng_ref[...])


def _mix_proj_kernel(*refs, seq_len):
    ins, scratch = refs[:N_MIX_INPUTS], refs[-3:]
    proj_ops = refs[N_MIX_INPUTS:N_MIX_INPUTS + N_PROJ_WEIGHTS + 1]
    out_ref = refs[N_MIX_INPUTS + N_PROJ_WEIGHTS + 1]
    proj_outs = refs[N_MIX_INPUTS + N_PROJ_WEIGHTS + 2:-3]
    y = _mix_body(*ins, *scratch, seq_len=seq_len)
    out_ref[0] = y
    _proj_body(y, *proj_ops, *proj_outs)


def _mix_body(x_ref, omla_ref, zgate_ref, u_ref, uprev_ref, unext_ref, gf_ref, gb_ref, wpool_ref, pscale_ref,
              gng_ref, hsum_ref, wout_ref, ubuf, lvl_a, lvl_b, *, seq_len):
    tm = x_ref.shape[1]
    i = pl.program_id(1)
    zg = zgate_ref[0].astype(F32)
    gate_mla = zg[:, 0:MLA_WIDTH]
    gate_pool = zg[:, MLA_WIDTH:MLA_WIDTH + POOL_WIDTH]
    gate_gla = zg[:, MLA_WIDTH + POOL_WIDTH:]
    u = u_ref[0]

    H = POOL_HALO
    lo, hi = H, tm + 3 * H
    pad = jnp.zeros((H, POOL_WIDTH), F32)
    for buf in (ubuf, lvl_a, lvl_b):
        buf[0:H] = pad[:, 0:buf.shape[1]]
        buf[hi:] = pad[:, 0:buf.shape[1]]
    ubuf[lo:2 * H] = jnp.where(i > 0, uprev_ref[0], 0.0)
    ubuf[2 * H:2 * H + tm] = u
    ubuf[2 * H + tm:hi] = jnp.where(i < pl.num_programs(1) - 1, unext_ref[0], 0.0)

    def level(src, cols, before, after):
        return src[lo - before:hi - before, cols] + src[lo + after:hi + after, cols]

    tile_rows = slice(H, H + tm)
    all_cols = slice(0, 128)
    p2 = level(ubuf, slice(0, 128), 1, 0)
    lvl_a[lo:hi] = p2
    p4 = level(lvl_a, all_cols, 1, 1)
    sums = [(p2[tile_rows], p4[tile_rows])]
    p2 = level(ubuf, slice(128, 256), 1, 0)
    lvl_a[lo:hi] = p2
    lvl_b[lo:hi] = level(lvl_a, all_cols, 1, 1)
    p8 = level(lvl_b, all_cols, 2, 2)
    lvl_a[lo:hi] = p8
    p16 = level(lvl_a, all_cols, 4, 4)
    sums.append((p8[tile_rows], p16[tile_rows]))

    pos = i * tm + lax.broadcasted_iota(jnp.int32, (tm, 128), 0)
    lane = lax.broadcasted_iota(jnp.int32, (1, 128), 1)
    cnt = lambda w: (jnp.minimum(pos + w // 2, seq_len) - jnp.maximum(pos - w // 2, 0)).astype(F32)
    pooled = [jnp.where(lane < POOL_GROUP_DIM, small / cnt(ws), big / cnt(wb))
              for (small, big), (ws, wb) in zip(sums, ((2, 4), (8, 16)))]
    pooled = jnp.concatenate(pooled, axis=1) - u
    o_pool = _dot(pooled.astype(BF16), wpool_ref[...]) * pscale_ref[...]

    og = gf_ref[0].astype(F32) + gb_ref[0].astype(F32)
    sq = og * og
    sq_hi = sq.astype(BF16)
    sq_lo = (sq - sq_hi.astype(F32)).astype(BF16)
    ms = (_dot(sq_hi, hsum_ref[...]) + _dot(sq_lo, hsum_ref[...])) * (1.0 / GLA_DV)
    o_gla = og * lax.rsqrt(ms + NORM_EPS) * gng_ref[...]

    m_mla = (omla_ref[0].astype(F32) * _silu(gate_mla)).astype(BF16)
    m_pool = (o_pool * _silu(gate_pool)).astype(BF16)
    m_gla = (o_gla * _silu(gate_gla)).astype(BF16)
    return (x_ref[0] + _dot(m_mla, wout_ref[0:MLA_WIDTH])
            + _dot(m_pool, wout_ref[MLA_WIDTH:MLA_WIDTH + POOL_WIDTH])
            + _dot(m_gla, wout_ref[MLA_WIDTH + POOL_WIDTH:]))


def _mix(x, omla, zgate, upool, gf, gb, lw, fng=None, next_lw=None, tab=None):
    B, S, _ = x.shape
    tm = min(ROW_TILE, S)
    hb = tm // POOL_HALO
    nhalo = S // POOL_HALO
    tile = lambda w: pl.BlockSpec((1, tm, w), lambda b, i: (b, i, 0))
    weights = (lw["wpool"], lw["pool_scale"], lw["gla_norm_g"], lw["hsum"], lw["wout"])
    inputs = (x, omla, zgate, upool, upool, upool, gf, gb) + weights
    assert len(inputs) == N_MIX_INPUTS
    in_specs = [
        tile(D_MODEL), tile(MLA_WIDTH), tile(ZGATE_WIDTH), tile(POOL_WIDTH),
        pl.BlockSpec((1, POOL_HALO, POOL_WIDTH), lambda b, i: (b, jnp.maximum(i * hb - 1, 0), 0)),
        pl.BlockSpec((1, POOL_HALO, POOL_WIDTH), lambda b, i: (b, jnp.minimum((i + 1) * hb, nhalo - 1), 0)),
        tile(GLA_WIDTH), tile(GLA_WIDTH),
    ] + [_full_spec(w) for w in weights]
    out_specs = [tile(D_MODEL)]
    out_shape = [jax.ShapeDtypeStruct((B, S, D_MODEL), F32)]
    if next_lw is None:
        body, name = _mix_final_kernel, "mix_final"
        inputs += (fng,)
        in_specs.append(_full_spec(fng))
    else:
        body, name = _mix_proj_kernel, "mix_proj"
        operands, op_specs = _proj_operands(next_lw, tab, tm)
        inputs += operands
        in_specs += op_specs
        proj_specs, proj_shapes = _proj_outputs(B, S, tm)
        out_specs += proj_specs
        out_shape += proj_shapes
    return pl.pallas_call(
        functools.partial(body, seq_len=S),
        grid=(B, S // tm),
        in_specs=in_specs,
        out_specs=out_specs,
        out_shape=out_shape,
        scratch_shapes=[pltpu.VMEM((tm + 4 * POOL_HALO, POOL_WIDTH), F32),
                        pltpu.VMEM((tm + 4 * POOL_HALO, 128), F32),
                        pltpu.VMEM((tm + 4 * POOL_HALO, 128), F32)],
        compiler_params=pltpu.CompilerParams(
            dimension_semantics=("parallel", "parallel"), vmem_limit_bytes=VMEM_LIMIT),
        name=name,
    )(*inputs)


def _rope_table(seq):
    inv_freq = 1.0 / (ROPE_BASE ** (jnp.arange(0, MLA_ROPE, 2, dtype=F32) / MLA_ROPE))
    ang = jnp.arange(seq, dtype=F32)[:, None] * inv_freq[None, :]
    cos, sin = jnp.cos(ang), jnp.sin(ang)
    return jnp.concatenate([cos, cos, -sin, sin], axis=1)


def _swap_halves(w):
    half = w.shape[-1] // 2
    return jnp.concatenate([w[..., half:], w[..., :half]], axis=-1)


def _layer_weights(l, norm_g, w_in, q_norm_g, w_uq, kv_norm_g, w_ukv, w_pool, pool_scale,
                   gk_up_fwd, gk_bias_fwd, gk_up_bwd, gk_bias_bwd, gla_norm_g, w_out):
    cols = [w_in[l][:, IN_OFFS[j]:IN_OFFS[j + 1]] for j in range(len(IN_SIZES))]
    (c_q, c_kv, k_rope, gate_mla, u_pool, gate_pool, q_gla, k_gla, v_gla, lr_f, lr_b, gate_gla) = cols
    lr_pad = jnp.zeros((D_MODEL, 128 - 2 * GLA_GATE_RANK), F32)
    wa = jnp.concatenate([c_q, c_kv, k_rope, _swap_halves(k_rope)], axis=1)
    wgate = jnp.concatenate([gate_mla, gate_pool, gate_gla], axis=1)
    wgla = jnp.concatenate([q_gla, k_gla, v_gla, lr_f, lr_b, lr_pad], axis=1)
    uq = w_uq[l].reshape(MLA_Q_LORA, MLA_HEADS, MLA_NOPE + MLA_ROPE)
    uq_rope = uq[..., MLA_NOPE:]
    wuq = jnp.concatenate([uq, _swap_halves(uq_rope)], axis=-1).reshape(MLA_Q_LORA, MLA_HEADS * QK_WIDTH)
    ukv = w_ukv[l].reshape(MLA_KV_LORA, MLA_HEADS, MLA_NOPE + MLA_VDIM)
    wuk = ukv[..., :MLA_NOPE].reshape(MLA_KV_LORA, MLA_HEADS * MLA_NOPE)
    wuvt = ukv[..., MLA_NOPE:].reshape(MLA_KV_LORA, MLA_HEADS * MLA_VDIM).T
    wpool = jnp.zeros((POOL_WIDTH, POOL_WIDTH), F32)
    for gi in range(len(POOL_WINDOWS)):
        sl = slice(gi * POOL_GROUP_DIM, (gi + 1) * POOL_GROUP_DIM)
        wpool = wpool.at[sl, sl].set(w_pool[l, gi])
    up_pad = jnp.zeros((128 - GLA_GATE_RANK, GLA_KEY), F32)
    up_f = jnp.concatenate([gk_up_fwd[l], up_pad], axis=0)
    up_b = jnp.concatenate([up_pad[:GLA_GATE_RANK], gk_up_bwd[l], up_pad[GLA_GATE_RANK:]], axis=0)
    head = np.arange(GLA_WIDTH) // GLA_DV
    hsum = jnp.asarray(head[:, None] == head[None, :], BF16)
    return {
        "norm_g": norm_g[l][None, :],
        "wa": wa.astype(BF16), "wgate": wgate.astype(BF16), "wpu": u_pool.astype(BF16),"wgla": wgla.astype(BF16),
        "q_norm_g": q_norm_g[l][None, :], "wuq": wuq.astype(BF16),
        "kv_norm_g": kv_norm_g[l][None, :], "wuk": wuk.astype(BF16), "wuvt": wuvt.astype(BF16),
        "wpool": wpool.astype(BF16), "pool_scale": pool_scale[l][None, :],
        "up_f": up_f.astype(BF16), "bias_f": gk_bias_fwd[l][None, :],
        "up_b": up_b.astype(BF16), "bias_b": gk_bias_bwd[l][None, :],
        "gla_norm_g": jnp.tile(gla_norm_g[l], GLA_HEADS)[None, :], "hsum": hsum,
        "wout": w_out[l].astype(BF16),
    }


def _trunk(x, layers, fng):
    tab = _rope_table(x.shape[1])
    q, k, vt, zgate, upool, zgla = _proj(x, layers[0], tab)
    for l, lw in enumerate(layers):
        omla = _attn(q, k, vt)
        gf, gb = _gla(zgla, lw)
        if l + 1 < len(layers):
            x, q, k, vt, zgate, upool, zgla = _mix(x, omla, zgate, upool, gf, gb, lw,
                                                   next_lw=layers[l + 1], tab=tab)
        else:
            (x,) = _mix(x, omla, zgate, upool, gf, gb, lw, fng=fng)
    return x


def kernel(x_prompt, x_sample, norm_g, w_in, q_norm_g, w_uq, kv_norm_g, w_ukv, w_pool, pool_scale,
           gk_up_fwd, gk_bias_fwd, gk_up_bwd, gk_bias_bwd, gla_norm_g, w_out, final_norm_g):
    depth = w_in.shape[0]
    layers = [_layer_weights(l, norm_g, w_in, q_norm_g, w_uq, kv_norm_g, w_ukv, w_pool, pool_scale,
                             gk_up_fwd, gk_bias_fwd, gk_up_bwd, gk_bias_bwd, gla_norm_g, w_out)
              for l in range(depth)]
    fng = final_norm_g[None, :]
    return (_trunk(x_prompt, layers, fng), _trunk(x_sample, layers, fng))
```

```python
import functools
import math

import jax
import jax.numpy as jnp
import numpy as np
from jax import lax
from jax.experimental import pallas as pl
from jax.experimental.pallas import tpu as pltpu

F32 = jnp.float32
BF16 = jnp.bfloat16

D_MODEL = 1024
NORM_EPS = 1e-6
MLA_HEADS = 4
MLA_NOPE = 128
MLA_ROPE = 64
MLA_VDIM = 128
MLA_Q_LORA = 256
MLA_KV_LORA = 128
MLA_WIDTH = MLA_HEADS * MLA_VDIM
MLA_SCALE = (MLA_NOPE + MLA_ROPE) ** -0.5
ROPE_BASE = 10000.0
QK_WIDTH = 256
POOL_WINDOWS = (2, 4, 8, 16)
POOL_WIDTH = 256
POOL_GROUP_DIM = 64
POOL_HALO = 8
GLA_HEADS = 4
GLA_WIDTH = 256
GLA_KEY = 128
GLA_DK = 32
GLA_DV = 64
GLA_GATE_RANK = 16
GLA_GATE_NORM = 16.0
GLA_SUB = 32
GLA_TILE = 256

MIX_WIDTH = MLA_WIDTH + POOL_WIDTH + GLA_WIDTH
IN_SIZES = (MLA_Q_LORA, MLA_KV_LORA, MLA_ROPE, MLA_WIDTH, POOL_WIDTH, POOL_WIDTH,
            GLA_KEY, GLA_KEY, GLA_WIDTH, GLA_GATE_RANK, GLA_GATE_RANK, GLA_WIDTH)
IN_OFFS = tuple(int(v) for v in np.cumsum((0,) + IN_SIZES))

ZA_WIDTH = MLA_Q_LORA + MLA_KV_LORA + 2 * MLA_ROPE
ZGATE_WIDTH = MLA_WIDTH + POOL_WIDTH + GLA_WIDTH
ZGLA_WIDTH = 2 * GLA_KEY + GLA_WIDTH + 128

ROW_TILE = 512
ATTN_TQ = 512
ATTN_TK = 1024
VMEM_LIMIT = 56 * 1024 * 1024


def _rms(x, g):
    return x * lax.rsqrt(jnp.mean(x * x, axis=-1, keepdims=True) + NORM_EPS) * g


def _dot(a, b):
    return jnp.dot(a, b, preferred_element_type=F32)


def _dot_nt(a, b):
    return lax.dot_general(a, b, (((1,), (1,)), ((), ())), preferred_element_type=F32)


N_PROJ_OPERANDS = 12
N_PROJ_OUTS = 6


def _proj_kernel(x_ref, *refs):
    _proj_body(x_ref[0], *refs)


def _proj_body(x, ng_ref, wa_ref, wgate_ref, wpu_ref, wgla_ref, qg_ref, wuqt_ref, kvg_ref, wuk_ref,
               wuvt_ref, tab_ref, tabt_ref, qt_ref, k_ref, vt_ref, zgate_ref, upool_ref, zgla_ref):
    h = _rms(x, ng_ref[...]).astype(BF16)
    za = _dot(h, wa_ref[...])
    zgate_ref[0] = _dot(h, wgate_ref[...]).astype(BF16)
    upool_ref[0] = _dot(h, wpu_ref[...])
    zgla_ref[0] = _dot(h, wgla_ref[...]).astype(BF16)
    tab = tab_ref[...]
    tabt = tabt_ref[...]
    qscale = MLA_SCALE * math.log2(math.e)
    cqn = _rms(za[:, :MLA_Q_LORA], qg_ref[...]).astype(BF16)
    qall = _dot_nt(wuqt_ref[...], cqn)
    zeros = jnp.zeros((QK_WIDTH - MLA_NOPE - MLA_ROPE, qall.shape[1]), BF16)
    for hh in range(MLA_HEADS):
        base = hh * QK_WIDTH
        qn = qall[base:base + MLA_NOPE] * qscale
        t = qall[base + MLA_NOPE:base + QK_WIDTH] * tabt
        r = (t[:MLA_ROPE] + t[MLA_ROPE:]) * qscale
        qt_ref[0, hh, 0:MLA_NOPE] = qn.astype(BF16)
        qt_ref[0, hh, MLA_NOPE:MLA_NOPE + MLA_ROPE] = r.astype(BF16)
        qt_ref[0, hh, MLA_NOPE + MLA_ROPE:] = zeros
    ckvn = _rms(za[:, MLA_Q_LORA:MLA_Q_LORA + MLA_KV_LORA], kvg_ref[...]).astype(BF16)
    kn = _dot(ckvn, wuk_ref[...])
    t = za[:, MLA_Q_LORA + MLA_KV_LORA:] * tab
    kr = t + pltpu.roll(t, MLA_ROPE, 1)
    lane = lax.broadcasted_iota(jnp.int32, kr.shape, 1)
    kr = jnp.where(lane < MLA_ROPE, kr, 0.0).astype(BF16)
    for hh in range(MLA_HEADS):
        k_ref[0, hh, :, 0:MLA_NOPE] = kn[:, hh * MLA_NOPE:(hh + 1) * MLA_NOPE].astype(BF16)
        k_ref[0, hh, :, MLA_NOPE:QK_WIDTH] = kr
    vt = _dot_nt(wuvt_ref[...], ckvn)
    for hh in range(MLA_HEADS):
        vt_ref[0, hh] = vt[hh * MLA_VDIM:(hh + 1) * MLA_VDIM].astype(BF16)


def _full_spec(a):
    return pl.BlockSpec(a.shape, lambda b, i: (0,) * a.ndim)


def _proj_operands(lw, tab, tm):
    weights = (lw["norm_g"], lw["wa"], lw["wgate"], lw["wpu"], lw["wgla"], lw["q_norm_g"], lw["wuqt"],
               lw["kv_norm_g"], lw["wuk"], lw["wuvt"])
    specs = [_full_spec(w) for w in weights] + [pl.BlockSpec((tm, 128), lambda b, i: (i, 0)),
                                                pl.BlockSpec((128, tm), lambda b, i: (0, i))]
    operands = weights + tuple(tab)
    assert len(operands) == N_PROJ_OPERANDS
    return operands, specs


def _proj_outputs(B, S, tm):
    specs = [
        pl.BlockSpec((1, MLA_HEADS, QK_WIDTH, tm), lambda b, i: (b, 0, 0, i)),
        pl.BlockSpec((1, MLA_HEADS, tm, QK_WIDTH), lambda b, i: (b, 0, i, 0)),
        pl.BlockSpec((1, MLA_HEADS, MLA_VDIM, tm), lambda b, i: (b, 0, 0, i)),
        pl.BlockSpec((1, tm, ZGATE_WIDTH), lambda b, i: (b, i, 0)),
        pl.BlockSpec((1, tm, POOL_WIDTH), lambda b, i: (b, i, 0)),
        pl.BlockSpec((1, tm, ZGLA_WIDTH), lambda b, i: (b, i, 0)),
    ]
    shapes = [
        jax.ShapeDtypeStruct((B, MLA_HEADS, QK_WIDTH, S), BF16),
        jax.ShapeDtypeStruct((B, MLA_HEADS, S, QK_WIDTH), BF16),
        jax.ShapeDtypeStruct((B, MLA_HEADS, MLA_VDIM, S), BF16),
        jax.ShapeDtypeStruct((B, S, ZGATE_WIDTH), BF16),
        jax.ShapeDtypeStruct((B, S, POOL_WIDTH), F32),
        jax.ShapeDtypeStruct((B, S, ZGLA_WIDTH), BF16),
    ]
    assert len(specs) == N_PROJ_OUTS
    return specs, shapes


def _proj(x, lw, tab):
    B, S, _ = x.shape
    tm = min(ROW_TILE, S)
    operands, op_specs = _proj_operands(lw, tab, tm)
    out_specs, out_shape = _proj_outputs(B, S, tm)
    return pl.pallas_call(
        _proj_kernel,
        grid=(B, S // tm),
        in_specs=[pl.BlockSpec((1, tm, D_MODEL), lambda b, i: (b, i, 0))] + op_specs,
        out_specs=out_specs,
        out_shape=out_shape,
        compiler_params=pltpu.CompilerParams(
            dimension_semantics=("parallel", "parallel"), vmem_limit_bytes=VMEM_LIMIT),
        name="proj",
    )(x, *operands)


def _attn_kernel(qt_ref, k_ref, vt_ref, o_ref, sa_ref, sb_ref, *, tk, nk):
    qt = qt_ref[0, 0]
    tq = qt.shape[1]

    def scores(i, s_ref):
        start = pl.multiple_of(i * tk, tk)
        s = _dot(k_ref[0, 0, pl.ds(start, tk), :], qt)
        s_ref[:, 0:tq] = s
        return jnp.max(s, axis=0, keepdims=True)

    def update(i, s_ref, mx, m, l, acc):
        start = pl.multiple_of(i * tk, tk)
        m_new = jnp.maximum(m, mx)
        alpha = jnp.exp2(m - m_new)
        p = jnp.exp2(s_ref[:, 0:tq] - m_new)
        l = alpha * l + jnp.sum(p, axis=0, keepdims=True)
        acc = alpha * acc + _dot(vt_ref[0, 0, :, pl.ds(start, tk)], p.astype(BF16))
        return m_new, l, acc

    def body(j, carry):
        m, l, acc, mx_a = carry
        i = 2 * j
        mx_b = scores(i + 1, sb_ref)
        m, l, acc = update(i, sa_ref, mx_a, m, l, acc)
        mx_a = scores(i + 2, sa_ref)
        m, l, acc = update(i + 1, sb_ref, mx_b, m, l, acc)
        return m, l, acc, mx_a

    m = jnp.full((1, tq), -jnp.inf, F32)
    l = jnp.zeros((1, tq), F32)
    acc = jnp.zeros((MLA_VDIM, tq), F32)
    mx_a = scores(0, sa_ref)
    if nk > 1:
        m, l, acc, mx_a = lax.fori_loop(0, nk // 2 - 1, body, (m, l, acc, mx_a))
        mx_b = scores(nk - 1, sb_ref)
        m, l, acc = update(nk - 2, sa_ref, mx_a, m, l, acc)
        m, l, acc = update(nk - 1, sb_ref, mx_b, m, l, acc)
    else:
        m, l, acc = update(0, sa_ref, mx_a, m, l, acc)
    o_ref[0] = (acc / l).T.astype(o_ref.dtype)


def _attn(qt, k, vt):
    B, H, S, _ = k.shape
    tq = min(ATTN_TQ, S)
    tk = min(ATTN_TK, S)
    nk = S // tk
    assert nk == 1 or nk % 2 == 0
    return pl.pallas_call(
        functools.partial(_attn_kernel, tk=tk, nk=nk),
        grid=(B, H, S // tq),
        in_specs=[
            pl.BlockSpec((1, 1, QK_WIDTH, tq), lambda b, h, i: (b, h, 0, i)),
            pl.BlockSpec((1, 1, S, QK_WIDTH), lambda b, h, i: (b, h, 0, 0)),
            pl.BlockSpec((1, 1, MLA_VDIM, S), lambda b, h, i: (b, h, 0, 0)),
        ],
        out_specs=pl.BlockSpec((1, tq, MLA_VDIM), lambda b, h, i: (b, i, h)),
        out_shape=jax.ShapeDtypeStruct((B, S, MLA_WIDTH), BF16),
        scratch_shapes=[pltpu.VMEM((tk, tq + 128), F32), pltpu.VMEM((tk, tq + 128), F32)],
        compiler_params=pltpu.CompilerParams(
            dimension_semantics=("parallel", "parallel", "parallel"), vmem_limit_bytes=VMEM_LIMIT),
        name="attn",
    )(qt, k, vt)


def _split2(x):
    hi = x.astype(BF16)
    lo = (x - hi.astype(F32)).astype(BF16)
    return jnp.concatenate([hi, lo], axis=1)


def _sum2(y, w):
    return y[:, :w] + y[:, w:]


def _gla_tile(z, up, bias, pin, pmid, pex, state, reverse):
    T = z.shape[0]
    nsub = T // GLA_SUB
    q = z[:, 0:GLA_KEY].astype(F32) * (GLA_DK ** -0.5)
    k = z[:, GLA_KEY:2 * GLA_KEY].astype(F32)
    v = z[:, 2 * GLA_KEY:2 * GLA_KEY + GLA_WIDTH]
    lr = z[:, 2 * GLA_KEY + GLA_WIDTH:]
    zg = _dot(lr, up) + bias
    yield
    g = (jnp.minimum(zg, 0.0) - jnp.log1p(jnp.exp(-jnp.abs(zg)))) * (1.0 / GLA_GATE_NORM)
    g2 = _split2(g)
    b_in = _sum2(_dot(pin, g2), GLA_KEY)
    yield
    b_mid = _sum2(_dot(pmid, g2), GLA_KEY)
    yield
    b_ex = _sum2(_dot(pex, g2), GLA_KEY)
    yield
    qt = q * jnp.exp(b_in)
    qa = q * jnp.exp(b_in - b_mid)
    kt = (k * jnp.exp(b_mid - b_in)).astype(BF16)
    kp = k * jnp.exp(b_ex)
    dd = jnp.exp(b_in + b_ex)

    pmask = pin > 0
    qhead = lax.broadcasted_iota(jnp.int32, (1, GLA_KEY), 1) // GLA_DK
    vhead = lax.broadcasted_iota(jnp.int32, (1, GLA_WIDTH), 1) // GLA_DV
    a = []
    for hh in range(GLA_HEADS):
        qh = jnp.where(qhead == hh, qa, 0.0).astype(BF16)
        a.append(jnp.where(pmask, _dot_nt(qh, kt), 0.0).astype(BF16))
        yield

    kpt = kp.T
    ddt = dd.T
    sub_of_lane = lax.broadcasted_iota(jnp.int32, (1, T), 1) // GLA_SUB
    stack = jnp.concatenate(
        [jnp.where(sub_of_lane == c, kpt, 0.0).astype(BF16) for c in range(nsub)], axis=0)
    inc = _dot(stack, v)
    yield
    same_head = (lax.broadcasted_iota(jnp.int32, (GLA_KEY, 1), 0) // GLA_DK) == vhead

    o = jnp.zeros((T, GLA_WIDTH), F32)
    for hh in range(GLA_HEADS):
        o = o + _dot(a[hh], jnp.where(vhead == hh, v, jnp.zeros_like(v)))
        yield

    inter = [None] * nsub
    order = range(nsub - 1, -1, -1) if reverse else range(nsub)
    for c in order:
        rows = slice(c * GLA_SUB, (c + 1) * GLA_SUB)
        inter[c] = _dot(qt[rows].astype(BF16), state.astype(BF16))
        state = state * ddt[:, c * GLA_SUB:c * GLA_SUB + 1] + jnp.where(
            same_head, inc[c * GLA_KEY:(c + 1) * GLA_KEY], 0.0)
        yield
    return o + jnp.concatenate(inter, axis=0), state


def _lockstep(gens):
    results = [None] * len(gens)
    live = list(range(len(gens)))
    while live:
        for idx in list(live):
            try:
                next(gens[idx])
            except StopIteration as done:
                results[idx] = done.value
                live.remove(idx)
    return results


def _gla_kernel(zf_ref, zb_ref, upf_ref, biasf_ref, upb_ref, biasb_ref, pinf_ref, pmidf_ref, pexf_ref,
                pinb_ref, pmidb_ref, pexb_ref, of_ref, ob_ref, sf_ref, sb_ref):
    @pl.when(pl.program_id(1) == 0)
    def _():
        sf_ref[...] = jnp.zeros_like(sf_ref)
        sb_ref[...] = jnp.zeros_like(sb_ref)

    (of, sf), (ob, sb) = _lockstep([
        _gla_tile(zf_ref[0], upf_ref[...], biasf_ref[...], pinf_ref[...], pmidf_ref[...], pexf_ref[...],
                  sf_ref[...], False),
        _gla_tile(zb_ref[0], upb_ref[...], biasb_ref[...], pinb_ref[...], pmidb_ref[...], pexb_ref[...],
                  sb_ref[...], True)])
    of_ref[0] = of.astype(of_ref.dtype)
    ob_ref[0] = ob.astype(ob_ref.dtype)
    sf_ref[...] = sf
    sb_ref[...] = sb


def _gla_masks(T, reverse):
    t = np.arange(T)
    same = (t[:, None] // GLA_SUB) == (t[None, :] // GLA_SUB)
    before = (t[None, :] >= t[:, None]) if reverse else (t[None, :] <= t[:, None])
    pin = same & before
    pex = same & ~before
    in_sub = t[None, :] % GLA_SUB
    first_half = (in_sub >= GLA_SUB // 2) if reverse else (in_sub < GLA_SUB // 2)
    pmid = same & first_half
    return jnp.asarray(pin, BF16), jnp.asarray(pmid, BF16), jnp.asarray(pex, BF16)


def _gla(zgla, lw):
    B, S, _ = zgla.shape
    T = min(GLA_TILE, S)
    n = S // T
    consts = (lw["up_f"], lw["bias_f"], lw["up_b"], lw["bias_b"]) + _gla_masks(T, False) + _gla_masks(T, True)
    fwd = lambda b, i: (b, i, 0)
    bwd = lambda b, i: (b, n - 1 - i, 0)
    full = lambda a: pl.BlockSpec(a.shape, lambda b, i: (0,) * a.ndim)
    return pl.pallas_call(
        _gla_kernel,
        grid=(B, n),
        in_specs=[pl.BlockSpec((1, T, ZGLA_WIDTH), fwd), pl.BlockSpec((1, T, ZGLA_WIDTH), bwd)]
        + [full(c) for c in consts],
        out_specs=[pl.BlockSpec((1, T, GLA_WIDTH), fwd), pl.BlockSpec((1, T, GLA_WIDTH), bwd)],
        out_shape=[jax.ShapeDtypeStruct((B, S, GLA_WIDTH), BF16)] * 2,
        scratch_shapes=[pltpu.VMEM((GLA_KEY, GLA_WIDTH), F32)] * 2,
        compiler_params=pltpu.CompilerParams(
            dimension_semantics=("parallel", "arbitrary"), vmem_limit_bytes=VMEM_LIMIT),
        name="gla",
    )(zgla, zgla, *consts)


def _silu(x):
    h = 0.5 * x
    return h + h * jnp.tanh(h)


N_MIX_INPUTS = 13


def _mix_final_kernel(*refs, seq_len):
    ins, (fng_ref, out_ref), scratch = refs[:N_MIX_INPUTS], refs[N_MIX_INPUTS:N_MIX_INPUTS + 2], refs[-3:]
    out_ref[0] = _rms(_mix_body(*ins, *scratch, seq_len=seq_len), fng_ref[...])


def _mix_proj_kernel(*refs, seq_len):
    ins, scratch = refs[:N_MIX_INPUTS], refs[-3:]
    proj_ops = refs[N_MIX_INPUTS:N_MIX_INPUTS + N_PROJ_OPERANDS]
    out_ref = refs[N_MIX_INPUTS + N_PROJ_OPERANDS]
    proj_outs = refs[N_MIX_INPUTS + N_PROJ_OPERANDS + 1:-3]
    y = _mix_body(*ins, *scratch, seq_len=seq_len)
    out_ref[0] = y
    _proj_body(y, *proj_ops, *proj_outs)


def _mix_body(x_ref, omla_ref, zgate_ref, u_ref, uprev_ref, unext_ref, gf_ref, gb_ref, wpool_ref, pscale_ref,
              gng_ref, hsum_ref, wout_ref, ubuf, lvl_a, lvl_b, *, seq_len):
    tm = x_ref.shape[1]
    i = pl.program_id(1)
    zg = zgate_ref[0].astype(F32)
    gate_mla = zg[:, 0:MLA_WIDTH]
    gate_pool = zg[:, MLA_WIDTH:MLA_WIDTH + POOL_WIDTH]
    gate_gla = zg[:, MLA_WIDTH + POOL_WIDTH:]
    u = u_ref[0]

    H = POOL_HALO
    lo, hi = H, tm + 3 * H
    pad = jnp.zeros((H, POOL_WIDTH), F32)
    for buf in (ubuf, lvl_a, lvl_b):
        buf[0:H] = pad[:, 0:buf.shape[1]]
        buf[hi:] = pad[:, 0:buf.shape[1]]
    ubuf[lo:2 * H] = jnp.where(i > 0, uprev_ref[0], 0.0)
    ubuf[2 * H:2 * H + tm] = u
    ubuf[2 * H + tm:hi] = jnp.where(i < pl.num_programs(1) - 1, unext_ref[0], 0.0)

    def level(src, cols, before, after):
        return src[lo - before:hi - before, cols] + src[lo + after:hi + after, cols]

    tile_rows = slice(H, H + tm)
    all_cols = slice(0, 128)
    p2 = level(ubuf, slice(0, 128), 1, 0)
    lvl_a[lo:hi] = p2
    p4 = level(lvl_a, all_cols, 1, 1)
    sums = [(p2[tile_rows], p4[tile_rows])]
    p2 = level(ubuf, slice(128, 256), 1, 0)
    lvl_a[lo:hi] = p2
    lvl_b[lo:hi] = level(lvl_a, all_cols, 1, 1)
    p8 = level(lvl_b, all_cols, 2, 2)
    lvl_a[lo:hi] = p8
    p16 = level(lvl_a, all_cols, 4, 4)
    sums.append((p8[tile_rows], p16[tile_rows]))

    pos = i * tm + lax.broadcasted_iota(jnp.int32, (tm, 128), 0)
    lane = lax.broadcasted_iota(jnp.int32, (1, 128), 1)
    cnt = lambda w: (jnp.minimum(pos + w // 2, seq_len) - jnp.maximum(pos - w // 2, 0)).astype(F32)
    pooled = [jnp.where(lane < POOL_GROUP_DIM, small / cnt(ws), big / cnt(wb))
              for (small, big), (ws, wb) in zip(sums, ((2, 4), (8, 16)))]
    pooled = jnp.concatenate(pooled, axis=1) - u
    o_pool = _dot(pooled.astype(BF16), wpool_ref[...]) * pscale_ref[...]

    og = gf_ref[0].astype(F32) + gb_ref[0].astype(F32)
    sq = og * og
    sq_hi = sq.astype(BF16)
    sq_lo = (sq - sq_hi.astype(F32)).astype(BF16)
    ms = (_dot(sq_hi, hsum_ref[...]) + _dot(sq_lo, hsum_ref[...])) * (1.0 / GLA_DV)
    o_gla = og * lax.rsqrt(ms + NORM_EPS) * gng_ref[...]

    m_mla = (omla_ref[0].astype(F32) * _silu(gate_mla)).astype(BF16)
    m_pool = (o_pool * _silu(gate_pool)).astype(BF16)
    m_gla = (o_gla * _silu(gate_gla)).astype(BF16)
    return (x_ref[0] + _dot(m_mla, wout_ref[0:MLA_WIDTH])
            + _dot(m_pool, wout_ref[MLA_WIDTH:MLA_WIDTH + POOL_WIDTH])
            + _dot(m_gla, wout_ref[MLA_WIDTH + POOL_WIDTH:]))


def _mix(x, omla, zgate, upool, gf, gb, lw, fng=None, next_lw=None, tab=None):
    B, S, _ = x.shape
    tm = min(ROW_TILE, S)
    hb = tm // POOL_HALO
    nhalo = S // POOL_HALO
    tile = lambda w: pl.BlockSpec((1, tm, w), lambda b, i: (b, i, 0))
    weights = (lw["wpool"], lw["pool_scale"], lw["gla_norm_g"], lw["hsum"], lw["wout"])
    inputs = (x, omla, zgate, upool, upool, upool, gf, gb) + weights
    assert len(inputs) == N_MIX_INPUTS
    in_specs = [
        tile(D_MODEL), tile(MLA_WIDTH), tile(ZGATE_WIDTH), tile(POOL_WIDTH),
        pl.BlockSpec((1, POOL_HALO, POOL_WIDTH), lambda b, i: (b, jnp.maximum(i * hb - 1, 0), 0)),
        pl.BlockSpec((1, POOL_HALO, POOL_WIDTH), lambda b, i: (b, jnp.minimum((i + 1) * hb, nhalo - 1), 0)),
        tile(GLA_WIDTH), tile(GLA_WIDTH),
    ] + [_full_spec(w) for w in weights]
    out_specs = [tile(D_MODEL)]
    out_shape = [jax.ShapeDtypeStruct((B, S, D_MODEL), F32)]
    if next_lw is None:
        body, name = _mix_final_kernel, "mix_final"
        inputs += (fng,)
        in_specs.append(_full_spec(fng))
    else:
        body, name = _mix_proj_kernel, "mix_proj"
        operands, op_specs = _proj_operands(next_lw, tab, tm)
        inputs += operands
        in_specs += op_specs
        proj_specs, proj_shapes = _proj_outputs(B, S, tm)
        out_specs += proj_specs
        out_shape += proj_shapes
    return pl.pallas_call(
        functools.partial(body, seq_len=S),
        grid=(B, S // tm),
        in_specs=in_specs,
        out_specs=out_specs,
        out_shape=out_shape,
        scratch_shapes=[pltpu.VMEM((tm + 4 * POOL_HALO, POOL_WIDTH), F32),
                        pltpu.VMEM((tm + 4 * POOL_HALO, 128), F32),
                        pltpu.VMEM((tm + 4 * POOL_HALO, 128), F32)],
        compiler_params=pltpu.CompilerParams(
            dimension_semantics=("parallel", "parallel"), vmem_limit_bytes=VMEM_LIMIT),
        name=name,
    )(*inputs)


def _rope_table(seq):
    inv_freq = 1.0 / (ROPE_BASE ** (jnp.arange(0, MLA_ROPE, 2, dtype=F32) / MLA_ROPE))
    ang = jnp.arange(seq, dtype=F32)[:, None] * inv_freq[None, :]
    cos, sin = jnp.cos(ang), jnp.sin(ang)
    return jnp.concatenate([cos, cos, -sin, sin], axis=1)


def _swap_halves(w):
    half = w.shape[-1] // 2
    return jnp.concatenate([w[..., half:], w[..., :half]], axis=-1)


def _layer_weights(l, norm_g, w_in, q_norm_g, w_uq, kv_norm_g, w_ukv, w_pool, pool_scale,
                   gk_up_fwd, gk_bias_fwd, gk_up_bwd, gk_bias_bwd, gla_norm_g, w_out):
    cols = [w_in[l][:, IN_OFFS[j]:IN_OFFS[j + 1]] for j in range(len(IN_SIZES))]
    (c_q, c_kv, k_rope, gate_mla, u_pool, gate_pool, q_gla, k_gla, v_gla, lr_f, lr_b, gate_gla) = cols
    lr_pad = jnp.zeros((D_MODEL, 128 - 2 * GLA_GATE_RANK), F32)
    wa = jnp.concatenate([c_q, c_kv, k_rope, _swap_halves(k_rope)], axis=1)
    wgate = jnp.concatenate([gate_mla, gate_pool, gate_gla], axis=1)
    wgla = jnp.concatenate([q_gla, k_gla, v_gla, lr_f, lr_b, lr_pad], axis=1)
    uq = w_uq[l].reshape(MLA_Q_LORA, MLA_HEADS, MLA_NOPE + MLA_ROPE)
    uq_rope = uq[..., MLA_NOPE:]
    wuq = jnp.concatenate([uq, _swap_halves(uq_rope)], axis=-1).reshape(MLA_Q_LORA, MLA_HEADS * QK_WIDTH)
    ukv = w_ukv[l].reshape(MLA_KV_LORA, MLA_HEADS, MLA_NOPE + MLA_VDIM)
    wuk = ukv[..., :MLA_NOPE].reshape(MLA_KV_LORA, MLA_HEADS * MLA_NOPE)
    wuvt = ukv[..., MLA_NOPE:].reshape(MLA_KV_LORA, MLA_HEADS * MLA_VDIM).T
    wpool = jnp.zeros((POOL_WIDTH, POOL_WIDTH), F32)
    for gi in range(len(POOL_WINDOWS)):
        sl = slice(gi * POOL_GROUP_DIM, (gi + 1) * POOL_GROUP_DIM)
        wpool = wpool.at[sl, sl].set(w_pool[l, gi])
    up_pad = jnp.zeros((128 - GLA_GATE_RANK, GLA_KEY), F32)
    up_f = jnp.concatenate([gk_up_fwd[l], up_pad], axis=0)
    up_b = jnp.concatenate([up_pad[:GLA_GATE_RANK], gk_up_bwd[l], up_pad[GLA_GATE_RANK:]], axis=0)
    head = np.arange(GLA_WIDTH) // GLA_DV
    hsum = jnp.asarray(head[:, None] == head[None, :], BF16)
    return {
        "norm_g": norm_g[l][None, :],
        "wa": wa.astype(BF16), "wgate": wgate.astype(BF16), "wpu": u_pool.astype(BF16),"wgla": wgla.astype(BF16),
        "q_norm_g": q_norm_g[l][None, :], "wuqt": wuq.T.astype(BF16),
        "kv_norm_g": kv_norm_g[l][None, :], "wuk": wuk.astype(BF16), "wuvt": wuvt.astype(BF16),
        "wpool": wpool.astype(BF16), "pool_scale": pool_scale[l][None, :],
        "up_f": up_f.astype(BF16), "bias_f": gk_bias_fwd[l][None, :],
        "up_b": up_b.astype(BF16), "bias_b": gk_bias_bwd[l][None, :],
        "gla_norm_g": jnp.tile(gla_norm_g[l], GLA_HEADS)[None, :], "hsum": hsum,
        "wout": w_out[l].astype(BF16),
    }


def _trunk(x, layers, fng):
    tab = _rope_table(x.shape[1])
    tab = (tab, tab.T)
    q, k, vt, zgate, upool, zgla = _proj(x, layers[0], tab)
    for l, lw in enumerate(layers):
        omla = _attn(q, k, vt)
        gf, gb = _gla(zgla, lw)
        if l + 1 < len(layers):
            x, q, k, vt, zgate, upool, zgla = _mix(x, omla, zgate, upool, gf, gb, lw,
                                                   next_lw=layers[l + 1], tab=tab)
        else:
            (x,) = _mix(x, omla, zgate, upool, gf, gb, lw, fng=fng)
    return x


def kernel(x_prompt, x_sample, norm_g, w_in, q_norm_g, w_uq, kv_norm_g, w_ukv, w_pool, pool_scale,
           gk_up_fwd, gk_bias_fwd, gk_up_bwd, gk_bias_bwd, gla_norm_g, w_out, final_norm_g):
    depth = w_in.shape[0]
    layers = [_layer_weights(l, norm_g, w_in, q_norm_g, w_uq, kv_norm_g, w_ukv, w_pool, pool_scale,
                             gk_up_fwd, gk_bias_fwd, gk_up_bwd, gk_bias_bwd, gla_norm_g, w_out)
              for l in range(depth)]
    fng = final_norm_g[None, :]
    return (_trunk(x_prompt, layers, fng), _trunk(x_sample, layers, fng))
```

```python
import functools
import math

import jax
import jax.numpy as jnp
import numpy as np
from jax import lax
from jax.experimental import pallas as pl
from jax.experimental.pallas import tpu as pltpu

F32 = jnp.float32
BF16 = jnp.bfloat16

D_MODEL = 1024
NORM_EPS = 1e-6
MLA_HEADS = 4
MLA_NOPE = 128
MLA_ROPE = 64
MLA_VDIM = 128
MLA_Q_LORA = 256
MLA_KV_LORA = 128
MLA_WIDTH = MLA_HEADS * MLA_VDIM
MLA_SCALE = (MLA_NOPE + MLA_ROPE) ** -0.5
ROPE_BASE = 10000.0
QK_WIDTH = 256
POOL_WINDOWS = (2, 4, 8, 16)
POOL_WIDTH = 256
POOL_GROUP_DIM = 64
POOL_HALO = 8
GLA_HEADS = 4
GLA_WIDTH = 256
GLA_KEY = 128
GLA_DK = 32
GLA_DV = 64
GLA_GATE_RANK = 16
GLA_GATE_NORM = 16.0
GLA_SUB = 32
GLA_TILE = 256
GLA_HALF = 128

MIX_WIDTH = MLA_WIDTH + POOL_WIDTH + GLA_WIDTH
IN_SIZES = (MLA_Q_LORA, MLA_KV_LORA, MLA_ROPE, MLA_WIDTH, POOL_WIDTH, POOL_WIDTH,
            GLA_KEY, GLA_KEY, GLA_WIDTH, GLA_GATE_RANK, GLA_GATE_RANK, GLA_WIDTH)
IN_OFFS = tuple(int(v) for v in np.cumsum((0,) + IN_SIZES))

ZA_WIDTH = MLA_Q_LORA + MLA_KV_LORA + 2 * MLA_ROPE
ZGATE_WIDTH = MLA_WIDTH + POOL_WIDTH + GLA_WIDTH
ZGLA_WIDTH = 2 * GLA_KEY + GLA_WIDTH
GLA_LR_PAD = 128

ROW_TILE = 512
ATTN_TQ = 512
ATTN_TK = 1024
VMEM_LIMIT = 56 * 1024 * 1024


def _rms(x, g):
    return x * lax.rsqrt(jnp.mean(x * x, axis=-1, keepdims=True) + NORM_EPS) * g


def _dot(a, b):
    return jnp.dot(a, b, preferred_element_type=F32)


def _dot_nt(a, b):
    return lax.dot_general(a, b, (((1,), (1,)), ((), ())), preferred_element_type=F32)


N_PROJ_OPERANDS = 14
N_PROJ_OUTS = 7


def _proj_kernel(x_ref, *refs):
    _proj_body(x_ref[0], *refs)


def _proj_body(x, ng_ref, wa_ref, wgate_ref, wpu_ref, wgla_ref, qg_ref, wuqt_ref, kvg_ref, wuk_ref,
               wuvt_ref, gup_ref, gbias_ref, tab_ref, tabt_ref,
               qt_ref, k_ref, vt_ref, zgate_ref, upool_ref, zgla_ref, glog_ref):
    h = _rms(x, ng_ref[...]).astype(BF16)
    za = _dot(h, wa_ref[...])
    zgate_ref[0] = _dot(h, wgate_ref[...]).astype(BF16)
    upool_ref[0] = _dot(h, wpu_ref[...])
    zgla = _dot(h, wgla_ref[...])
    zgla_ref[0] = zgla[:, :ZGLA_WIDTH].astype(BF16)
    zg = _dot(zgla[:, ZGLA_WIDTH:].astype(BF16), gup_ref[...]) + gbias_ref[...]
    glog_ref[0] = (jnp.minimum(zg, 0.0) - jnp.log1p(jnp.exp(-jnp.abs(zg)))) * (1.0 / GLA_GATE_NORM)
    tab = tab_ref[...]
    tabt = tabt_ref[...]
    qscale = MLA_SCALE * math.log2(math.e)
    cqn = _rms(za[:, :MLA_Q_LORA], qg_ref[...]).astype(BF16)
    qall = _dot_nt(wuqt_ref[...], cqn)
    zeros = jnp.zeros((QK_WIDTH - MLA_NOPE - MLA_ROPE, qall.shape[1]), BF16)
    for hh in range(MLA_HEADS):
        base = hh * QK_WIDTH
        qn = qall[base:base + MLA_NOPE] * qscale
        t = qall[base + MLA_NOPE:base + QK_WIDTH] * tabt
        r = (t[:MLA_ROPE] + t[MLA_ROPE:]) * qscale
        qt_ref[0, hh, 0:MLA_NOPE] = qn.astype(BF16)
        qt_ref[0, hh, MLA_NOPE:MLA_NOPE + MLA_ROPE] = r.astype(BF16)
        qt_ref[0, hh, MLA_NOPE + MLA_ROPE:] = zeros
    ckvn = _rms(za[:, MLA_Q_LORA:MLA_Q_LORA + MLA_KV_LORA], kvg_ref[...]).astype(BF16)
    kn = _dot(ckvn, wuk_ref[...])
    t = za[:, MLA_Q_LORA + MLA_KV_LORA:] * tab
    kr = t + pltpu.roll(t, MLA_ROPE, 1)
    lane = lax.broadcasted_iota(jnp.int32, kr.shape, 1)
    kr = jnp.where(lane < MLA_ROPE, kr, 0.0).astype(BF16)
    for hh in range(MLA_HEADS):
        k_ref[0, hh, :, 0:MLA_NOPE] = kn[:, hh * MLA_NOPE:(hh + 1) * MLA_NOPE].astype(BF16)
        k_ref[0, hh, :, MLA_NOPE:QK_WIDTH] = kr
    vt = _dot_nt(wuvt_ref[...], ckvn)
    for hh in range(MLA_HEADS):
        vt_ref[0, hh] = vt[hh * MLA_VDIM:(hh + 1) * MLA_VDIM].astype(BF16)


def _full_spec(a):
    return pl.BlockSpec(a.shape, lambda b, i: (0,) * a.ndim)


def _proj_operands(lw, tab, tm):
    weights = (lw["norm_g"], lw["wa"], lw["wgate"], lw["wpu"], lw["wgla"], lw["q_norm_g"], lw["wuqt"],
               lw["kv_norm_g"], lw["wuk"], lw["wuvt"], lw["gup"], lw["gbias"])
    specs = [_full_spec(w) for w in weights] + [pl.BlockSpec((tm, 128), lambda b, i: (i, 0)),
                                                pl.BlockSpec((128, tm), lambda b, i: (0, i))]
    operands = weights + tuple(tab)
    assert len(operands) == N_PROJ_OPERANDS
    return operands, specs


def _proj_outputs(B, S, tm):
    specs = [
        pl.BlockSpec((1, MLA_HEADS, QK_WIDTH, tm), lambda b, i: (b, 0, 0, i)),
        pl.BlockSpec((1, MLA_HEADS, tm, QK_WIDTH), lambda b, i: (b, 0, i, 0)),
        pl.BlockSpec((1, MLA_HEADS, MLA_VDIM, tm), lambda b, i: (b, 0, 0, i)),
        pl.BlockSpec((1, tm, ZGATE_WIDTH), lambda b, i: (b, i, 0)),
        pl.BlockSpec((1, tm, POOL_WIDTH), lambda b, i: (b, i, 0)),
        pl.BlockSpec((1, tm, ZGLA_WIDTH), lambda b, i: (b, i, 0)),
        pl.BlockSpec((1, tm, 2 * GLA_KEY), lambda b, i: (b, i, 0)),
    ]
    shapes = [
        jax.ShapeDtypeStruct((B, MLA_HEADS, QK_WIDTH, S), BF16),
        jax.ShapeDtypeStruct((B, MLA_HEADS, S, QK_WIDTH), BF16),
        jax.ShapeDtypeStruct((B, MLA_HEADS, MLA_VDIM, S), BF16),
        jax.ShapeDtypeStruct((B, S, ZGATE_WIDTH), BF16),
        jax.ShapeDtypeStruct((B, S, POOL_WIDTH), F32),
        jax.ShapeDtypeStruct((B, S, ZGLA_WIDTH), BF16),
        jax.ShapeDtypeStruct((B, S, 2 * GLA_KEY), F32),
    ]
    assert len(specs) == N_PROJ_OUTS
    return specs, shapes


def _proj(x, lw, tab):
    B, S, _ = x.shape
    tm = min(ROW_TILE, S)
    operands, op_specs = _proj_operands(lw, tab, tm)
    out_specs, out_shape = _proj_outputs(B, S, tm)
    return pl.pallas_call(
        _proj_kernel,
        grid=(B, S // tm),
        in_specs=[pl.BlockSpec((1, tm, D_MODEL), lambda b, i: (b, i, 0))] + op_specs,
        out_specs=out_specs,
        out_shape=out_shape,
        compiler_params=pltpu.CompilerParams(
            dimension_semantics=("parallel", "parallel"), vmem_limit_bytes=VMEM_LIMIT),
        name="proj",
    )(x, *operands)


def _attn_kernel(qt_ref, qn_ref, k_ref, vt_ref, o_ref, sa_ref, sb_ref, mxa_ref, *, tk, nk):
    qt = qt_ref[0, 0]
    tq = qt.shape[1]

    def scores(i, q, s_ref):
        start = pl.multiple_of(i * tk, tk)
        s = _dot(k_ref[0, 0, pl.ds(start, tk), :], q)
        s_ref[:, 0:tq] = s
        return jnp.max(s, axis=0, keepdims=True)

    def update(i, s_ref, mx, m, l, acc):
        start = pl.multiple_of(i * tk, tk)
        m_new = jnp.maximum(m, mx)
        alpha = jnp.exp2(m - m_new)
        p = jnp.exp2(s_ref[:, 0:tq] - m_new)
        l = alpha * l + jnp.sum(p, axis=0, keepdims=True)
        acc = alpha * acc + _dot(vt_ref[0, 0, :, pl.ds(start, tk)], p.astype(BF16))
        return m_new, l, acc

    @pl.when(pl.program_id(2) == 0)
    def _():
        mxa_ref[...] = scores(0, qt, sa_ref)

    def pair(i, next_chunk, next_q, carry):
        m, l, acc, mx_a = carry
        mx_b = scores(i + 1, qt, sb_ref)
        m, l, acc = update(i, sa_ref, mx_a, m, l, acc)
        mx_a = scores(next_chunk, next_q, sa_ref)
        m, l, acc = update(i + 1, sb_ref, mx_b, m, l, acc)
        return m, l, acc, mx_a

    m = jnp.full((1, tq), -jnp.inf, F32)
    l = jnp.zeros((1, tq), F32)
    acc = jnp.zeros((MLA_VDIM, tq), F32)
    carry = lax.fori_loop(0, nk // 2 - 1, lambda j, c: pair(2 * j, 2 * j + 2, qt, c), (m, l, acc, mxa_ref[...]))
    m, l, acc, mx_a = pair(nk - 2, 0, qn_ref[0, 0], carry)
    mxa_ref[...] = mx_a
    o_ref[0] = (acc / l).T.astype(o_ref.dtype)


def _attn(qt, k, vt):
    B, H, S, _ = k.shape
    tq = min(ATTN_TQ, S)
    tk = min(ATTN_TK, S // 2)
    nk = S // tk
    nq = S // tq
    assert nk % 2 == 0
    return pl.pallas_call(
        functools.partial(_attn_kernel, tk=tk, nk=nk),
        grid=(B, H, nq),
        in_specs=[
            pl.BlockSpec((1, 1, QK_WIDTH, tq), lambda b, h, i: (b, h, 0, i)),
            pl.BlockSpec((1, 1, QK_WIDTH, tq), lambda b, h, i: (b, h, 0, jnp.minimum(i + 1, nq - 1))),
            pl.BlockSpec((1, 1, S, QK_WIDTH), lambda b, h, i: (b, h, 0, 0)),
            pl.BlockSpec((1, 1, MLA_VDIM, S), lambda b, h, i: (b, h, 0, 0)),
        ],
        out_specs=pl.BlockSpec((1, tq, MLA_VDIM), lambda b, h, i: (b, i, h)),
        out_shape=jax.ShapeDtypeStruct((B, S, MLA_WIDTH), BF16),
        scratch_shapes=[pltpu.VMEM((tk, tq + 128), F32), pltpu.VMEM((tk, tq + 128), F32),
                        pltpu.VMEM((1, tq), F32)],
        compiler_params=pltpu.CompilerParams(
            dimension_semantics=("parallel", "parallel", "arbitrary"), vmem_limit_bytes=VMEM_LIMIT),
        name="attn",
    )(qt, qt, k, vt)


def _split2(x):
    hi = x.astype(BF16)
    lo = (x - hi.astype(F32)).astype(BF16)
    return jnp.concatenate([hi, lo], axis=1)


def _sum2(y, w):
    return y[:, :w] + y[:, w:]


def _gla_tile(z, g, pin, pmid, pex, state, reverse):
    T = z.shape[0]
    nsub = T // GLA_SUB
    q = z[:, 0:GLA_KEY].astype(F32) * (GLA_DK ** -0.5)
    k = z[:, GLA_KEY:2 * GLA_KEY].astype(F32)
    v = z[:, 2 * GLA_KEY:2 * GLA_KEY + GLA_WIDTH]
    g2 = _split2(g)
    b_in = _sum2(_dot(pin, g2), GLA_KEY)
    yield
    b_mid = _sum2(_dot(pmid, g2), GLA_KEY)
    yield
    b_ex = _sum2(_dot(pex, g2), GLA_KEY)
    yield
    qt = q * jnp.exp(b_in)
    qa = q * jnp.exp(b_in - b_mid)
    kt = (k * jnp.exp(b_mid - b_in)).astype(BF16)
    kp = k * jnp.exp(b_ex)
    dd = jnp.exp(b_in + b_ex)

    HT = GLA_HALF
    qhead = lax.broadcasted_iota(jnp.int32, (1, GLA_KEY), 1) // GLA_DK
    vhead = lax.broadcasted_iota(jnp.int32, (1, GLA_WIDTH), 1) // GLA_DV
    half_mask = pin[0:HT, 0:HT] > 0
    pmask = jnp.concatenate([half_mask, half_mask], axis=1)
    qa16 = qa.astype(BF16)
    a = []
    for hf in range(T // HT):
        rows = slice(hf * HT, (hf + 1) * HT)
        for hp in range(GLA_HEADS // 2):
            kstack = jnp.concatenate([jnp.where(qhead == 2 * hp + e, kt[rows], jnp.zeros_like(kt[rows]))
                                      for e in range(2)], axis=0)
            a.append(jnp.where(pmask, _dot_nt(qa16[rows], kstack), 0.0).astype(BF16))
            yield

    kpt = kp.T
    ddt = dd.T
    sub_of_lane = lax.broadcasted_iota(jnp.int32, (1, T), 1) // GLA_SUB
    stack = jnp.concatenate(
        [jnp.where(sub_of_lane == c, kpt, 0.0).astype(BF16) for c in range(nsub)], axis=0)
    inc = _dot(stack, v)
    yield
    same_head = (lax.broadcasted_iota(jnp.int32, (GLA_KEY, 1), 0) // GLA_DK) == vhead

    o_halves = []
    for hf in range(T // HT):
        rows = slice(hf * HT, (hf + 1) * HT)
        o_half = jnp.zeros((HT, GLA_WIDTH), F32)
        for hp in range(GLA_HEADS // 2):
            vstack = jnp.concatenate([jnp.where(vhead == 2 * hp + e, v[rows], jnp.zeros_like(v[rows]))
                                      for e in range(2)], axis=0)
            o_half = o_half + _dot(a[hf * (GLA_HEADS // 2) + hp], vstack)
            yield
        o_halves.append(o_half)
    o = jnp.concatenate(o_halves, axis=0)

    inter = [None] * nsub
    order = range(nsub - 1, -1, -1) if reverse else range(nsub)
    for c in order:
        rows = slice(c * GLA_SUB, (c + 1) * GLA_SUB)
        inter[c] = _dot(qt[rows].astype(BF16), state.astype(BF16))
        state = state * ddt[:, c * GLA_SUB:c * GLA_SUB + 1] + jnp.where(
            same_head, inc[c * GLA_KEY:(c + 1) * GLA_KEY], 0.0)
        yield
    return o + jnp.concatenate(inter, axis=0), state


def _lockstep(gens):
    results = [None] * len(gens)
    live = list(range(len(gens)))
    while live:
        for idx in list(live):
            try:
                next(gens[idx])
            except StopIteration as done:
                results[idx] = done.value
                live.remove(idx)
    return results


def _gla_kernel(zf_ref, zb_ref, gf_ref, gb_ref, pinf_ref, pmidf_ref, pexf_ref,
                pinb_ref, pmidb_ref, pexb_ref, of_ref, ob_ref, sf_ref, sb_ref):
    @pl.when(pl.program_id(1) == 0)
    def _():
        sf_ref[...] = jnp.zeros_like(sf_ref)
        sb_ref[...] = jnp.zeros_like(sb_ref)

    (of, sf), (ob, sb) = _lockstep([
        _gla_tile(zf_ref[0], gf_ref[0], pinf_ref[...], pmidf_ref[...], pexf_ref[...], sf_ref[...], False),
        _gla_tile(zb_ref[0], gb_ref[0], pinb_ref[...], pmidb_ref[...], pexb_ref[...], sb_ref[...], True)])
    of_ref[0] = of.astype(of_ref.dtype)
    ob_ref[0] = ob.astype(ob_ref.dtype)
    sf_ref[...] = sf
    sb_ref[...] = sb


def _gla_masks(T, reverse):
    t = np.arange(T)
    same = (t[:, None] // GLA_SUB) == (t[None, :] // GLA_SUB)
    before = (t[None, :] >= t[:, None]) if reverse else (t[None, :] <= t[:, None])
    pin = same & before
    pex = same & ~before
    in_sub = t[None, :] % GLA_SUB
    first_half = (in_sub >= GLA_SUB // 2) if reverse else (in_sub < GLA_SUB // 2)
    pmid = same & first_half
    return jnp.asarray(pin, BF16), jnp.asarray(pmid, BF16), jnp.asarray(pex, BF16)


def _gla(zgla, glog):
    B, S, _ = zgla.shape
    T = min(GLA_TILE, S)
    n = S // T
    consts = _gla_masks(T, False) + _gla_masks(T, True)
    fwd = lambda b, i: (b, i, 0)
    bwd = lambda b, i: (b, n - 1 - i, 0)
    return pl.pallas_call(
        _gla_kernel,
        grid=(B, n),
        in_specs=[pl.BlockSpec((1, T, ZGLA_WIDTH), fwd), pl.BlockSpec((1, T, ZGLA_WIDTH), bwd),
                  pl.BlockSpec((1, T, GLA_KEY), fwd), pl.BlockSpec((1, T, GLA_KEY), lambda b, i: (b, n - 1 - i, 1))]
        + [_full_spec(c) for c in consts],
        out_specs=[pl.BlockSpec((1, T, GLA_WIDTH), fwd), pl.BlockSpec((1, T, GLA_WIDTH), bwd)],
        out_shape=[jax.ShapeDtypeStruct((B, S, GLA_WIDTH), BF16)] * 2,
        scratch_shapes=[pltpu.VMEM((GLA_KEY, GLA_WIDTH), F32)] * 2,
        compiler_params=pltpu.CompilerParams(
            dimension_semantics=("parallel", "arbitrary"), vmem_limit_bytes=VMEM_LIMIT),
        name="gla",
    )(zgla, zgla, glog, glog, *consts)


def _silu(x):
    h = 0.5 * x
    return h + h * jnp.tanh(h)


N_MIX_INPUTS = 13


def _mix_final_kernel(*refs, seq_len):
    ins, (fng_ref, out_ref), scratch = refs[:N_MIX_INPUTS], refs[N_MIX_INPUTS:N_MIX_INPUTS + 2], refs[-3:]
    out_ref[0] = _rms(_mix_body(*ins, *scratch, seq_len=seq_len), fng_ref[...])


def _mix_proj_kernel(*refs, seq_len):
    ins, scratch = refs[:N_MIX_INPUTS], refs[-3:]
    proj_ops = refs[N_MIX_INPUTS:N_MIX_INPUTS + N_PROJ_OPERANDS]
    out_ref = refs[N_MIX_INPUTS + N_PROJ_OPERANDS]
    proj_outs = refs[N_MIX_INPUTS + N_PROJ_OPERANDS + 1:-3]
    y = _mix_body(*ins, *scratch, seq_len=seq_len)
    out_ref[0] = y
    _proj_body(y, *proj_ops, *proj_outs)


def _mix_body(x_ref, omla_ref, zgate_ref, u_ref, uprev_ref, unext_ref, gf_ref, gb_ref, wpool_ref, pscale_ref,
              gng_ref, hsum_ref, wout_ref, ubuf, lvl_a, lvl_b, *, seq_len):
    tm = x_ref.shape[1]
    i = pl.program_id(1)
    zg = zgate_ref[0].astype(F32)
    gate_mla = zg[:, 0:MLA_WIDTH]
    gate_pool = zg[:, MLA_WIDTH:MLA_WIDTH + POOL_WIDTH]
    gate_gla = zg[:, MLA_WIDTH + POOL_WIDTH:]
    u = u_ref[0]

    H = POOL_HALO
    lo, hi = H, tm + 3 * H
    pad = jnp.zeros((H, POOL_WIDTH), F32)
    for buf in (ubuf, lvl_a, lvl_b):
        buf[0:H] = pad[:, 0:buf.shape[1]]
        buf[hi:] = pad[:, 0:buf.shape[1]]
    ubuf[lo:2 * H] = jnp.where(i > 0, uprev_ref[0], 0.0)
    ubuf[2 * H:2 * H + tm] = u
    ubuf[2 * H + tm:hi] = jnp.where(i < pl.num_programs(1) - 1, unext_ref[0], 0.0)

    def level(src, cols, before, after):
        return src[lo - before:hi - before, cols] + src[lo + after:hi + after, cols]

    tile_rows = slice(H, H + tm)
    all_cols = slice(0, 128)
    p2 = level(ubuf, slice(0, 128), 1, 0)
    lvl_a[lo:hi] = p2
    p4 = level(lvl_a, all_cols, 1, 1)
    sums = [(p2[tile_rows], p4[tile_rows])]
    p2 = level(ubuf, slice(128, 256), 1, 0)
    lvl_a[lo:hi] = p2
    lvl_b[lo:hi] = level(lvl_a, all_cols, 1, 1)
    p8 = level(lvl_b, all_cols, 2, 2)
    lvl_a[lo:hi] = p8
    p16 = level(lvl_a, all_cols, 4, 4)
    sums.append((p8[tile_rows], p16[tile_rows]))

    pos = i * tm + lax.broadcasted_iota(jnp.int32, (tm, 128), 0)
    lane = lax.broadcasted_iota(jnp.int32, (1, 128), 1)
    cnt = lambda w: (jnp.minimum(pos + w // 2, seq_len) - jnp.maximum(pos - w // 2, 0)).astype(F32)
    pooled = [jnp.where(lane < POOL_GROUP_DIM, small / cnt(ws), big / cnt(wb))
              for (small, big), (ws, wb) in zip(sums, ((2, 4), (8, 16)))]
    pooled = jnp.concatenate(pooled, axis=1) - u
    o_pool = _dot(pooled.astype(BF16), wpool_ref[...]) * pscale_ref[...]

    og = gf_ref[0].astype(F32) + gb_ref[0].astype(F32)
    sq = og * og
    sq_hi = sq.astype(BF16)
    sq_lo = (sq - sq_hi.astype(F32)).astype(BF16)
    ms = (_dot(sq_hi, hsum_ref[...]) + _dot(sq_lo, hsum_ref[...])) * (1.0 / GLA_DV)
    o_gla = og * lax.rsqrt(ms + NORM_EPS) * gng_ref[...]

    m_mla = (omla_ref[0].astype(F32) * _silu(gate_mla)).astype(BF16)
    m_pool = (o_pool * _silu(gate_pool)).astype(BF16)
    m_gla = (o_gla * _silu(gate_gla)).astype(BF16)
    return (x_ref[0] + _dot(m_mla, wout_ref[0:MLA_WIDTH])
            + _dot(m_pool, wout_ref[MLA_WIDTH:MLA_WIDTH + POOL_WIDTH])
            + _dot(m_gla, wout_ref[MLA_WIDTH + POOL_WIDTH:]))


def _mix(x, omla, zgate, upool, gf, gb, lw, fng=None, next_lw=None, tab=None):
    B, S, _ = x.shape
    tm = min(ROW_TILE, S)
    hb = tm // POOL_HALO
    nhalo = S // POOL_HALO
    tile = lambda w: pl.BlockSpec((1, tm, w), lambda b, i: (b, i, 0))
    weights = (lw["wpool"], lw["pool_scale"], lw["gla_norm_g"], lw["hsum"], lw["wout"])
    inputs = (x, omla, zgate, upool, upool, upool, gf, gb) + weights
    assert len(inputs) == N_MIX_INPUTS
    in_specs = [
        tile(D_MODEL), tile(MLA_WIDTH), tile(ZGATE_WIDTH), tile(POOL_WIDTH),
        pl.BlockSpec((1, POOL_HALO, POOL_WIDTH), lambda b, i: (b, jnp.maximum(i * hb - 1, 0), 0)),
        pl.BlockSpec((1, POOL_HALO, POOL_WIDTH), lambda b, i: (b, jnp.minimum((i + 1) * hb, nhalo - 1), 0)),
        tile(GLA_WIDTH), tile(GLA_WIDTH),
    ] + [_full_spec(w) for w in weights]
    out_specs = [tile(D_MODEL)]
    out_shape = [jax.ShapeDtypeStruct((B, S, D_MODEL), F32)]
    if next_lw is None:
        body, name = _mix_final_kernel, "mix_final"
        inputs += (fng,)
        in_specs.append(_full_spec(fng))
    else:
        body, name = _mix_proj_kernel, "mix_proj"
        operands, op_specs = _proj_operands(next_lw, tab, tm)
        inputs += operands
        in_specs += op_specs
        proj_specs, proj_shapes = _proj_outputs(B, S, tm)
        out_specs += proj_specs
        out_shape += proj_shapes
    return pl.pallas_call(
        functools.partial(body, seq_len=S),
        grid=(B, S // tm),
        in_specs=in_specs,
        out_specs=out_specs,
        out_shape=out_shape,
        scratch_shapes=[pltpu.VMEM((tm + 4 * POOL_HALO, POOL_WIDTH), F32),
                        pltpu.VMEM((tm + 4 * POOL_HALO, 128), F32),
                        pltpu.VMEM((tm + 4 * POOL_HALO, 128), F32)],
        compiler_params=pltpu.CompilerParams(
            dimension_semantics=("parallel", "parallel"), vmem_limit_bytes=VMEM_LIMIT),
        name=name,
    )(*inputs)


def _rope_table(seq):
    inv_freq = 1.0 / (ROPE_BASE ** (jnp.arange(0, MLA_ROPE, 2, dtype=F32) / MLA_ROPE))
    ang = jnp.arange(seq, dtype=F32)[:, None] * inv_freq[None, :]
    cos, sin = jnp.cos(ang), jnp.sin(ang)
    return jnp.concatenate([cos, cos, -sin, sin], axis=1)


def _swap_halves(w):
    half = w.shape[-1] // 2
    return jnp.concatenate([w[..., half:], w[..., :half]], axis=-1)


def _layer_weights(l, norm_g, w_in, q_norm_g, w_uq, kv_norm_g, w_ukv, w_pool, pool_scale,
                   gk_up_fwd, gk_bias_fwd, gk_up_bwd, gk_bias_bwd, gla_norm_g, w_out):
    cols = [w_in[l][:, IN_OFFS[j]:IN_OFFS[j + 1]] for j in range(len(IN_SIZES))]
    (c_q, c_kv, k_rope, gate_mla, u_pool, gate_pool, q_gla, k_gla, v_gla, lr_f, lr_b, gate_gla) = cols
    lr_pad = jnp.zeros((D_MODEL, GLA_LR_PAD - 2 * GLA_GATE_RANK), F32)
    wa = jnp.concatenate([c_q, c_kv, k_rope, _swap_halves(k_rope)], axis=1)
    wgate = jnp.concatenate([gate_mla, gate_pool, gate_gla], axis=1)
    wgla = jnp.concatenate([q_gla, k_gla, v_gla, lr_f, lr_b, lr_pad], axis=1)
    uq = w_uq[l].reshape(MLA_Q_LORA, MLA_HEADS, MLA_NOPE + MLA_ROPE)
    uq_rope = uq[..., MLA_NOPE:]
    wuq = jnp.concatenate([uq, _swap_halves(uq_rope)], axis=-1).reshape(MLA_Q_LORA, MLA_HEADS * QK_WIDTH)
    ukv = w_ukv[l].reshape(MLA_KV_LORA, MLA_HEADS, MLA_NOPE + MLA_VDIM)
    wuk = ukv[..., :MLA_NOPE].reshape(MLA_KV_LORA, MLA_HEADS * MLA_NOPE)
    wuvt = ukv[..., MLA_NOPE:].reshape(MLA_KV_LORA, MLA_HEADS * MLA_VDIM).T
    wpool = jnp.zeros((POOL_WIDTH, POOL_WIDTH), F32)
    for gi in range(len(POOL_WINDOWS)):
        sl = slice(gi * POOL_GROUP_DIM, (gi + 1) * POOL_GROUP_DIM)
        wpool = wpool.at[sl, sl].set(w_pool[l, gi])
    gup = jnp.zeros((GLA_LR_PAD, 2 * GLA_KEY), F32)
    gup = gup.at[:GLA_GATE_RANK, :GLA_KEY].set(gk_up_fwd[l])
    gup = gup.at[GLA_GATE_RANK:2 * GLA_GATE_RANK, GLA_KEY:].set(gk_up_bwd[l])
    gbias = jnp.concatenate([gk_bias_fwd[l], gk_bias_bwd[l]])[None, :]
    head = np.arange(GLA_WIDTH) // GLA_DV
    hsum = jnp.asarray(head[:, None] == head[None, :], BF16)
    return {
        "norm_g": norm_g[l][None, :],
        "wa": wa.astype(BF16), "wgate": wgate.astype(BF16), "wpu": u_pool.astype(BF16),"wgla": wgla.astype(BF16),
        "q_norm_g": q_norm_g[l][None, :], "wuqt": wuq.T.astype(BF16),
        "kv_norm_g": kv_norm_g[l][None, :], "wuk": wuk.astype(BF16), "wuvt": wuvt.astype(BF16),
        "wpool": wpool.astype(BF16), "pool_scale": pool_scale[l][None, :],
        "gup": gup.astype(BF16), "gbias": gbias,
        "gla_norm_g": jnp.tile(gla_norm_g[l], GLA_HEADS)[None, :], "hsum": hsum,
        "wout": w_out[l].astype(BF16),
    }


def _trunk(x, layers, fng):
    tab = _rope_table(x.shape[1])
    tab = (tab, tab.T)
    q, k, vt, zgate, upool, zgla, glog = _proj(x, layers[0], tab)
    for l, lw in enumerate(layers):
        omla = _attn(q, k, vt)
        gf, gb = _gla(zgla, glog)
        if l + 1 < len(layers):
            x, q, k, vt, zgate, upool, zgla, glog = _mix(x, omla, zgate, upool, gf, gb, lw,
                                                         next_lw=layers[l + 1], tab=tab)
        else:
            (x,) = _mix(x, omla, zgate, upool, gf, gb, lw, fng=fng)
    return x


def kernel(x_prompt, x_sample, norm_g, w_in, q_norm_g, w_uq, kv_norm_g, w_ukv, w_pool, pool_scale,
           gk_up_fwd, gk_bias_fwd, gk_up_bwd, gk_bias_bwd, gla_norm_g, w_out, final_norm_g):
    depth = w_in.shape[0]
    layers = [_layer_weights(l, norm_g, w_in, q_norm_g, w_uq, kv_norm_g, w_ukv, w_pool, pool_scale,
                             gk_up_fwd, gk_bias_fwd, gk_up_bwd, gk_bias_bwd, gla_norm_g, w_out)
              for l in range(depth)]
    fng = final_norm_g[None, :]
    return (_trunk(x_prompt, layers, fng), _trunk(x_sample, layers, fng))
```

```python
import functools
import math

import jax
import jax.numpy as jnp
import numpy as np
from jax import lax
from jax.experimental import pallas as pl
from jax.experimental.pallas import tpu as pltpu

F32 = jnp.float32
BF16 = jnp.bfloat16

D_MODEL = 1024
NORM_EPS = 1e-6
MLA_HEADS = 4
MLA_NOPE = 128
MLA_ROPE = 64
MLA_VDIM = 128
MLA_Q_LORA = 256
MLA_KV_LORA = 128
MLA_WIDTH = MLA_HEADS * MLA_VDIM
MLA_SCALE = (MLA_NOPE + MLA_ROPE) ** -0.5
ROPE_BASE = 10000.0
QK_WIDTH = 256
POOL_WINDOWS = (2, 4, 8, 16)
POOL_WIDTH = 256
POOL_GROUP_DIM = 64
POOL_HALO = 8
GLA_HEADS = 4
GLA_WIDTH = 256
GLA_KEY = 128
GLA_DK = 32
GLA_DV = 64
GLA_GATE_RANK = 16
GLA_GATE_NORM = 16.0
GLA_SUB = 32
GLA_TILE = 256
GLA_HALF = 128

MIX_WIDTH = MLA_WIDTH + POOL_WIDTH + GLA_WIDTH
IN_SIZES = (MLA_Q_LORA, MLA_KV_LORA, MLA_ROPE, MLA_WIDTH, POOL_WIDTH, POOL_WIDTH,
            GLA_KEY, GLA_KEY, GLA_WIDTH, GLA_GATE_RANK, GLA_GATE_RANK, GLA_WIDTH)
IN_OFFS = tuple(int(v) for v in np.cumsum((0,) + IN_SIZES))

ZA_WIDTH = MLA_Q_LORA + MLA_KV_LORA + 2 * MLA_ROPE
ZGATE_WIDTH = MLA_WIDTH + POOL_WIDTH + GLA_WIDTH
ZGLA_WIDTH = 2 * GLA_KEY + GLA_WIDTH
GLA_LR_PAD = 128

ROW_TILE = 512
ATTN_TQ = 512
ATTN_TK = 1024
VMEM_LIMIT = 56 * 1024 * 1024


def _rms(x, g):
    return x * lax.rsqrt(jnp.mean(x * x, axis=-1, keepdims=True) + NORM_EPS) * g


def _dot(a, b):
    return jnp.dot(a, b, preferred_element_type=F32)


def _dot_nt(a, b):
    return lax.dot_general(a, b, (((1,), (1,)), ((), ())), preferred_element_type=F32)


N_PROJ_OPERANDS = 14
N_PROJ_OUTS = 7


def _proj_kernel(x_ref, *refs):
    _proj_body(x_ref[0], *refs)


def _proj_body(x, ng_ref, wa_ref, wgate_ref, wpu_ref, wgla_ref, qg_ref, wuqt_ref, kvg_ref, wuk_ref,
               wuvt_ref, gup_ref, gbias_ref, tab_ref, tabt_ref,
               qt_ref, k_ref, vt_ref, zgate_ref, upool_ref, zgla_ref, glog_ref):
    h = _rms(x, ng_ref[...]).astype(BF16)
    za = _dot(h, wa_ref[...])
    zgate_ref[0] = _dot(h, wgate_ref[...]).astype(BF16)
    upool_ref[0] = _dot(h, wpu_ref[...])
    zgla = _dot(h, wgla_ref[...])
    zgla_ref[0] = zgla[:, :ZGLA_WIDTH].astype(BF16)
    zg = _dot(zgla[:, ZGLA_WIDTH:].astype(BF16), gup_ref[...]) + gbias_ref[...]
    glog_ref[0] = (jnp.minimum(zg, 0.0) - jnp.log1p(jnp.exp(-jnp.abs(zg)))) * (1.0 / GLA_GATE_NORM)
    tab = tab_ref[...]
    tabt = tabt_ref[...]
    qscale = MLA_SCALE * math.log2(math.e)
    cqn = _rms(za[:, :MLA_Q_LORA], qg_ref[...]).astype(BF16)
    qall = _dot_nt(wuqt_ref[...], cqn)
    zeros = jnp.zeros((QK_WIDTH - MLA_NOPE - MLA_ROPE, qall.shape[1]), BF16)
    for hh in range(MLA_HEADS):
        base = hh * QK_WIDTH
        qn = qall[base:base + MLA_NOPE] * qscale
        t = qall[base + MLA_NOPE:base + QK_WIDTH] * tabt
        r = (t[:MLA_ROPE] + t[MLA_ROPE:]) * qscale
        qt_ref[0, hh, 0:MLA_NOPE] = qn.astype(BF16)
        qt_ref[0, hh, MLA_NOPE:MLA_NOPE + MLA_ROPE] = r.astype(BF16)
        qt_ref[0, hh, MLA_NOPE + MLA_ROPE:] = zeros
    ckvn = _rms(za[:, MLA_Q_LORA:MLA_Q_LORA + MLA_KV_LORA], kvg_ref[...]).astype(BF16)
    kn = _dot(ckvn, wuk_ref[...])
    t = za[:, MLA_Q_LORA + MLA_KV_LORA:] * tab
    kr = t + pltpu.roll(t, MLA_ROPE, 1)
    lane = lax.broadcasted_iota(jnp.int32, kr.shape, 1)
    kr = jnp.where(lane < MLA_ROPE, kr, 0.0).astype(BF16)
    for hh in range(MLA_HEADS):
        k_ref[0, hh, :, 0:MLA_NOPE] = kn[:, hh * MLA_NOPE:(hh + 1) * MLA_NOPE].astype(BF16)
        k_ref[0, hh, :, MLA_NOPE:QK_WIDTH] = kr
    vt = _dot_nt(wuvt_ref[...], ckvn)
    for hh in range(MLA_HEADS):
        vt_ref[0, hh] = vt[hh * MLA_VDIM:(hh + 1) * MLA_VDIM].astype(BF16)


def _full_spec(a):
    return pl.BlockSpec(a.shape, lambda b, i: (0,) * a.ndim)


def _proj_operands(lw, tab, tm):
    weights = (lw["norm_g"], lw["wa"], lw["wgate"], lw["wpu"], lw["wgla"], lw["q_norm_g"], lw["wuqt"],
               lw["kv_norm_g"], lw["wuk"], lw["wuvt"], lw["gup"], lw["gbias"])
    specs = [_full_spec(w) for w in weights] + [pl.BlockSpec((tm, 128), lambda b, i: (i, 0)),
                                                pl.BlockSpec((128, tm), lambda b, i: (0, i))]
    operands = weights + tuple(tab)
    assert len(operands) == N_PROJ_OPERANDS
    return operands, specs


def _proj_outputs(B, S, tm):
    specs = [
        pl.BlockSpec((1, MLA_HEADS, QK_WIDTH, tm), lambda b, i: (b, 0, 0, i)),
        pl.BlockSpec((1, MLA_HEADS, tm, QK_WIDTH), lambda b, i: (b, 0, i, 0)),
        pl.BlockSpec((1, MLA_HEADS, MLA_VDIM, tm), lambda b, i: (b, 0, 0, i)),
        pl.BlockSpec((1, tm, ZGATE_WIDTH), lambda b, i: (b, i, 0)),
        pl.BlockSpec((1, tm, POOL_WIDTH), lambda b, i: (b, i, 0)),
        pl.BlockSpec((1, tm, ZGLA_WIDTH), lambda b, i: (b, i, 0)),
        pl.BlockSpec((1, tm, 2 * GLA_KEY), lambda b, i: (b, i, 0)),
    ]
    shapes = [
        jax.ShapeDtypeStruct((B, MLA_HEADS, QK_WIDTH, S), BF16),
        jax.ShapeDtypeStruct((B, MLA_HEADS, S, QK_WIDTH), BF16),
        jax.ShapeDtypeStruct((B, MLA_HEADS, MLA_VDIM, S), BF16),
        jax.ShapeDtypeStruct((B, S, ZGATE_WIDTH), BF16),
        jax.ShapeDtypeStruct((B, S, POOL_WIDTH), F32),
        jax.ShapeDtypeStruct((B, S, ZGLA_WIDTH), BF16),
        jax.ShapeDtypeStruct((B, S, 2 * GLA_KEY), F32),
    ]
    assert len(specs) == N_PROJ_OUTS
    return specs, shapes


def _proj(x, lw, tab):
    B, S, _ = x.shape
    tm = min(ROW_TILE, S)
    operands, op_specs = _proj_operands(lw, tab, tm)
    out_specs, out_shape = _proj_outputs(B, S, tm)
    return pl.pallas_call(
        _proj_kernel,
        grid=(B, S // tm),
        in_specs=[pl.BlockSpec((1, tm, D_MODEL), lambda b, i: (b, i, 0))] + op_specs,
        out_specs=out_specs,
        out_shape=out_shape,
        compiler_params=pltpu.CompilerParams(
            dimension_semantics=("parallel", "parallel"), vmem_limit_bytes=VMEM_LIMIT),
        name="proj",
    )(x, *operands)


def _attn_kernel(qt_ref, qn_ref, k_ref, vt_ref, o_ref, sa_ref, sb_ref, mxa_ref, acc_ref, *, tk, nk):
    tq = qt_ref.shape[3]

    def scores(i, q_ref, s_ref):
        start = pl.multiple_of(i * tk, tk)
        s = _dot(k_ref[0, 0, pl.ds(start, tk), :], q_ref[0, 0])
        s_ref[:, 0:tq] = s
        return jnp.max(s, axis=0, keepdims=True)

    def update(i, s_ref, mx, m, l):
        start = pl.multiple_of(i * tk, tk)
        m_new = jnp.maximum(m, mx)
        alpha = jnp.exp2(m - m_new)
        p = jnp.exp2(s_ref[:, 0:tq] - m_new)
        l = alpha * l + jnp.sum(p, axis=0, keepdims=True)
        acc_ref[...] = alpha * acc_ref[...] + _dot(vt_ref[0, 0, :, pl.ds(start, tk)], p.astype(BF16))
        return m_new, l

    @pl.when(pl.program_id(2) == 0)
    def _():
        mxa_ref[...] = scores(0, qt_ref, sa_ref)

    def pair(i, next_chunk, next_q_ref, carry):
        m, l, mx_a = carry
        mx_b = scores(i + 1, qt_ref, sb_ref)
        m, l = update(i, sa_ref, mx_a, m, l)
        mx_a = scores(next_chunk, next_q_ref, sa_ref)
        m, l = update(i + 1, sb_ref, mx_b, m, l)
        return m, l, mx_a

    m = jnp.full((1, tq), -jnp.inf, F32)
    l = jnp.zeros((1, tq), F32)
    acc_ref[...] = jnp.zeros_like(acc_ref)
    carry = lax.fori_loop(0, nk // 2 - 1, lambda j, c: pair(2 * j, 2 * j + 2, qt_ref, c),
                          (m, l, mxa_ref[...]))
    m, l, mx_a = pair(nk - 2, 0, qn_ref, carry)
    mxa_ref[...] = mx_a
    o_ref[0] = (acc_ref[...] / l).T.astype(o_ref.dtype)


def _attn(qt, k, vt):
    B, H, S, _ = k.shape
    tq = min(ATTN_TQ, S)
    tk = min(ATTN_TK, S // 2)
    nk = S // tk
    nq = S // tq
    assert nk % 2 == 0
    return pl.pallas_call(
        functools.partial(_attn_kernel, tk=tk, nk=nk),
        grid=(B, H, nq),
        in_specs=[
            pl.BlockSpec((1, 1, QK_WIDTH, tq), lambda b, h, i: (b, h, 0, i)),
            pl.BlockSpec((1, 1, QK_WIDTH, tq), lambda b, h, i: (b, h, 0, jnp.minimum(i + 1, nq - 1))),
            pl.BlockSpec((1, 1, S, QK_WIDTH), lambda b, h, i: (b, h, 0, 0)),
            pl.BlockSpec((1, 1, MLA_VDIM, S), lambda b, h, i: (b, h, 0, 0)),
        ],
        out_specs=pl.BlockSpec((1, tq, MLA_VDIM), lambda b, h, i: (b, i, h)),
        out_shape=jax.ShapeDtypeStruct((B, S, MLA_WIDTH), BF16),
        scratch_shapes=[pltpu.VMEM((tk, tq + 128), F32), pltpu.VMEM((tk, tq + 128), F32),
                        pltpu.VMEM((1, tq), F32), pltpu.VMEM((MLA_VDIM, tq), F32)],
        compiler_params=pltpu.CompilerParams(
            dimension_semantics=("parallel", "parallel", "arbitrary"), vmem_limit_bytes=VMEM_LIMIT),
        name="attn",
    )(qt, qt, k, vt)


def _split2(x):
    hi = x.astype(BF16)
    lo = (x - hi.astype(F32)).astype(BF16)
    return jnp.concatenate([hi, lo], axis=1)


def _sum2(y, w):
    return y[:, :w] + y[:, w:]


def _gla_prep(z, g, pin, pmid, pex):
    T = z.shape[0]
    nsub = T // GLA_SUB
    q = z[:, 0:GLA_KEY].astype(F32) * (GLA_DK ** -0.5)
    k = z[:, GLA_KEY:2 * GLA_KEY].astype(F32)
    g2 = _split2(g)
    b_in = _sum2(_dot(pin, g2), GLA_KEY)
    yield
    b_mid = _sum2(_dot(pmid, g2), GLA_KEY)
    yield
    b_ex = _sum2(_dot(pex, g2), GLA_KEY)
    yield
    qt = (q * jnp.exp(b_in)).astype(BF16)
    qa = (q * jnp.exp(b_in - b_mid)).astype(BF16)
    kt = (k * jnp.exp(b_mid - b_in)).astype(BF16)
    kp = k * jnp.exp(b_ex)
    dd = jnp.exp(b_in + b_ex)
    kpt = kp.T
    sub_of_lane = lax.broadcasted_iota(jnp.int32, (1, T), 1) // GLA_SUB
    stack = jnp.concatenate(
        [jnp.where(sub_of_lane == c, kpt, 0.0).astype(BF16) for c in range(nsub)], axis=0)
    return qt, qa, kt, stack, dd.T


def _gla_main(v, qt, qa16, kt, stack, ddt, pin, state, reverse):
    T = v.shape[0]
    nsub = T // GLA_SUB
    HT = GLA_HALF
    qhead = lax.broadcasted_iota(jnp.int32, (1, GLA_KEY), 1) // GLA_DK
    vhead = lax.broadcasted_iota(jnp.int32, (1, GLA_WIDTH), 1) // GLA_DV
    half_mask = pin[0:HT, 0:HT] > 0
    pmask = jnp.concatenate([half_mask, half_mask], axis=1)
    a = []
    for hf in range(T // HT):
        rows = slice(hf * HT, (hf + 1) * HT)
        for hp in range(GLA_HEADS // 2):
            kstack = jnp.concatenate([jnp.where(qhead == 2 * hp + e, kt[rows], jnp.zeros_like(kt[rows]))
                                      for e in range(2)], axis=0)
            a.append(jnp.where(pmask, _dot_nt(qa16[rows], kstack), 0.0).astype(BF16))
            yield

    inc = _dot(stack, v)
    yield
    same_head = (lax.broadcasted_iota(jnp.int32, (GLA_KEY, 1), 0) // GLA_DK) == vhead

    o_halves = []
    for hf in range(T // HT):
        rows = slice(hf * HT, (hf + 1) * HT)
        o_half = jnp.zeros((HT, GLA_WIDTH), F32)
        for hp in range(GLA_HEADS // 2):
            vstack = jnp.concatenate([jnp.where(vhead == 2 * hp + e, v[rows], jnp.zeros_like(v[rows]))
                                      for e in range(2)], axis=0)
            o_half = o_half + _dot(a[hf * (GLA_HEADS // 2) + hp], vstack)
            yield
        o_halves.append(o_half)
    o = jnp.concatenate(o_halves, axis=0)

    inter = [None] * nsub
    order = range(nsub - 1, -1, -1) if reverse else range(nsub)
    for c in order:
        rows = slice(c * GLA_SUB, (c + 1) * GLA_SUB)
        inter[c] = _dot(qt[rows], state.astype(BF16))
        state = state * ddt[:, c * GLA_SUB:c * GLA_SUB + 1] + jnp.where(
            same_head, inc[c * GLA_KEY:(c + 1) * GLA_KEY], 0.0)
        yield
    return o + jnp.concatenate(inter, axis=0), state


def _lockstep(gens):
    results = [None] * len(gens)
    live = list(range(len(gens)))
    while live:
        for idx in list(live):
            try:
                next(gens[idx])
            except StopIteration as done:
                results[idx] = done.value
                live.remove(idx)
    return results


def _gla_tile(z, g, pin, pmid, pex, state, reverse):
    prep = yield from _gla_prep(z, g, pin, pmid, pex)
    return (yield from _gla_main(z[:, 2 * GLA_KEY:2 * GLA_KEY + GLA_WIDTH], *prep, pin, state, reverse))


def _gla_kernel(zf_ref, zb_ref, gf_ref, gb_ref, pinf_ref, pmidf_ref, pexf_ref,
                pinb_ref, pmidb_ref, pexb_ref, of_ref, ob_ref, sf_ref, sb_ref):
    @pl.when(pl.program_id(1) == 0)
    def _():
        sf_ref[...] = jnp.zeros_like(sf_ref)
        sb_ref[...] = jnp.zeros_like(sb_ref)

    (of, sf), (ob, sb) = _lockstep([
        _gla_tile(zf_ref[0], gf_ref[0], pinf_ref[...], pmidf_ref[...], pexf_ref[...], sf_ref[...], False),
        _gla_tile(zb_ref[0], gb_ref[0], pinb_ref[...], pmidb_ref[...], pexb_ref[...], sb_ref[...], True)])
    of_ref[0] = of.astype(of_ref.dtype)
    ob_ref[0] = ob.astype(ob_ref.dtype)
    sf_ref[...] = sf
    sb_ref[...] = sb


def _gla_masks(T, reverse):
    t = np.arange(T)
    same = (t[:, None] // GLA_SUB) == (t[None, :] // GLA_SUB)
    before = (t[None, :] >= t[:, None]) if reverse else (t[None, :] <= t[:, None])
    pin = same & before
    pex = same & ~before
    in_sub = t[None, :] % GLA_SUB
    first_half = (in_sub >= GLA_SUB // 2) if reverse else (in_sub < GLA_SUB // 2)
    pmid = same & first_half
    return jnp.asarray(pin, BF16), jnp.asarray(pmid, BF16), jnp.asarray(pex, BF16)


def _gla(zgla, glog):
    B, S, _ = zgla.shape
    T = min(GLA_TILE, S)
    n = S // T
    consts = _gla_masks(T, False) + _gla_masks(T, True)
    fwd = lambda b, i: (b, i, 0)
    bwd = lambda b, i: (b, n - 1 - i, 0)
    return pl.pallas_call(
        _gla_kernel,
        grid=(B, n),
        in_specs=[pl.BlockSpec((1, T, ZGLA_WIDTH), fwd), pl.BlockSpec((1, T, ZGLA_WIDTH), bwd),
                  pl.BlockSpec((1, T, GLA_KEY), fwd), pl.BlockSpec((1, T, GLA_KEY), lambda b, i: (b, n - 1 - i, 1))]
        + [_full_spec(c) for c in consts],
        out_specs=[pl.BlockSpec((1, T, GLA_WIDTH), fwd), pl.BlockSpec((1, T, GLA_WIDTH), bwd)],
        out_shape=[jax.ShapeDtypeStruct((B, S, GLA_WIDTH), BF16)] * 2,
        scratch_shapes=[pltpu.VMEM((GLA_KEY, GLA_WIDTH), F32)] * 2,
        compiler_params=pltpu.CompilerParams(
            dimension_semantics=("parallel", "arbitrary"), vmem_limit_bytes=VMEM_LIMIT),
        name="gla",
    )(zgla, zgla, glog, glog, *consts)


def _silu(x):
    h = 0.5 * x
    return h + h * jnp.tanh(h)


N_MIX_INPUTS = 13


def _mix_final_kernel(*refs, seq_len):
    ins, (fng_ref, out_ref), scratch = refs[:N_MIX_INPUTS], refs[N_MIX_INPUTS:N_MIX_INPUTS + 2], refs[-3:]
    out_ref[0] = _rms(_mix_body(*ins, *scratch, seq_len=seq_len), fng_ref[...])


def _mix_proj_kernel(*refs, seq_len):
    ins, scratch = refs[:N_MIX_INPUTS], refs[-3:]
    proj_ops = refs[N_MIX_INPUTS:N_MIX_INPUTS + N_PROJ_OPERANDS]
    out_ref = refs[N_MIX_INPUTS + N_PROJ_OPERANDS]
    proj_outs = refs[N_MIX_INPUTS + N_PROJ_OPERANDS + 1:-3]
    y = _mix_body(*ins, *scratch, seq_len=seq_len)
    out_ref[0] = y
    _proj_body(y, *proj_ops, *proj_outs)


def _mix_body(x_ref, omla_ref, zgate_ref, u_ref, uprev_ref, unext_ref, gf_ref, gb_ref, wpool_ref, pscale_ref,
              gng_ref, hsum_ref, wout_ref, ubuf, lvl_a, lvl_b, *, seq_len):
    tm = x_ref.shape[1]
    i = pl.program_id(1)
    zg = zgate_ref[0].astype(F32)
    gate_mla = zg[:, 0:MLA_WIDTH]
    gate_pool = zg[:, MLA_WIDTH:MLA_WIDTH + POOL_WIDTH]
    gate_gla = zg[:, MLA_WIDTH + POOL_WIDTH:]
    u = u_ref[0]

    H = POOL_HALO
    lo, hi = H, tm + 3 * H
    pad = jnp.zeros((H, POOL_WIDTH), F32)
    for buf in (ubuf, lvl_a, lvl_b):
        buf[0:H] = pad[:, 0:buf.shape[1]]
        buf[hi:] = pad[:, 0:buf.shape[1]]
    ubuf[lo:2 * H] = jnp.where(i > 0, uprev_ref[0], 0.0)
    ubuf[2 * H:2 * H + tm] = u
    ubuf[2 * H + tm:hi] = jnp.where(i < pl.num_programs(1) - 1, unext_ref[0], 0.0)

    def level(src, cols, before, after):
        return src[lo - before:hi - before, cols] + src[lo + after:hi + after, cols]

    tile_rows = slice(H, H + tm)
    all_cols = slice(0, 128)
    p2 = level(ubuf, slice(0, 128), 1, 0)
    lvl_a[lo:hi] = p2
    p4 = level(lvl_a, all_cols, 1, 1)
    sums = [(p2[tile_rows], p4[tile_rows])]
    p2 = level(ubuf, slice(128, 256), 1, 0)
    lvl_a[lo:hi] = p2
    lvl_b[lo:hi] = level(lvl_a, all_cols, 1, 1)
    p8 = level(lvl_b, all_cols, 2, 2)
    lvl_a[lo:hi] = p8
    p16 = level(lvl_a, all_cols, 4, 4)
    sums.append((p8[tile_rows], p16[tile_rows]))

    pos = i * tm + lax.broadcasted_iota(jnp.int32, (tm, 128), 0)
    lane = lax.broadcasted_iota(jnp.int32, (1, 128), 1)
    cnt = lambda w: (jnp.minimum(pos + w // 2, seq_len) - jnp.maximum(pos - w // 2, 0)).astype(F32)
    pooled = [jnp.where(lane < POOL_GROUP_DIM, small / cnt(ws), big / cnt(wb))
              for (small, big), (ws, wb) in zip(sums, ((2, 4), (8, 16)))]
    pooled = jnp.concatenate(pooled, axis=1) - u
    o_pool = _dot(pooled.astype(BF16), wpool_ref[...]) * pscale_ref[...]

    og = gf_ref[0].astype(F32) + gb_ref[0].astype(F32)
    sq = og * og
    sq_hi = sq.astype(BF16)
    sq_lo = (sq - sq_hi.astype(F32)).astype(BF16)
    ms = (_dot(sq_hi, hsum_ref[...]) + _dot(sq_lo, hsum_ref[...])) * (1.0 / GLA_DV)
    o_gla = og * lax.rsqrt(ms + NORM_EPS) * gng_ref[...]

    m_mla = (omla_ref[0].astype(F32) * _silu(gate_mla)).astype(BF16)
    m_pool = (o_pool * _silu(gate_pool)).astype(BF16)
    m_gla = (o_gla * _silu(gate_gla)).astype(BF16)
    return (x_ref[0] + _dot(m_mla, wout_ref[0:MLA_WIDTH])
            + _dot(m_pool, wout_ref[MLA_WIDTH:MLA_WIDTH + POOL_WIDTH])
            + _dot(m_gla, wout_ref[MLA_WIDTH + POOL_WIDTH:]))


def _mix(x, omla, zgate, upool, gf, gb, lw, fng=None, next_lw=None, tab=None):
    B, S, _ = x.shape
    tm = min(ROW_TILE, S)
    hb = tm // POOL_HALO
    nhalo = S // POOL_HALO
    tile = lambda w: pl.BlockSpec((1, tm, w), lambda b, i: (b, i, 0))
    weights = (lw["wpool"], lw["pool_scale"], lw["gla_norm_g"], lw["hsum"], lw["wout"])
    inputs = (x, omla, zgate, upool, upool, upool, gf, gb) + weights
    assert len(inputs) == N_MIX_INPUTS
    in_specs = [
        tile(D_MODEL), tile(MLA_WIDTH), tile(ZGATE_WIDTH), tile(POOL_WIDTH),
        pl.BlockSpec((1, POOL_HALO, POOL_WIDTH), lambda b, i: (b, jnp.maximum(i * hb - 1, 0), 0)),
        pl.BlockSpec((1, POOL_HALO, POOL_WIDTH), lambda b, i: (b, jnp.minimum((i + 1) * hb, nhalo - 1), 0)),
        tile(GLA_WIDTH), tile(GLA_WIDTH),
    ] + [_full_spec(w) for w in weights]
    out_specs = [tile(D_MODEL)]
    out_shape = [jax.ShapeDtypeStruct((B, S, D_MODEL), F32)]
    if next_lw is None:
        body, name = _mix_final_kernel, "mix_final"
        inputs += (fng,)
        in_specs.append(_full_spec(fng))
    else:
        body, name = _mix_proj_kernel, "mix_proj"
        operands, op_specs = _proj_operands(next_lw, tab, tm)
        inputs += operands
        in_specs += op_specs
        proj_specs, proj_shapes = _proj_outputs(B, S, tm)
        out_specs += proj_specs
        out_shape += proj_shapes
    return pl.pallas_call(
        functools.partial(body, seq_len=S),
        grid=(B, S // tm),
        in_specs=in_specs,
        out_specs=out_specs,
        out_shape=out_shape,
        scratch_shapes=[pltpu.VMEM((tm + 4 * POOL_HALO, POOL_WIDTH), F32),
                        pltpu.VMEM((tm + 4 * POOL_HALO, 128), F32),
                        pltpu.VMEM((tm + 4 * POOL_HALO, 128), F32)],
        compiler_params=pltpu.CompilerParams(
            dimension_semantics=("parallel", "parallel"), vmem_limit_bytes=VMEM_LIMIT),
        name=name,
    )(*inputs)


def _rope_table(seq):
    inv_freq = 1.0 / (ROPE_BASE ** (jnp.arange(0, MLA_ROPE, 2, dtype=F32) / MLA_ROPE))
    ang = jnp.arange(seq, dtype=F32)[:, None] * inv_freq[None, :]
    cos, sin = jnp.cos(ang), jnp.sin(ang)
    return jnp.concatenate([cos, cos, -sin, sin], axis=1)


def _swap_halves(w):
    half = w.shape[-1] // 2
    return jnp.concatenate([w[..., half:], w[..., :half]], axis=-1)


def _layer_weights(l, norm_g, w_in, q_norm_g, w_uq, kv_norm_g, w_ukv, w_pool, pool_scale,
                   gk_up_fwd, gk_bias_fwd, gk_up_bwd, gk_bias_bwd, gla_norm_g, w_out):
    cols = [w_in[l][:, IN_OFFS[j]:IN_OFFS[j + 1]] for j in range(len(IN_SIZES))]
    (c_q, c_kv, k_rope, gate_mla, u_pool, gate_pool, q_gla, k_gla, v_gla, lr_f, lr_b, gate_gla) = cols
    lr_pad = jnp.zeros((D_MODEL, GLA_LR_PAD - 2 * GLA_GATE_RANK), F32)
    wa = jnp.concatenate([c_q, c_kv, k_rope, _swap_halves(k_rope)], axis=1)
    wgate = jnp.concatenate([gate_mla, gate_pool, gate_gla], axis=1)
    wgla = jnp.concatenate([q_gla, k_gla, v_gla, lr_f, lr_b, lr_pad], axis=1)
    uq = w_uq[l].reshape(MLA_Q_LORA, MLA_HEADS, MLA_NOPE + MLA_ROPE)
    uq_rope = uq[..., MLA_NOPE:]
    wuq = jnp.concatenate([uq, _swap_halves(uq_rope)], axis=-1).reshape(MLA_Q_LORA, MLA_HEADS * QK_WIDTH)
    ukv = w_ukv[l].reshape(MLA_KV_LORA, MLA_HEADS, MLA_NOPE + MLA_VDIM)
    wuk = ukv[..., :MLA_NOPE].reshape(MLA_KV_LORA, MLA_HEADS * MLA_NOPE)
    wuvt = ukv[..., MLA_NOPE:].reshape(MLA_KV_LORA, MLA_HEADS * MLA_VDIM).T
    wpool = jnp.zeros((POOL_WIDTH, POOL_WIDTH), F32)
    for gi in range(len(POOL_WINDOWS)):
        sl = slice(gi * POOL_GROUP_DIM, (gi + 1) * POOL_GROUP_DIM)
        wpool = wpool.at[sl, sl].set(w_pool[l, gi])
    gup = jnp.zeros((GLA_LR_PAD, 2 * GLA_KEY), F32)
    gup = gup.at[:GLA_GATE_RANK, :GLA_KEY].set(gk_up_fwd[l])
    gup = gup.at[GLA_GATE_RANK:2 * GLA_GATE_RANK, GLA_KEY:].set(gk_up_bwd[l])
    gbias = jnp.concatenate([gk_bias_fwd[l], gk_bias_bwd[l]])[None, :]
    head = np.arange(GLA_WIDTH) // GLA_DV
    hsum = jnp.asarray(head[:, None] == head[None, :], BF16)
    return {
        "norm_g": norm_g[l][None, :],
        "wa": wa.astype(BF16), "wgate": wgate.astype(BF16), "wpu": u_pool.astype(BF16), "wgla": wgla.astype(BF16),
        "q_norm_g": q_norm_g[l][None, :], "wuqt": wuq.T.astype(BF16),
        "kv_norm_g": kv_norm_g[l][None, :], "wuk": wuk.astype(BF16), "wuvt": wuvt.astype(BF16),
        "wpool": wpool.astype(BF16), "pool_scale": pool_scale[l][None, :],
        "gup": gup.astype(BF16), "gbias": gbias,
        "gla_norm_g": jnp.tile(gla_norm_g[l], GLA_HEADS)[None, :], "hsum": hsum,
        "wout": w_out[l].astype(BF16),
    }


def _trunk(x, layers, fng):
    tab = _rope_table(x.shape[1])
    tab = (tab, tab.T)
    q, k, vt, zgate, upool, zgla, glog = _proj(x, layers[0], tab)
    for l, lw in enumerate(layers):
        omla = _attn(q, k, vt)
        gf, gb = _gla(zgla, glog)
        if l + 1 < len(layers):
            x, q, k, vt, zgate, upool, zgla, glog = _mix(x, omla, zgate, upool, gf, gb, lw,
                                                         next_lw=layers[l + 1], tab=tab)
        else:
            (x,) = _mix(x, omla, zgate, upool, gf, gb, lw, fng=fng)
    return x


def kernel(x_prompt, x_sample, norm_g, w_in, q_norm_g, w_uq, kv_norm_g, w_ukv, w_pool, pool_scale,
           gk_up_fwd, gk_bias_fwd, gk_up_bwd, gk_bias_bwd, gla_norm_g, w_out, final_norm_g):
    depth = w_in.shape[0]
    layers = [_layer_weights(l, norm_g, w_in, q_norm_g, w_uq, kv_norm_g, w_ukv, w_pool, pool_scale,
                             gk_up_fwd, gk_bias_fwd, gk_up_bwd, gk_bias_bwd, gla_norm_g, w_out)
              for l in range(depth)]
    fng = final_norm_g[None, :]
    return (_trunk(x_prompt, layers, fng), _trunk(x_sample, layers, fng))
```

```python
import functools
import math

import jax
import jax.numpy as jnp
import numpy as np
from jax import lax
from jax.experimental import pallas as pl
from jax.experimental.pallas import tpu as pltpu

F32 = jnp.float32
BF16 = jnp.bfloat16

D_MODEL = 1024
NORM_EPS = 1e-6
MLA_HEADS = 4
MLA_NOPE = 128
MLA_ROPE = 64
MLA_VDIM = 128
MLA_Q_LORA = 256
MLA_KV_LORA = 128
MLA_WIDTH = MLA_HEADS * MLA_VDIM
MLA_SCALE = (MLA_NOPE + MLA_ROPE) ** -0.5
ROPE_BASE = 10000.0
QK_WIDTH = 256
POOL_WINDOWS = (2, 4, 8, 16)
POOL_WIDTH = 256
POOL_GROUP_DIM = 64
POOL_HALO = 8
GLA_HEADS = 4
GLA_WIDTH = 256
GLA_KEY = 128
GLA_DK = 32
GLA_DV = 64
GLA_GATE_RANK = 16
GLA_GATE_NORM = 16.0
GLA_SUB = 32
GLA_TILE = 256
GLA_HALF = 128

MIX_WIDTH = MLA_WIDTH + POOL_WIDTH + GLA_WIDTH
IN_SIZES = (MLA_Q_LORA, MLA_KV_LORA, MLA_ROPE, MLA_WIDTH, POOL_WIDTH, POOL_WIDTH,
            GLA_KEY, GLA_KEY, GLA_WIDTH, GLA_GATE_RANK, GLA_GATE_RANK, GLA_WIDTH)
IN_OFFS = tuple(int(v) for v in np.cumsum((0,) + IN_SIZES))

ZA_WIDTH = MLA_Q_LORA + MLA_KV_LORA + 2 * MLA_ROPE
ZGATE_WIDTH = MLA_WIDTH + POOL_WIDTH + GLA_WIDTH
ZGLA_WIDTH = 2 * GLA_KEY + GLA_WIDTH
GLA_LR_PAD = 128

ROW_TILE = 512
ATTN_TQ = 512
ATTN_TK = 1024
VMEM_LIMIT = 56 * 1024 * 1024


def _rms(x, g):
    return x * lax.rsqrt(jnp.mean(x * x, axis=-1, keepdims=True) + NORM_EPS) * g


def _dot(a, b):
    return jnp.dot(a, b, preferred_element_type=F32)


def _dot_nt(a, b):
    return lax.dot_general(a, b, (((1,), (1,)), ((), ())), preferred_element_type=F32)


N_PROJ_OPERANDS = 14
N_PROJ_OUTS = 7


def _proj_kernel(x_ref, *refs):
    _proj_body(x_ref[0], *refs)


def _proj_body(x, ng_ref, wa_ref, wgate_ref, wpu_ref, wgla_ref, qg_ref, wuqt_ref, kvg_ref, wuk_ref,
               wuvt_ref, gup_ref, gbias_ref, tab_ref, tabt_ref,
               qt_ref, k_ref, vt_ref, zgate_ref, upool_ref, zgla_ref, glog_ref):
    h = _rms(x, ng_ref[...]).astype(BF16)
    za = _dot(h, wa_ref[...])
    zgate_ref[0] = _dot(h, wgate_ref[...]).astype(BF16)
    upool_ref[0] = _dot(h, wpu_ref[...])
    zgla = _dot(h, wgla_ref[...])
    zgla_ref[0] = zgla[:, :ZGLA_WIDTH].astype(BF16)
    zg = _dot(zgla[:, ZGLA_WIDTH:].astype(BF16), gup_ref[...]) + gbias_ref[...]
    glog_ref[0] = (jnp.minimum(zg, 0.0) - jnp.log1p(jnp.exp(-jnp.abs(zg)))) * (1.0 / GLA_GATE_NORM)
    tab = tab_ref[...]
    tabt = tabt_ref[...]
    qscale = MLA_SCALE * math.log2(math.e)
    cqn = _rms(za[:, :MLA_Q_LORA], qg_ref[...]).astype(BF16)
    qall = _dot_nt(wuqt_ref[...], cqn)
    zeros = jnp.zeros((QK_WIDTH - MLA_NOPE - MLA_ROPE, qall.shape[1]), BF16)
    for hh in range(MLA_HEADS):
        base = hh * QK_WIDTH
        qn = qall[base:base + MLA_NOPE] * qscale
        t = qall[base + MLA_NOPE:base + QK_WIDTH] * tabt
        r = (t[:MLA_ROPE] + t[MLA_ROPE:]) * qscale
        qt_ref[0, hh, 0:MLA_NOPE] = qn.astype(BF16)
        qt_ref[0, hh, MLA_NOPE:MLA_NOPE + MLA_ROPE] = r.astype(BF16)
        qt_ref[0, hh, MLA_NOPE + MLA_ROPE:] = zeros
    ckvn = _rms(za[:, MLA_Q_LORA:MLA_Q_LORA + MLA_KV_LORA], kvg_ref[...]).astype(BF16)
    kn = _dot(ckvn, wuk_ref[...])
    t = za[:, MLA_Q_LORA + MLA_KV_LORA:] * tab
    kr = t + pltpu.roll(t, MLA_ROPE, 1)
    lane = lax.broadcasted_iota(jnp.int32, kr.shape, 1)
    kr = jnp.where(lane < MLA_ROPE, kr, 0.0).astype(BF16)
    for hh in range(MLA_HEADS):
        k_ref[0, hh, :, 0:MLA_NOPE] = kn[:, hh * MLA_NOPE:(hh + 1) * MLA_NOPE].astype(BF16)
        k_ref[0, hh, :, MLA_NOPE:QK_WIDTH] = kr
    vt = _dot_nt(wuvt_ref[...], ckvn)
    for hh in range(MLA_HEADS):
        vt_ref[0, hh] = vt[hh * MLA_VDIM:(hh + 1) * MLA_VDIM].astype(BF16)


def _full_spec(a):
    return pl.BlockSpec(a.shape, lambda b, i: (0,) * a.ndim)


def _proj_operands(lw, tab, tm):
    weights = (lw["norm_g"], lw["wa"], lw["wgate"], lw["wpu"], lw["wgla"], lw["q_norm_g"], lw["wuqt"],
               lw["kv_norm_g"], lw["wuk"], lw["wuvt"], lw["gup"], lw["gbias"])
    specs = [_full_spec(w) for w in weights] + [pl.BlockSpec((tm, 128), lambda b, i: (i, 0)),
                                                pl.BlockSpec((128, tm), lambda b, i: (0, i))]
    operands = weights + tuple(tab)
    assert len(operands) == N_PROJ_OPERANDS
    return operands, specs


def _proj_outputs(B, S, tm):
    specs = [
        pl.BlockSpec((1, MLA_HEADS, QK_WIDTH, tm), lambda b, i: (b, 0, 0, i)),
        pl.BlockSpec((1, MLA_HEADS, tm, QK_WIDTH), lambda b, i: (b, 0, i, 0)),
        pl.BlockSpec((1, MLA_HEADS, MLA_VDIM, tm), lambda b, i: (b, 0, 0, i)),
        pl.BlockSpec((1, tm, ZGATE_WIDTH), lambda b, i: (b, i, 0)),
        pl.BlockSpec((1, tm, POOL_WIDTH), lambda b, i: (b, i, 0)),
        pl.BlockSpec((1, tm, ZGLA_WIDTH), lambda b, i: (b, i, 0)),
        pl.BlockSpec((1, tm, 2 * GLA_KEY), lambda b, i: (b, i, 0)),
    ]
    shapes = [
        jax.ShapeDtypeStruct((B, MLA_HEADS, QK_WIDTH, S), BF16),
        jax.ShapeDtypeStruct((B, MLA_HEADS, S, QK_WIDTH), BF16),
        jax.ShapeDtypeStruct((B, MLA_HEADS, MLA_VDIM, S), BF16),
        jax.ShapeDtypeStruct((B, S, ZGATE_WIDTH), BF16),
        jax.ShapeDtypeStruct((B, S, POOL_WIDTH), F32),
        jax.ShapeDtypeStruct((B, S, ZGLA_WIDTH), BF16),
        jax.ShapeDtypeStruct((B, S, 2 * GLA_KEY), F32),
    ]
    assert len(specs) == N_PROJ_OUTS
    return specs, shapes


def _proj(x, lw, tab):
    B, S, _ = x.shape
    tm = min(ROW_TILE, S)
    operands, op_specs = _proj_operands(lw, tab, tm)
    out_specs, out_shape = _proj_outputs(B, S, tm)
    return pl.pallas_call(
        _proj_kernel,
        grid=(B, S // tm),
        in_specs=[pl.BlockSpec((1, tm, D_MODEL), lambda b, i: (b, i, 0))] + op_specs,
        out_specs=out_specs,
        out_shape=out_shape,
        compiler_params=pltpu.CompilerParams(
            dimension_semantics=("parallel", "parallel"), vmem_limit_bytes=VMEM_LIMIT),
        name="proj",
    )(x, *operands)


def _attn_kernel(qt_ref, qn_ref, k_ref, vt_ref, o_ref, s2_ref, mxa_ref, acc_ref, *, tk, nk):
    tq = qt_ref.shape[3]
    zero = jnp.minimum(pl.program_id(2), 0)
    sa_ref = s2_ref.at[zero]
    sb_ref = s2_ref.at[zero + 1]

    def scores(i, q_ref, s_ref):
        start = pl.multiple_of(i * tk, tk)
        s = _dot(k_ref[0, 0, pl.ds(start, tk), :], q_ref[0, 0])
        s_ref[:, 0:tq] = s
        return jnp.max(s, axis=0, keepdims=True)

    def update(i, s_ref, mx, m, l):
        start = pl.multiple_of(i * tk, tk)
        m_new = jnp.maximum(m, mx)
        alpha = jnp.exp2(m - m_new)
        p = jnp.exp2(s_ref[:, 0:tq] - m_new)
        l = alpha * l + jnp.sum(p, axis=0, keepdims=True)
        acc_ref[...] = alpha * acc_ref[...] + _dot(vt_ref[0, 0, :, pl.ds(start, tk)], p.astype(BF16))
        return m_new, l

    @pl.when(pl.program_id(2) == 0)
    def _():
        mxa_ref[...] = scores(0, qt_ref, sa_ref)

    def pair(i, next_chunk, next_q_ref, carry):
        m, l, mx_a = carry
        mx_b = scores(i + 1, qt_ref, sb_ref)
        m, l = update(i, sa_ref, mx_a, m, l)
        mx_a = scores(next_chunk, next_q_ref, sa_ref)
        m, l = update(i + 1, sb_ref, mx_b, m, l)
        return m, l, mx_a

    m = jnp.full((1, tq), -jnp.inf, F32)
    l = jnp.zeros((1, tq), F32)
    acc_ref[...] = jnp.zeros_like(acc_ref)
    carry = lax.fori_loop(0, nk // 2 - 1, lambda j, c: pair(2 * j, 2 * j + 2, qt_ref, c),
                          (m, l, mxa_ref[...]), unroll=True)
    m, l, mx_a = pair(nk - 2, 0, qn_ref, carry)
    mxa_ref[...] = mx_a
    o_ref[0] = (acc_ref[...] / l).T.astype(o_ref.dtype)


def _attn(qt, k, vt):
    B, H, S, _ = k.shape
    tq = min(ATTN_TQ, S)
    tk = min(ATTN_TK, S // 2)
    nk = S // tk
    nq = S // tq
    assert nk % 2 == 0
    return pl.pallas_call(
        functools.partial(_attn_kernel, tk=tk, nk=nk),
        grid=(B, H, nq),
        in_specs=[
            pl.BlockSpec((1, 1, QK_WIDTH, tq), lambda b, h, i: (b, h, 0, i)),
            pl.BlockSpec((1, 1, QK_WIDTH, tq), lambda b, h, i: (b, h, 0, jnp.minimum(i + 1, nq - 1))),
            pl.BlockSpec((1, 1, S, QK_WIDTH), lambda b, h, i: (b, h, 0, 0)),
            pl.BlockSpec((1, 1, MLA_VDIM, S), lambda b, h, i: (b, h, 0, 0)),
        ],
        out_specs=pl.BlockSpec((1, tq, MLA_VDIM), lambda b, h, i: (b, i, h)),
        out_shape=jax.ShapeDtypeStruct((B, S, MLA_WIDTH), BF16),
        scratch_shapes=[pltpu.VMEM((2, tk, tq + 128), F32),
                        pltpu.VMEM((1, tq), F32), pltpu.VMEM((MLA_VDIM, tq), F32)],
        compiler_params=pltpu.CompilerParams(
            dimension_semantics=("parallel", "parallel", "arbitrary"), vmem_limit_bytes=VMEM_LIMIT),
        name="attn",
    )(qt, qt, k, vt)


def _split2(x):
    hi = x.astype(BF16)
    lo = (x - hi.astype(F32)).astype(BF16)
    return jnp.concatenate([hi, lo], axis=1)


def _sum2(y, w):
    return y[:, :w] + y[:, w:]


def _gla_prep(z, g, pin, pmid, pex):
    T = z.shape[0]
    nsub = T // GLA_SUB
    q = z[:, 0:GLA_KEY].astype(F32) * (GLA_DK ** -0.5)
    k = z[:, GLA_KEY:2 * GLA_KEY].astype(F32)
    g2 = _split2(g)
    b_in = _sum2(_dot(pin, g2), GLA_KEY)
    yield
    b_mid = _sum2(_dot(pmid, g2), GLA_KEY)
    yield
    b_ex = _sum2(_dot(pex, g2), GLA_KEY)
    yield
    qt = (q * jnp.exp(b_in)).astype(BF16)
    qa = (q * jnp.exp(b_in - b_mid)).astype(BF16)
    kt = (k * jnp.exp(b_mid - b_in)).astype(BF16)
    kp = k * jnp.exp(b_ex)
    dd = jnp.exp(b_in + b_ex)
    kpt = kp.T
    sub_of_lane = lax.broadcasted_iota(jnp.int32, (1, T), 1) // GLA_SUB
    stack = jnp.concatenate(
        [jnp.where(sub_of_lane == c, kpt, 0.0).astype(BF16) for c in range(nsub)], axis=0)
    return qt, qa, kt, stack, dd.T


def _gla_main(v, qt, qa16, kt, stack, ddt, pin, state, reverse):
    T = v.shape[0]
    nsub = T // GLA_SUB
    HT = GLA_HALF
    qhead = lax.broadcasted_iota(jnp.int32, (1, GLA_KEY), 1) // GLA_DK
    vhead = lax.broadcasted_iota(jnp.int32, (1, GLA_WIDTH), 1) // GLA_DV
    half_mask = pin[0:HT, 0:HT] > 0
    pmask = jnp.concatenate([half_mask, half_mask], axis=1)
    a = []
    for hf in range(T // HT):
        rows = slice(hf * HT, (hf + 1) * HT)
        for hp in range(GLA_HEADS // 2):
            kstack = jnp.concatenate([jnp.where(qhead == 2 * hp + e, kt[rows], jnp.zeros_like(kt[rows]))
                                      for e in range(2)], axis=0)
            a.append(jnp.where(pmask, _dot_nt(qa16[rows], kstack), 0.0).astype(BF16))
            yield

    inc = _dot(stack, v)
    yield
    same_head = (lax.broadcasted_iota(jnp.int32, (GLA_KEY, 1), 0) // GLA_DK) == vhead

    o_halves = []
    for hf in range(T // HT):
        rows = slice(hf * HT, (hf + 1) * HT)
        o_half = jnp.zeros((HT, GLA_WIDTH), F32)
        for hp in range(GLA_HEADS // 2):
            vstack = jnp.concatenate([jnp.where(vhead == 2 * hp + e, v[rows], jnp.zeros_like(v[rows]))
                                      for e in range(2)], axis=0)
            o_half = o_half + _dot(a[hf * (GLA_HEADS // 2) + hp], vstack)
            yield
        o_halves.append(o_half)
    o = jnp.concatenate(o_halves, axis=0)

    inter = [None] * nsub
    order = range(nsub - 1, -1, -1) if reverse else range(nsub)
    for c in order:
        rows = slice(c * GLA_SUB, (c + 1) * GLA_SUB)
        inter[c] = _dot(qt[rows], state.astype(BF16))
        state = state * ddt[:, c * GLA_SUB:c * GLA_SUB + 1] + jnp.where(
            same_head, inc[c * GLA_KEY:(c + 1) * GLA_KEY], 0.0)
        yield
    return o + jnp.concatenate(inter, axis=0), state


def _lockstep(gens):
    results = [None] * len(gens)
    live = list(range(len(gens)))
    while live:
        for idx in list(live):
            try:
                next(gens[idx])
            except StopIteration as done:
                results[idx] = done.value
                live.remove(idx)
    return results


def _gla_tile(z, g, pin, pmid, pex, state, reverse):
    prep = yield from _gla_prep(z, g, pin, pmid, pex)
    return (yield from _gla_main(z[:, 2 * GLA_KEY:2 * GLA_KEY + GLA_WIDTH], *prep, pin, state, reverse))


def _gla_kernel(zf_ref, zb_ref, gf_ref, gb_ref, pinf_ref, pmidf_ref, pexf_ref,
                pinb_ref, pmidb_ref, pexb_ref, of_ref, ob_ref, sf_ref, sb_ref):
    @pl.when(pl.program_id(1) == 0)
    def _():
        sf_ref[...] = jnp.zeros_like(sf_ref)
        sb_ref[...] = jnp.zeros_like(sb_ref)

    (of, sf), (ob, sb) = _lockstep([
        _gla_tile(zf_ref[0], gf_ref[0], pinf_ref[...], pmidf_ref[...], pexf_ref[...], sf_ref[...], False),
        _gla_tile(zb_ref[0], gb_ref[0], pinb_ref[...], pmidb_ref[...], pexb_ref[...], sb_ref[...], True)])
    of_ref[0] = of.astype(of_ref.dtype)
    ob_ref[0] = ob.astype(ob_ref.dtype)
    sf_ref[...] = sf
    sb_ref[...] = sb


def _gla_masks(T, reverse):
    t = np.arange(T)
    same = (t[:, None] // GLA_SUB) == (t[None, :] // GLA_SUB)
    before = (t[None, :] >= t[:, None]) if reverse else (t[None, :] <= t[:, None])
    pin = same & before
    pex = same & ~before
    in_sub = t[None, :] % GLA_SUB
    first_half = (in_sub >= GLA_SUB // 2) if reverse else (in_sub < GLA_SUB // 2)
    pmid = same & first_half
    return jnp.asarray(pin, BF16), jnp.asarray(pmid, BF16), jnp.asarray(pex, BF16)


def _gla(zgla, glog):
    B, S, _ = zgla.shape
    T = min(GLA_TILE, S)
    n = S // T
    consts = _gla_masks(T, False) + _gla_masks(T, True)
    fwd = lambda b, i: (b, i, 0)
    bwd = lambda b, i: (b, n - 1 - i, 0)
    return pl.pallas_call(
        _gla_kernel,
        grid=(B, n),
        in_specs=[pl.BlockSpec((1, T, ZGLA_WIDTH), fwd), pl.BlockSpec((1, T, ZGLA_WIDTH), bwd),
                  pl.BlockSpec((1, T, GLA_KEY), fwd), pl.BlockSpec((1, T, GLA_KEY), lambda b, i: (b, n - 1 - i, 1))]
        + [_full_spec(c) for c in consts],
        out_specs=[pl.BlockSpec((1, T, GLA_WIDTH), fwd), pl.BlockSpec((1, T, GLA_WIDTH), bwd)],
        out_shape=[jax.ShapeDtypeStruct((B, S, GLA_WIDTH), BF16)] * 2,
        scratch_shapes=[pltpu.VMEM((GLA_KEY, GLA_WIDTH), F32)] * 2,
        compiler_params=pltpu.CompilerParams(
            dimension_semantics=("parallel", "arbitrary"), vmem_limit_bytes=VMEM_LIMIT),
        name="gla",
    )(zgla, zgla, glog, glog, *consts)


def _silu(x):
    h = 0.5 * x
    return h + h * jnp.tanh(h)


N_MIX_INPUTS = 13


def _mix_final_kernel(*refs, seq_len):
    ins, (fng_ref, out_ref), scratch = refs[:N_MIX_INPUTS], refs[N_MIX_INPUTS:N_MIX_INPUTS + 2], refs[-3:]
    out_ref[0] = _rms(_mix_body(*ins, *scratch, seq_len=seq_len), fng_ref[...])


def _mix_proj_kernel(*refs, seq_len):
    ins, scratch = refs[:N_MIX_INPUTS], refs[-3:]
    proj_ops = refs[N_MIX_INPUTS:N_MIX_INPUTS + N_PROJ_OPERANDS]
    out_ref = refs[N_MIX_INPUTS + N_PROJ_OPERANDS]
    proj_outs = refs[N_MIX_INPUTS + N_PROJ_OPERANDS + 1:-3]
    y = _mix_body(*ins, *scratch, seq_len=seq_len)
    out_ref[0] = y
    _proj_body(y, *proj_ops, *proj_outs)


def _mix_body(x_ref, omla_ref, zgate_ref, u_ref, uprev_ref, unext_ref, gf_ref, gb_ref, wpool_ref, pscale_ref,
              gng_ref, hsum_ref, wout_ref, ubuf, lvl_a, lvl_b, *, seq_len):
    tm = x_ref.shape[1]
    i = pl.program_id(1)
    zg = zgate_ref[0].astype(F32)
    gate_mla = zg[:, 0:MLA_WIDTH]
    gate_pool = zg[:, MLA_WIDTH:MLA_WIDTH + POOL_WIDTH]
    gate_gla = zg[:, MLA_WIDTH + POOL_WIDTH:]
    u = u_ref[0]

    H = POOL_HALO
    lo, hi = H, tm + 3 * H
    pad = jnp.zeros((H, POOL_WIDTH), F32)
    for buf in (ubuf, lvl_a, lvl_b):
        buf[0:H] = pad[:, 0:buf.shape[1]]
        buf[hi:] = pad[:, 0:buf.shape[1]]
    ubuf[lo:2 * H] = jnp.where(i > 0, uprev_ref[0], 0.0)
    ubuf[2 * H:2 * H + tm] = u
    ubuf[2 * H + tm:hi] = jnp.where(i < pl.num_programs(1) - 1, unext_ref[0], 0.0)

    def level(src, cols, before, after):
        return src[lo - before:hi - before, cols] + src[lo + after:hi + after, cols]

    tile_rows = slice(H, H + tm)
    all_cols = slice(0, 128)
    p2 = level(ubuf, slice(0, 128), 1, 0)
    lvl_a[lo:hi] = p2
    p4 = level(lvl_a, all_cols, 1, 1)
    sums = [(p2[tile_rows], p4[tile_rows])]
    p2 = level(ubuf, slice(128, 256), 1, 0)
    lvl_a[lo:hi] = p2
    lvl_b[lo:hi] = level(lvl_a, all_cols, 1, 1)
    p8 = level(lvl_b, all_cols, 2, 2)
    lvl_a[lo:hi] = p8
    p16 = level(lvl_a, all_cols, 4, 4)
    sums.append((p8[tile_rows], p16[tile_rows]))

    pos = i * tm + lax.broadcasted_iota(jnp.int32, (tm, 128), 0)
    lane = lax.broadcasted_iota(jnp.int32, (1, 128), 1)
    cnt = lambda w: (jnp.minimum(pos + w // 2, seq_len) - jnp.maximum(pos - w // 2, 0)).astype(F32)
    pooled = [jnp.where(lane < POOL_GROUP_DIM, small / cnt(ws), big / cnt(wb))
              for (small, big), (ws, wb) in zip(sums, ((2, 4), (8, 16)))]
    pooled = jnp.concatenate(pooled, axis=1) - u
    o_pool = _dot(pooled.astype(BF16), wpool_ref[...]) * pscale_ref[...]

    og = gf_ref[0].astype(F32) + gb_ref[0].astype(F32)
    sq = og * og
    sq_hi = sq.astype(BF16)
    sq_lo = (sq - sq_hi.astype(F32)).astype(BF16)
    ms = (_dot(sq_hi, hsum_ref[...]) + _dot(sq_lo, hsum_ref[...])) * (1.0 / GLA_DV)
    o_gla = og * lax.rsqrt(ms + NORM_EPS) * gng_ref[...]

    m_mla = (omla_ref[0].astype(F32) * _silu(gate_mla)).astype(BF16)
    m_pool = (o_pool * _silu(gate_pool)).astype(BF16)
    m_gla = (o_gla * _silu(gate_gla)).astype(BF16)
    return (x_ref[0] + _dot(m_mla, wout_ref[0:MLA_WIDTH])
            + _dot(m_pool, wout_ref[MLA_WIDTH:MLA_WIDTH + POOL_WIDTH])
            + _dot(m_gla, wout_ref[MLA_WIDTH + POOL_WIDTH:]))


def _mix(x, omla, zgate, upool, gf, gb, lw, fng=None, next_lw=None, tab=None):
    B, S, _ = x.shape
    tm = min(ROW_TILE, S)
    hb = tm // POOL_HALO
    nhalo = S // POOL_HALO
    tile = lambda w: pl.BlockSpec((1, tm, w), lambda b, i: (b, i, 0))
    weights = (lw["wpool"], lw["pool_scale"], lw["gla_norm_g"], lw["hsum"], lw["wout"])
    inputs = (x, omla, zgate, upool, upool, upool, gf, gb) + weights
    assert len(inputs) == N_MIX_INPUTS
    in_specs = [
        tile(D_MODEL), tile(MLA_WIDTH), tile(ZGATE_WIDTH), tile(POOL_WIDTH),
        pl.BlockSpec((1, POOL_HALO, POOL_WIDTH), lambda b, i: (b, jnp.maximum(i * hb - 1, 0), 0)),
        pl.BlockSpec((1, POOL_HALO, POOL_WIDTH), lambda b, i: (b, jnp.minimum((i + 1) * hb, nhalo - 1), 0)),
        tile(GLA_WIDTH), tile(GLA_WIDTH),
    ] + [_full_spec(w) for w in weights]
    out_specs = [tile(D_MODEL)]
    out_shape = [jax.ShapeDtypeStruct((B, S, D_MODEL), F32)]
    if next_lw is None:
        body, name = _mix_final_kernel, "mix_final"
        inputs += (fng,)
        in_specs.append(_full_spec(fng))
    else:
        body, name = _mix_proj_kernel, "mix_proj"
        operands, op_specs = _proj_operands(next_lw, tab, tm)
        inputs += operands
        in_specs += op_specs
        proj_specs, proj_shapes = _proj_outputs(B, S, tm)
        out_specs += proj_specs
        out_shape += proj_shapes
    return pl.pallas_call(
        functools.partial(body, seq_len=S),
        grid=(B, S // tm),
        in_specs=in_specs,
        out_specs=out_specs,
        out_shape=out_shape,
        scratch_shapes=[pltpu.VMEM((tm + 4 * POOL_HALO, POOL_WIDTH), F32),
                        pltpu.VMEM((tm + 4 * POOL_HALO, 128), F32),
                        pltpu.VMEM((tm + 4 * POOL_HALO, 128), F32)],
        compiler_params=pltpu.CompilerParams(
            dimension_semantics=("parallel", "parallel"), vmem_limit_bytes=VMEM_LIMIT),
        name=name,
    )(*inputs)


def _rope_table(seq):
    inv_freq = 1.0 / (ROPE_BASE ** (jnp.arange(0, MLA_ROPE, 2, dtype=F32) / MLA_ROPE))
    ang = jnp.arange(seq, dtype=F32)[:, None] * inv_freq[None, :]
    cos, sin = jnp.cos(ang), jnp.sin(ang)
    return jnp.concatenate([cos, cos, -sin, sin], axis=1)


def _swap_halves(w):
    half = w.shape[-1] // 2
    return jnp.concatenate([w[..., half:], w[..., :half]], axis=-1)


def _layer_weights(l, norm_g, w_in, q_norm_g, w_uq, kv_norm_g, w_ukv, w_pool, pool_scale,
                   gk_up_fwd, gk_bias_fwd, gk_up_bwd, gk_bias_bwd, gla_norm_g, w_out):
    cols = [w_in[l][:, IN_OFFS[j]:IN_OFFS[j + 1]] for j in range(len(IN_SIZES))]
    (c_q, c_kv, k_rope, gate_mla, u_pool, gate_pool, q_gla, k_gla, v_gla, lr_f, lr_b, gate_gla) = cols
    lr_pad = jnp.zeros((D_MODEL, GLA_LR_PAD - 2 * GLA_GATE_RANK), F32)
    wa = jnp.concatenate([c_q, c_kv, k_rope, _swap_halves(k_rope)], axis=1)
    wgate = jnp.concatenate([gate_mla, gate_pool, gate_gla], axis=1)
    wgla = jnp.concatenate([q_gla, k_gla, v_gla, lr_f, lr_b, lr_pad], axis=1)
    uq = w_uq[l].reshape(MLA_Q_LORA, MLA_HEADS, MLA_NOPE + MLA_ROPE)
    uq_rope = uq[..., MLA_NOPE:]
    wuq = jnp.concatenate([uq, _swap_halves(uq_rope)], axis=-1).reshape(MLA_Q_LORA, MLA_HEADS * QK_WIDTH)
    ukv = w_ukv[l].reshape(MLA_KV_LORA, MLA_HEADS, MLA_NOPE + MLA_VDIM)
    wuk = ukv[..., :MLA_NOPE].reshape(MLA_KV_LORA, MLA_HEADS * MLA_NOPE)
    wuvt = ukv[..., MLA_NOPE:].reshape(MLA_KV_LORA, MLA_HEADS * MLA_VDIM).T
    wpool = jnp.zeros((POOL_WIDTH, POOL_WIDTH), F32)
    for gi in range(len(POOL_WINDOWS)):
        sl = slice(gi * POOL_GROUP_DIM, (gi + 1) * POOL_GROUP_DIM)
        wpool = wpool.at[sl, sl].set(w_pool[l, gi])
    gup = jnp.zeros((GLA_LR_PAD, 2 * GLA_KEY), F32)
    gup = gup.at[:GLA_GATE_RANK, :GLA_KEY].set(gk_up_fwd[l])
    gup = gup.at[GLA_GATE_RANK:2 * GLA_GATE_RANK, GLA_KEY:].set(gk_up_bwd[l])
    gbias = jnp.concatenate([gk_bias_fwd[l], gk_bias_bwd[l]])[None, :]
    head = np.arange(GLA_WIDTH) // GLA_DV
    hsum = jnp.asarray(head[:, None] == head[None, :], BF16)
    return {
        "norm_g": norm_g[l][None, :],
        "wa": wa.astype(BF16), "wgate": wgate.astype(BF16), "wpu": u_pool.astype(BF16), "wgla": wgla.astype(BF16),
        "q_norm_g": q_norm_g[l][None, :], "wuqt": wuq.T.astype(BF16),
        "kv_norm_g": kv_norm_g[l][None, :], "wuk": wuk.astype(BF16), "wuvt": wuvt.astype(BF16),
        "wpool": wpool.astype(BF16), "pool_scale": pool_scale[l][None, :],
        "gup": gup.astype(BF16), "gbias": gbias,
        "gla_norm_g": jnp.tile(gla_norm_g[l], GLA_HEADS)[None, :], "hsum": hsum,
        "wout": w_out[l].astype(BF16),
    }


def _trunk(x, layers, fng):
    tab = _rope_table(x.shape[1])
    tab = (tab, tab.T)
    q, k, vt, zgate, upool, zgla, glog = _proj(x, layers[0], tab)
    for l, lw in enumerate(layers):
        omla = _attn(q, k, vt)
        gf, gb = _gla(zgla, glog)
        if l + 1 < len(layers):
            x, q, k, vt, zgate, upool, zgla, glog = _mix(x, omla, zgate, upool, gf, gb, lw,
                                                         next_lw=layers[l + 1], tab=tab)
        else:
            (x,) = _mix(x, omla, zgate, upool, gf, gb, lw, fng=fng)
    return x


def kernel(x_prompt, x_sample, norm_g, w_in, q_norm_g, w_uq, kv_norm_g, w_ukv, w_pool, pool_scale,
           gk_up_fwd, gk_bias_fwd, gk_up_bwd, gk_bias_bwd, gla_norm_g, w_out, final_norm_g):
    depth = w_in.shape[0]
    layers = [_layer_weights(l, norm_g, w_in, q_norm_g, w_uq, kv_norm_g, w_ukv, w_pool, pool_scale,
                             gk_up_fwd, gk_bias_fwd, gk_up_bwd, gk_bias_bwd, gla_norm_g, w_out)
              for l in range(depth)]
    fng = final_norm_g[None, :]
    return (_trunk(x_prompt, layers, fng), _trunk(x_sample, layers, fng))
```

```python
import functools
import math

import jax
import jax.numpy as jnp
import numpy as np
from jax import lax
from jax.experimental import pallas as pl
from jax.experimental.pallas import tpu as pltpu

F32 = jnp.float32
BF16 = jnp.bfloat16

D_MODEL = 1024
NORM_EPS = 1e-6
MLA_HEADS = 4
MLA_NOPE = 128
MLA_ROPE = 64
MLA_VDIM = 128
MLA_Q_LORA = 256
MLA_KV_LORA = 128
MLA_WIDTH = MLA_HEADS * MLA_VDIM
MLA_SCALE = (MLA_NOPE + MLA_ROPE) ** -0.5
ROPE_BASE = 10000.0
QK_WIDTH = 256
POOL_WINDOWS = (2, 4, 8, 16)
POOL_WIDTH = 256
POOL_GROUP_DIM = 64
POOL_HALO = 8
GLA_HEADS = 4
GLA_WIDTH = 256
GLA_KEY = 128
GLA_DK = 32
GLA_DV = 64
GLA_GATE_RANK = 16
GLA_GATE_NORM = 16.0
GLA_SUB = 32
GLA_TILE = 256
GLA_HALF = 128

MIX_WIDTH = MLA_WIDTH + POOL_WIDTH + GLA_WIDTH
IN_SIZES = (MLA_Q_LORA, MLA_KV_LORA, MLA_ROPE, MLA_WIDTH, POOL_WIDTH, POOL_WIDTH,
            GLA_KEY, GLA_KEY, GLA_WIDTH, GLA_GATE_RANK, GLA_GATE_RANK, GLA_WIDTH)
IN_OFFS = tuple(int(v) for v in np.cumsum((0,) + IN_SIZES))

ZA_WIDTH = MLA_Q_LORA + MLA_KV_LORA + 2 * MLA_ROPE
ZGATE_WIDTH = MLA_WIDTH + POOL_WIDTH + GLA_WIDTH
ZGLA_WIDTH = 2 * GLA_KEY + GLA_WIDTH
GLA_LR_PAD = 128

ROW_TILE = 512
ATTN_TQ = 512
ATTN_TK = 1024
ATTN_TILES_PER_STEP = 4
VMEM_LIMIT = 56 * 1024 * 1024


def _rms(x, g):
    return x * lax.rsqrt(jnp.mean(x * x, axis=-1, keepdims=True) + NORM_EPS) * g


def _dot(a, b):
    return jnp.dot(a, b, preferred_element_type=F32)


def _dot_nt(a, b):
    return lax.dot_general(a, b, (((1,), (1,)), ((), ())), preferred_element_type=F32)


N_PROJ_OPERANDS = 14
N_PROJ_OUTS = 7


def _proj_kernel(x_ref, *refs):
    _lockstep([_proj_body(_proj_norm(x_ref[0], refs[0]), *refs)])


def _proj_norm(x, ng_ref):
    return _rms(x, ng_ref[...]).astype(BF16)


def _proj_body(h, ng_ref, wa_ref, wgate_ref, wpu_ref, wgla_ref, qg_ref, wuqt_ref, kvg_ref, wuk_ref,
               wuvt_ref, gup_ref, gbias_ref, tab_ref, tabt_ref,
               qt_ref, k_ref, vt_ref, zgate_ref, upool_ref, zgla_ref, glog_ref):
    za = _dot(h, wa_ref[...])
    yield
    zgate_ref[0] = _dot(h, wgate_ref[...]).astype(BF16)
    yield
    upool_ref[0] = _dot(h, wpu_ref[...])
    yield
    zgla = _dot(h, wgla_ref[...])
    yield
    zgla_ref[0] = zgla[:, :ZGLA_WIDTH].astype(BF16)
    zg = _dot(zgla[:, ZGLA_WIDTH:].astype(BF16), gup_ref[...]) + gbias_ref[...]
    yield
    glog_ref[0] = (jnp.minimum(zg, 0.0) - jnp.log1p(jnp.exp(-jnp.abs(zg)))) * (1.0 / GLA_GATE_NORM)
    tab = tab_ref[...]
    tabt = tabt_ref[...]
    qscale = MLA_SCALE * math.log2(math.e)
    cqn = _rms(za[:, :MLA_Q_LORA], qg_ref[...]).astype(BF16)
    qall = _dot_nt(wuqt_ref[...], cqn)
    yield
    zeros = jnp.zeros((QK_WIDTH - MLA_NOPE - MLA_ROPE, qall.shape[1]), BF16)
    for hh in range(MLA_HEADS):
        base = hh * QK_WIDTH
        qn = qall[base:base + MLA_NOPE] * qscale
        t = qall[base + MLA_NOPE:base + QK_WIDTH] * tabt
        r = (t[:MLA_ROPE] + t[MLA_ROPE:]) * qscale
        qt_ref[0, hh, 0:MLA_NOPE] = qn.astype(BF16)
        qt_ref[0, hh, MLA_NOPE:MLA_NOPE + MLA_ROPE] = r.astype(BF16)
        qt_ref[0, hh, MLA_NOPE + MLA_ROPE:] = zeros
    ckvn = _rms(za[:, MLA_Q_LORA:MLA_Q_LORA + MLA_KV_LORA], kvg_ref[...]).astype(BF16)
    kn = _dot(ckvn, wuk_ref[...])
    yield
    t = za[:, MLA_Q_LORA + MLA_KV_LORA:] * tab
    kr = t + pltpu.roll(t, MLA_ROPE, 1)
    lane = lax.broadcasted_iota(jnp.int32, kr.shape, 1)
    kr = jnp.where(lane < MLA_ROPE, kr, 0.0).astype(BF16)
    for hh in range(MLA_HEADS):
        k_ref[0, hh, :, 0:MLA_NOPE] = kn[:, hh * MLA_NOPE:(hh + 1) * MLA_NOPE].astype(BF16)
        k_ref[0, hh, :, MLA_NOPE:QK_WIDTH] = kr
    vt = _dot_nt(wuvt_ref[...], ckvn)
    yield
    for hh in range(MLA_HEADS):
        vt_ref[0, hh] = vt[hh * MLA_VDIM:(hh + 1) * MLA_VDIM].astype(BF16)


def _full_spec(a):
    return pl.BlockSpec(a.shape, lambda *g: (0,) * a.ndim)


def _proj_operands(lw, tab, tm):
    weights = (lw["norm_g"], lw["wa"], lw["wgate"], lw["wpu"], lw["wgla"], lw["q_norm_g"], lw["wuqt"],
               lw["kv_norm_g"], lw["wuk"], lw["wuvt"], lw["gup"], lw["gbias"])
    specs = [_full_spec(w) for w in weights] + [pl.BlockSpec((tm, 128), lambda b, i: (i, 0)),
                                                pl.BlockSpec((128, tm), lambda b, i: (0, i))]
    operands = weights + tuple(tab)
    assert len(operands) == N_PROJ_OPERANDS
    return operands, specs


def _proj_outputs(B, S, tm):
    tok = lambda w: pl.BlockSpec((1, tm, w), lambda b, i: (b, i, 0))
    specs = [
        pl.BlockSpec((1, MLA_HEADS, QK_WIDTH, tm), lambda b, i: (b, 0, 0, i)),
        pl.BlockSpec((1, MLA_HEADS, tm, QK_WIDTH), lambda b, i: (b, 0, i, 0)),
        pl.BlockSpec((1, MLA_HEADS, MLA_VDIM, tm), lambda b, i: (b, 0, 0, i)),
        tok(ZGATE_WIDTH), tok(POOL_WIDTH), tok(ZGLA_WIDTH), tok(2 * GLA_KEY),
    ]
    shapes = [
        jax.ShapeDtypeStruct((B, MLA_HEADS, QK_WIDTH, S), BF16),
        jax.ShapeDtypeStruct((B, MLA_HEADS, S, QK_WIDTH), BF16),
        jax.ShapeDtypeStruct((B, MLA_HEADS, MLA_VDIM, S), BF16),
        jax.ShapeDtypeStruct((B, S, ZGATE_WIDTH), BF16),
        jax.ShapeDtypeStruct((B, S, POOL_WIDTH), F32),
        jax.ShapeDtypeStruct((B, S, ZGLA_WIDTH), BF16),
        jax.ShapeDtypeStruct((B, S, 2 * GLA_KEY), F32),
    ]
    assert len(specs) == N_PROJ_OUTS
    return specs, shapes


def _proj(x, lw, tab):
    B, S, _ = x.shape
    tm = min(ROW_TILE, S)
    operands, op_specs = _proj_operands(lw, tab, tm)
    out_specs, out_shape = _proj_outputs(B, S, tm)
    return pl.pallas_call(
        _proj_kernel,
        grid=(B, S // tm),
        in_specs=[pl.BlockSpec((1, tm, D_MODEL), lambda b, i: (b, i, 0))] + op_specs,
        out_specs=out_specs,
        out_shape=out_shape,
        compiler_params=pltpu.CompilerParams(
            dimension_semantics=("parallel", "parallel"), vmem_limit_bytes=VMEM_LIMIT),
        name="proj",
    )(x, *operands)


def _attn_kernel(qt_ref, qn_ref, k_ref, vt_ref, o_ref, s2_ref, mxa_ref, acc_ref, *, tk, nk):
    tq = qn_ref.shape[3]
    n_tiles = qt_ref.shape[3] // tq
    zero = jnp.minimum(pl.program_id(2), 0)
    sa_ref = s2_ref.at[zero]
    sb_ref = s2_ref.at[zero + 1]

    def scores(i, q, s_ref):
        q_ref, tile = q
        start = i * tk
        s = _dot(k_ref[0, 0, pl.ds(start, tk), :], q_ref[0, 0, :, tile * tq:(tile + 1) * tq])
        s_ref[:, 0:tq] = s
        return jnp.max(s, axis=0, keepdims=True)

    def update(i, s_ref, mx, m, l):
        start = i * tk
        m_new = jnp.maximum(m, mx)
        alpha = jnp.exp2(m - m_new)
        p = jnp.exp2(s_ref[:, 0:tq] - m_new)
        l = alpha * l + jnp.sum(p, axis=0, keepdims=True)
        acc_ref[...] = alpha * acc_ref[...] + _dot(vt_ref[0, 0, :, pl.ds(start, tk)], p.astype(BF16))
        return m_new, l

    @pl.when(pl.program_id(2) == 0)
    def _():
        mxa_ref[...] = scores(0, (qt_ref, 0), sa_ref)

    mx_a = mxa_ref[...]
    for tile in range(n_tiles):
        q = (qt_ref, tile)
        q_next = (qt_ref, tile + 1) if tile + 1 < n_tiles else (qn_ref, 0)
        m = jnp.full((1, tq), -jnp.inf, F32)
        l = jnp.zeros((1, tq), F32)
        acc_ref[...] = jnp.zeros_like(acc_ref)
        for i in range(0, nk, 2):
            mx_b = scores(i + 1, q, sb_ref)
            m, l = update(i, sa_ref, mx_a, m, l)
            mx_a = scores(i + 2, q, sa_ref) if i + 2 < nk else scores(0, q_next, sa_ref)
            m, l = update(i + 1, sb_ref, mx_b, m, l)
        o_ref[0, tile * tq:(tile + 1) * tq] = (acc_ref[...] / l).T.astype(o_ref.dtype)
    mxa_ref[...] = mx_a


def _attn(qt, k, vt):
    B, H, S, _ = k.shape
    tq = min(ATTN_TQ, S)
    tk = min(ATTN_TK, S // 2)
    nk = S // tk
    nq = S // tq
    per_step = min(ATTN_TILES_PER_STEP, nq)
    steps = nq // per_step
    assert nk % 2 == 0 and nq % per_step == 0
    return pl.pallas_call(
        functools.partial(_attn_kernel, tk=tk, nk=nk),
        grid=(B, H, steps),
        in_specs=[
            pl.BlockSpec((1, 1, QK_WIDTH, per_step * tq), lambda b, h, i: (b, h, 0, i)),
            pl.BlockSpec((1, 1, QK_WIDTH, tq), lambda b, h, i: (b, h, 0, jnp.minimum((i + 1) * per_step, nq - 1))),
            pl.BlockSpec((1, 1, S, QK_WIDTH), lambda b, h, i: (b, h, 0, 0)),
            pl.BlockSpec((1, 1, MLA_VDIM, S), lambda b, h, i: (b, h, 0, 0)),
        ],
        out_specs=pl.BlockSpec((1, per_step * tq, MLA_VDIM), lambda b, h, i: (b, i, h)),
        out_shape=jax.ShapeDtypeStruct((B, S, MLA_WIDTH), BF16),
        scratch_shapes=[pltpu.VMEM((2, tk, tq + 128), F32),
                        pltpu.VMEM((1, tq), F32), pltpu.VMEM((MLA_VDIM, tq), F32)],
        compiler_params=pltpu.CompilerParams(
            dimension_semantics=("parallel", "parallel", "arbitrary"), vmem_limit_bytes=VMEM_LIMIT),
        name="attn",
    )(qt, qt, k, vt)


def _split2(x):
    hi = x.astype(BF16)
    lo = (x - hi.astype(F32)).astype(BF16)
    return jnp.concatenate([hi, lo], axis=1)


def _sum2(y, w):
    return y[:, :w] + y[:, w:]


def _gla_prep(z, g, pin, pmid, pex):
    T = z.shape[0]
    nsub = T // GLA_SUB
    q = z[:, 0:GLA_KEY].astype(F32) * (GLA_DK ** -0.5)
    k = z[:, GLA_KEY:2 * GLA_KEY].astype(F32)
    g2 = _split2(g)
    b_in = _sum2(_dot(pin, g2), GLA_KEY)
    yield
    b_mid = _sum2(_dot(pmid, g2), GLA_KEY)
    yield
    b_ex = _sum2(_dot(pex, g2), GLA_KEY)
    yield
    qt = (q * jnp.exp(b_in)).astype(BF16)
    qa = (q * jnp.exp(b_in - b_mid)).astype(BF16)
    kt = (k * jnp.exp(b_mid - b_in)).astype(BF16)
    kp = k * jnp.exp(b_ex)
    dd = jnp.exp(b_in + b_ex)
    kpt = kp.T
    sub_of_lane = lax.broadcasted_iota(jnp.int32, (1, T), 1) // GLA_SUB
    stack = jnp.concatenate(
        [jnp.where(sub_of_lane == c, kpt, 0.0).astype(BF16) for c in range(nsub)], axis=0)
    return qt, qa, kt, stack, dd.T


def _gla_main(v, qt, qa16, kt, stack, ddt, pin, state, reverse):
    T = v.shape[0]
    nsub = T // GLA_SUB
    HT = GLA_HALF
    qhead = lax.broadcasted_iota(jnp.int32, (1, GLA_KEY), 1) // GLA_DK
    vhead = lax.broadcasted_iota(jnp.int32, (1, GLA_WIDTH), 1) // GLA_DV
    half_mask = pin[0:HT, 0:HT] > 0
    pmask = jnp.concatenate([half_mask, half_mask], axis=1)
    a = []
    for hf in range(T // HT):
        rows = slice(hf * HT, (hf + 1) * HT)
        for hp in range(GLA_HEADS // 2):
            kstack = jnp.concatenate([jnp.where(qhead == 2 * hp + e, kt[rows], jnp.zeros_like(kt[rows]))
                                      for e in range(2)], axis=0)
            a.append(jnp.where(pmask, _dot_nt(qa16[rows], kstack), 0.0).astype(BF16))
            yield

    inc = _dot(stack, v)
    yield
    same_head = (lax.broadcasted_iota(jnp.int32, (GLA_KEY, 1), 0) // GLA_DK) == vhead

    o_halves = []
    for hf in range(T // HT):
        rows = slice(hf * HT, (hf + 1) * HT)
        o_half = jnp.zeros((HT, GLA_WIDTH), F32)
        for hp in range(GLA_HEADS // 2):
            vstack = jnp.concatenate([jnp.where(vhead == 2 * hp + e, v[rows], jnp.zeros_like(v[rows]))
                                      for e in range(2)], axis=0)
            o_half = o_half + _dot(a[hf * (GLA_HEADS // 2) + hp], vstack)
            yield
        o_halves.append(o_half)
    o = jnp.concatenate(o_halves, axis=0)

    inter = [None] * nsub
    order = range(nsub - 1, -1, -1) if reverse else range(nsub)
    for c in order:
        rows = slice(c * GLA_SUB, (c + 1) * GLA_SUB)
        inter[c] = _dot(qt[rows], state.astype(BF16))
        state = state * ddt[:, c * GLA_SUB:c * GLA_SUB + 1] + jnp.where(
            same_head, inc[c * GLA_KEY:(c + 1) * GLA_KEY], 0.0)
        yield
    return o + jnp.concatenate(inter, axis=0), state


def _lockstep(gens):
    results = [None] * len(gens)
    live = list(range(len(gens)))
    while live:
        for idx in list(live):
            try:
                next(gens[idx])
            except StopIteration as done:
                results[idx] = done.value
                live.remove(idx)
    return results


def _gla_tile(z, g, pin, pmid, pex, state, reverse):
    prep = yield from _gla_prep(z, g, pin, pmid, pex)
    return (yield from _gla_main(z[:, 2 * GLA_KEY:2 * GLA_KEY + GLA_WIDTH], *prep, pin, state, reverse))


def _gla_kernel(zf_ref, zb_ref, gf_ref, gb_ref, pinf_ref, pmidf_ref, pexf_ref,
                pinb_ref, pmidb_ref, pexb_ref, of_ref, ob_ref, sf_ref, sb_ref):
    @pl.when(pl.program_id(1) == 0)
    def _():
        sf_ref[...] = jnp.zeros_like(sf_ref)
        sb_ref[...] = jnp.zeros_like(sb_ref)

    (of, sf), (ob, sb) = _lockstep([
        _gla_tile(zf_ref[0], gf_ref[0], pinf_ref[...], pmidf_ref[...], pexf_ref[...], sf_ref[...], False),
        _gla_tile(zb_ref[0], gb_ref[0], pinb_ref[...], pmidb_ref[...], pexb_ref[...], sb_ref[...], True)])
    of_ref[0] = of.astype(of_ref.dtype)
    ob_ref[0] = ob.astype(ob_ref.dtype)
    sf_ref[...] = sf
    sb_ref[...] = sb


def _gla_masks(T, reverse):
    t = np.arange(T)
    same = (t[:, None] // GLA_SUB) == (t[None, :] // GLA_SUB)
    before = (t[None, :] >= t[:, None]) if reverse else (t[None, :] <= t[:, None])
    pin = same & before
    pex = same & ~before
    in_sub = t[None, :] % GLA_SUB
    first_half = (in_sub >= GLA_SUB // 2) if reverse else (in_sub < GLA_SUB // 2)
    pmid = same & first_half
    return jnp.asarray(pin, BF16), jnp.asarray(pmid, BF16), jnp.asarray(pex, BF16)


def _gla(zgla, glog):
    B, S, _ = zgla.shape
    T = min(GLA_TILE, S)
    n = S // T
    consts = _gla_masks(T, False) + _gla_masks(T, True)
    fwd = lambda b, i: (b, i, 0)
    bwd = lambda b, i: (b, n - 1 - i, 0)
    return pl.pallas_call(
        _gla_kernel,
        grid=(B, n),
        in_specs=[pl.BlockSpec((1, T, ZGLA_WIDTH), fwd), pl.BlockSpec((1, T, ZGLA_WIDTH), bwd),
                  pl.BlockSpec((1, T, GLA_KEY), fwd), pl.BlockSpec((1, T, GLA_KEY), lambda b, i: (b, n - 1 - i, 1))]
        + [_full_spec(c) for c in consts],
        out_specs=[pl.BlockSpec((1, T, GLA_WIDTH), fwd), pl.BlockSpec((1, T, GLA_WIDTH), bwd)],
        out_shape=[jax.ShapeDtypeStruct((B, S, GLA_WIDTH), BF16)] * 2,
        scratch_shapes=[pltpu.VMEM((GLA_KEY, GLA_WIDTH), F32)] * 2,
        compiler_params=pltpu.CompilerParams(
            dimension_semantics=("parallel", "arbitrary"), vmem_limit_bytes=VMEM_LIMIT),
        name="gla",
    )(zgla, zgla, glog, glog, *consts)


def _silu(x):
    h = 0.5 * x
    return h + h * jnp.tanh(h)


N_MIX_INPUTS = 13


def _mix_final_kernel(*refs, seq_len):
    ins, (fng_ref, out_ref), scratch = refs[:N_MIX_INPUTS], refs[N_MIX_INPUTS:N_MIX_INPUTS + 2], refs[-3:]
    (y,) = _lockstep([_mix_body(*ins, *scratch, seq_len=seq_len, tile=pl.program_id(1),
                                n_tiles=pl.num_programs(1))])
    out_ref[0] = _rms(y, fng_ref[...])


def _mix_proj_kernel(*refs, seq_len):
    ins, scratch = refs[:N_MIX_INPUTS], refs[-3:]
    proj_ops = refs[N_MIX_INPUTS:N_MIX_INPUTS + N_PROJ_OPERANDS]
    out_ref = refs[N_MIX_INPUTS + N_PROJ_OPERANDS]
    proj_outs = refs[N_MIX_INPUTS + N_PROJ_OPERANDS + 1:-3]
    (y,) = _lockstep([_mix_body(*ins, *scratch, seq_len=seq_len, tile=pl.program_id(1),
                                n_tiles=pl.num_programs(1))])
    out_ref[0] = y
    _lockstep([_proj_body(_proj_norm(y, proj_ops[0]), *proj_ops, *proj_outs)])


def _mix_body(x_ref, omla_ref, zgate_ref, u_ref, uprev_ref, unext_ref, gf_ref, gb_ref, wpool_ref, pscale_ref,
              gng_ref, hsum_ref, wout_ref, ubuf, lvl_a, lvl_b, *, seq_len, tile, n_tiles):
    tm = x_ref.shape[1]
    i = tile
    zg = zgate_ref[0].astype(F32)
    gate_mla = zg[:, 0:MLA_WIDTH]
    gate_pool = zg[:, MLA_WIDTH:MLA_WIDTH + POOL_WIDTH]
    gate_gla = zg[:, MLA_WIDTH + POOL_WIDTH:]
    u = u_ref[0]

    H = POOL_HALO
    lo, hi = H, tm + 3 * H
    pad = jnp.zeros((H, POOL_WIDTH), F32)
    for buf in (ubuf, lvl_a, lvl_b):
        buf[0:H] = pad[:, 0:buf.shape[1]]
        buf[hi:] = pad[:, 0:buf.shape[1]]
    ubuf[lo:2 * H] = jnp.where(i > 0, uprev_ref[0], 0.0)
    ubuf[2 * H:2 * H + tm] = u
    ubuf[2 * H + tm:hi] = jnp.where(i < n_tiles - 1, unext_ref[0], 0.0)

    def level(src, cols, before, after):
        return src[lo - before:hi - before, cols] + src[lo + after:hi + after, cols]

    tile_rows = slice(H, H + tm)
    all_cols = slice(0, 128)
    p2 = level(ubuf, slice(0, 128), 1, 0)
    lvl_a[lo:hi] = p2
    p4 = level(lvl_a, all_cols, 1, 1)
    sums = [(p2[tile_rows], p4[tile_rows])]
    yield
    p2 = level(ubuf, slice(128, 256), 1, 0)
    lvl_a[lo:hi] = p2
    lvl_b[lo:hi] = level(lvl_a, all_cols, 1, 1)
    p8 = level(lvl_b, all_cols, 2, 2)
    lvl_a[lo:hi] = p8
    p16 = level(lvl_a, all_cols, 4, 4)
    sums.append((p8[tile_rows], p16[tile_rows]))
    yield

    pos = i * tm + lax.broadcasted_iota(jnp.int32, (tm, 128), 0)
    lane = lax.broadcasted_iota(jnp.int32, (1, 128), 1)
    cnt = lambda w: (jnp.minimum(pos + w // 2, seq_len) - jnp.maximum(pos - w // 2, 0)).astype(F32)
    pooled = [jnp.where(lane < POOL_GROUP_DIM, small / cnt(ws), big / cnt(wb))
              for (small, big), (ws, wb) in zip(sums, ((2, 4), (8, 16)))]
    pooled = jnp.concatenate(pooled, axis=1) - u
    o_pool = _dot(pooled.astype(BF16), wpool_ref[...]) * pscale_ref[...]
    yield

    og = gf_ref[0].astype(F32) + gb_ref[0].astype(F32)
    sq = og * og
    sq_hi = sq.astype(BF16)
    sq_lo = (sq - sq_hi.astype(F32)).astype(BF16)
    ms = (_dot(sq_hi, hsum_ref[...]) + _dot(sq_lo, hsum_ref[...])) * (1.0 / GLA_DV)
    yield
    o_gla = og * lax.rsqrt(ms + NORM_EPS) * gng_ref[...]

    m_mla = (omla_ref[0].astype(F32) * _silu(gate_mla)).astype(BF16)
    y = x_ref[0] + _dot(m_mla, wout_ref[0:MLA_WIDTH])
    yield
    m_pool = (o_pool * _silu(gate_pool)).astype(BF16)
    y = y + _dot(m_pool, wout_ref[MLA_WIDTH:MLA_WIDTH + POOL_WIDTH])
    yield
    m_gla = (o_gla * _silu(gate_gla)).astype(BF16)
    y = y + _dot(m_gla, wout_ref[MLA_WIDTH + POOL_WIDTH:])
    yield
    return y


def _mix(x, omla, zgate, upool, gf, gb, lw, fng=None, next_lw=None, tab=None):
    B, S, _ = x.shape
    tm = min(ROW_TILE, S)
    n = S // tm
    hb = tm // POOL_HALO
    nhalo = S // POOL_HALO
    tile = lambda w: pl.BlockSpec((1, tm, w), lambda b, i: (b, i, 0))
    weights = (lw["wpool"], lw["pool_scale"], lw["gla_norm_g"], lw["hsum"], lw["wout"])
    inputs = (x, omla, zgate, upool, upool, upool, gf, gb) + weights
    assert len(inputs) == N_MIX_INPUTS
    in_specs = [
        tile(D_MODEL), tile(MLA_WIDTH), tile(ZGATE_WIDTH), tile(POOL_WIDTH),
        pl.BlockSpec((1, POOL_HALO, POOL_WIDTH), lambda b, i: (b, jnp.maximum(i * hb - 1, 0), 0)),
        pl.BlockSpec((1, POOL_HALO, POOL_WIDTH), lambda b, i: (b, jnp.minimum((i + 1) * hb, nhalo - 1), 0)),
        tile(GLA_WIDTH), tile(GLA_WIDTH),
    ] + [_full_spec(w) for w in weights]
    out_specs = [tile(D_MODEL)]
    out_shape = [jax.ShapeDtypeStruct((B, S, D_MODEL), F32)]
    scratch = [pltpu.VMEM((tm + 4 * POOL_HALO, POOL_WIDTH), F32),
               pltpu.VMEM((tm + 4 * POOL_HALO, 128), F32),
               pltpu.VMEM((tm + 4 * POOL_HALO, 128), F32)]
    if next_lw is None:
        body, name = functools.partial(_mix_final_kernel, seq_len=S), "mix_final"
        inputs += (fng,)
        in_specs.append(_full_spec(fng))
    else:
        body, name = functools.partial(_mix_proj_kernel, seq_len=S), "mix_proj"
        operands, op_specs = _proj_operands(next_lw, tab, tm)
        inputs += operands
        in_specs += op_specs
        proj_specs, proj_shapes = _proj_outputs(B, S, tm)
        out_specs += proj_specs
        out_shape += proj_shapes
    return pl.pallas_call(
        body,
        grid=(B, n),
        in_specs=in_specs,
        out_specs=out_specs,
        out_shape=out_shape,
        scratch_shapes=scratch,
        compiler_params=pltpu.CompilerParams(
            dimension_semantics=("parallel", "parallel"), vmem_limit_bytes=VMEM_LIMIT),
        name=name,
    )(*inputs)


def _rope_table(seq):
    inv_freq = 1.0 / (ROPE_BASE ** (jnp.arange(0, MLA_ROPE, 2, dtype=F32) / MLA_ROPE))
    ang = jnp.arange(seq, dtype=F32)[:, None] * inv_freq[None, :]
    cos, sin = jnp.cos(ang), jnp.sin(ang)
    return jnp.concatenate([cos, cos, -sin, sin], axis=1)


def _swap_halves(w):
    half = w.shape[-1] // 2
    return jnp.concatenate([w[..., half:], w[..., :half]], axis=-1)


def _layer_weights(l, norm_g, w_in, q_norm_g, w_uq, kv_norm_g, w_ukv, w_pool, pool_scale,
                   gk_up_fwd, gk_bias_fwd, gk_up_bwd, gk_bias_bwd, gla_norm_g, w_out):
    cols = [w_in[l][:, IN_OFFS[j]:IN_OFFS[j + 1]] for j in range(len(IN_SIZES))]
    (c_q, c_kv, k_rope, gate_mla, u_pool, gate_pool, q_gla, k_gla, v_gla, lr_f, lr_b, gate_gla) = cols
    lr_pad = jnp.zeros((D_MODEL, GLA_LR_PAD - 2 * GLA_GATE_RANK), F32)
    wa = jnp.concatenate([c_q, c_kv, k_rope, _swap_halves(k_rope)], axis=1)
    wgate = jnp.concatenate([gate_mla, gate_pool, gate_gla], axis=1)
    wgla = jnp.concatenate([q_gla, k_gla, v_gla, lr_f, lr_b, lr_pad], axis=1)
    uq = w_uq[l].reshape(MLA_Q_LORA, MLA_HEADS, MLA_NOPE + MLA_ROPE)
    uq_rope = uq[..., MLA_NOPE:]
    wuq = jnp.concatenate([uq, _swap_halves(uq_rope)], axis=-1).reshape(MLA_Q_LORA, MLA_HEADS * QK_WIDTH)
    ukv = w_ukv[l].reshape(MLA_KV_LORA, MLA_HEADS, MLA_NOPE + MLA_VDIM)
    wuk = ukv[..., :MLA_NOPE].reshape(MLA_KV_LORA, MLA_HEADS * MLA_NOPE)
    wuvt = ukv[..., MLA_NOPE:].reshape(MLA_KV_LORA, MLA_HEADS * MLA_VDIM).T
    wpool = jnp.zeros((POOL_WIDTH, POOL_WIDTH), F32)
    for gi in range(len(POOL_WINDOWS)):
        sl = slice(gi * POOL_GROUP_DIM, (gi + 1) * POOL_GROUP_DIM)
        wpool = wpool.at[sl, sl].set(w_pool[l, gi])
    gup = jnp.zeros((GLA_LR_PAD, 2 * GLA_KEY), F32)
    gup = gup.at[:GLA_GATE_RANK, :GLA_KEY].set(gk_up_fwd[l])
    gup = gup.at[GLA_GATE_RANK:2 * GLA_GATE_RANK, GLA_KEY:].set(gk_up_bwd[l])
    gbias = jnp.concatenate([gk_bias_fwd[l], gk_bias_bwd[l]])[None, :]
    head = np.arange(GLA_WIDTH) // GLA_DV
    hsum = jnp.asarray(head[:, None] == head[None, :], BF16)
    return {
        "norm_g": norm_g[l][None, :],
        "wa": wa.astype(BF16), "wgate": wgate.astype(BF16), "wpu": u_pool.astype(BF16), "wgla": wgla.astype(BF16),
        "q_norm_g": q_norm_g[l][None, :], "wuqt": wuq.T.astype(BF16),
        "kv_norm_g": kv_norm_g[l][None, :], "wuk": wuk.astype(BF16), "wuvt": wuvt.astype(BF16),
        "wpool": wpool.astype(BF16), "pool_scale": pool_scale[l][None, :],
        "gup": gup.astype(BF16), "gbias": gbias,
        "gla_norm_g": jnp.tile(gla_norm_g[l], GLA_HEADS)[None, :], "hsum": hsum,
        "wout": w_out[l].astype(BF16),
    }


def _trunk(x, layers, fng):
    tab = _rope_table(x.shape[1])
    tab = (tab, tab.T)
    q, k, vt, zgate, upool, zgla, glog = _proj(x, layers[0], tab)
    for l, lw in enumerate(layers):
        omla = _attn(q, k, vt)
        gf, gb = _gla(zgla, glog)
        if l + 1 < len(layers):
            x, q, k, vt, zgate, upool, zgla, glog = _mix(x, omla, zgate, upool, gf, gb, lw,
                                                         next_lw=layers[l + 1], tab=tab)
        else:
            (x,) = _mix(x, omla, zgate, upool, gf, gb, lw, fng=fng)
    return x


def kernel(x_prompt, x_sample, norm_g, w_in, q_norm_g, w_uq, kv_norm_g, w_ukv, w_pool, pool_scale,
           gk_up_fwd, gk_bias_fwd, gk_up_bwd, gk_bias_bwd, gla_norm_g, w_out, final_norm_g):
    depth = w_in.shape[0]
    layers = [_layer_weights(l, norm_g, w_in, q_norm_g, w_uq, kv_norm_g, w_ukv, w_pool, pool_scale,
                             gk_up_fwd, gk_bias_fwd, gk_up_bwd, gk_bias_bwd, gla_norm_g, w_out)
              for l in range(depth)]
    fng = final_norm_g[None, :]
    return (_trunk(x_prompt, layers, fng), _trunk(x_sample, layers, fng))
```

```python
import functools
import math

import jax
import jax.numpy as jnp
import numpy as np
from jax import lax
from jax.experimental import pallas as pl
from jax.experimental.pallas import tpu as pltpu

F32 = jnp.float32
BF16 = jnp.bfloat16

D_MODEL = 1024
NORM_EPS = 1e-6
MLA_HEADS = 4
MLA_NOPE = 128
MLA_ROPE = 64
MLA_VDIM = 128
MLA_Q_LORA = 256
MLA_KV_LORA = 128
MLA_WIDTH = MLA_HEADS * MLA_VDIM
MLA_SCALE = (MLA_NOPE + MLA_ROPE) ** -0.5
ROPE_BASE = 10000.0
QK_WIDTH = 256
POOL_WINDOWS = (2, 4, 8, 16)
POOL_WIDTH = 256
POOL_GROUP_DIM = 64
POOL_HALO = 8
GLA_HEADS = 4
GLA_WIDTH = 256
GLA_KEY = 128
GLA_DK = 32
GLA_DV = 64
GLA_GATE_RANK = 16
GLA_GATE_NORM = 16.0
GLA_SUB = 32
GLA_TILE = 256
GLA_HALF = 128
GLA_TILES_PER_STEP = 4

MIX_WIDTH = MLA_WIDTH + POOL_WIDTH + GLA_WIDTH
IN_SIZES = (MLA_Q_LORA, MLA_KV_LORA, MLA_ROPE, MLA_WIDTH, POOL_WIDTH, POOL_WIDTH,
            GLA_KEY, GLA_KEY, GLA_WIDTH, GLA_GATE_RANK, GLA_GATE_RANK, GLA_WIDTH)
IN_OFFS = tuple(int(v) for v in np.cumsum((0,) + IN_SIZES))

ZA_WIDTH = MLA_Q_LORA + MLA_KV_LORA + 2 * MLA_ROPE
ZGATE_WIDTH = MLA_WIDTH + POOL_WIDTH + GLA_WIDTH
ZGLA_WIDTH = 2 * GLA_KEY + GLA_WIDTH
GLA_LR_PAD = 128

ROW_TILE = 512
ATTN_TQ = 512
ATTN_TK = 1024
ATTN_TILES_PER_STEP = 4
VMEM_LIMIT = 56 * 1024 * 1024


def _rms(x, g):
    return x * lax.rsqrt(jnp.mean(x * x, axis=-1, keepdims=True) + NORM_EPS) * g


def _dot(a, b):
    return jnp.dot(a, b, preferred_element_type=F32)


def _dot_nt(a, b):
    return lax.dot_general(a, b, (((1,), (1,)), ((), ())), preferred_element_type=F32)


N_PROJ_OPERANDS = 14
N_PROJ_OUTS = 7


def _proj_kernel(x_ref, *refs):
    _lockstep([_proj_body(_proj_norm(x_ref[0], refs[0]), *refs)])


def _proj_norm(x, ng_ref):
    return _rms(x, ng_ref[...]).astype(BF16)


def _proj_body(h, ng_ref, wa_ref, wgate_ref, wpu_ref, wgla_ref, qg_ref, wuqt_ref, kvg_ref, wuk_ref,
               wuvt_ref, gup_ref, gbias_ref, tab_ref, tabt_ref,
               qt_ref, k_ref, vt_ref, zgate_ref, upool_ref, zgla_ref, glog_ref):
    za = _dot(h, wa_ref[...])
    yield
    zgate_ref[0] = _dot(h, wgate_ref[...]).astype(BF16)
    yield
    upool_ref[0] = _dot(h, wpu_ref[...])
    yield
    zgla = _dot(h, wgla_ref[...])
    yield
    zgla_ref[0] = zgla[:, :ZGLA_WIDTH].astype(BF16)
    zg = _dot(zgla[:, ZGLA_WIDTH:].astype(BF16), gup_ref[...]) + gbias_ref[...]
    yield
    glog_ref[0] = (jnp.minimum(zg, 0.0) - jnp.log1p(jnp.exp(-jnp.abs(zg)))) * (1.0 / GLA_GATE_NORM)
    tab = tab_ref[...]
    tabt = tabt_ref[...]
    qscale = MLA_SCALE * math.log2(math.e)
    cqn = _rms(za[:, :MLA_Q_LORA], qg_ref[...]).astype(BF16)
    qall = _dot_nt(wuqt_ref[...], cqn)
    yield
    zeros = jnp.zeros((QK_WIDTH - MLA_NOPE - MLA_ROPE, qall.shape[1]), BF16)
    for hh in range(MLA_HEADS):
        base = hh * QK_WIDTH
        qn = qall[base:base + MLA_NOPE] * qscale
        t = qall[base + MLA_NOPE:base + QK_WIDTH] * tabt
        r = (t[:MLA_ROPE] + t[MLA_ROPE:]) * qscale
        qt_ref[0, hh, 0:MLA_NOPE] = qn.astype(BF16)
        qt_ref[0, hh, MLA_NOPE:MLA_NOPE + MLA_ROPE] = r.astype(BF16)
        qt_ref[0, hh, MLA_NOPE + MLA_ROPE:] = zeros
    ckvn = _rms(za[:, MLA_Q_LORA:MLA_Q_LORA + MLA_KV_LORA], kvg_ref[...]).astype(BF16)
    kn = _dot(ckvn, wuk_ref[...])
    yield
    t = za[:, MLA_Q_LORA + MLA_KV_LORA:] * tab
    kr = t + pltpu.roll(t, MLA_ROPE, 1)
    lane = lax.broadcasted_iota(jnp.int32, kr.shape, 1)
    kr = jnp.where(lane < MLA_ROPE, kr, 0.0).astype(BF16)
    for hh in range(MLA_HEADS):
        k_ref[0, hh, :, 0:MLA_NOPE] = kn[:, hh * MLA_NOPE:(hh + 1) * MLA_NOPE].astype(BF16)
        k_ref[0, hh, :, MLA_NOPE:QK_WIDTH] = kr
    vt = _dot_nt(wuvt_ref[...], ckvn)
    yield
    for hh in range(MLA_HEADS):
        vt_ref[0, hh] = vt[hh * MLA_VDIM:(hh + 1) * MLA_VDIM].astype(BF16)


def _full_spec(a):
    return pl.BlockSpec(a.shape, lambda *g: (0,) * a.ndim)


def _proj_operands(lw, tab, tm):
    weights = (lw["norm_g"], lw["wa"], lw["wgate"], lw["wpu"], lw["wgla"], lw["q_norm_g"], lw["wuqt"],
               lw["kv_norm_g"], lw["wuk"], lw["wuvt"], lw["gup"], lw["gbias"])
    specs = [_full_spec(w) for w in weights] + [pl.BlockSpec((tm, 128), lambda b, i: (i, 0)),
                                                pl.BlockSpec((128, tm), lambda b, i: (0, i))]
    operands = weights + tuple(tab)
    assert len(operands) == N_PROJ_OPERANDS
    return operands, specs


def _proj_outputs(B, S, tm):
    tok = lambda w: pl.BlockSpec((1, tm, w), lambda b, i: (b, i, 0))
    specs = [
        pl.BlockSpec((1, MLA_HEADS, QK_WIDTH, tm), lambda b, i: (b, 0, 0, i)),
        pl.BlockSpec((1, MLA_HEADS, tm, QK_WIDTH), lambda b, i: (b, 0, i, 0)),
        pl.BlockSpec((1, MLA_HEADS, MLA_VDIM, tm), lambda b, i: (b, 0, 0, i)),
        tok(ZGATE_WIDTH), tok(POOL_WIDTH), tok(ZGLA_WIDTH), tok(2 * GLA_KEY),
    ]
    shapes = [
        jax.ShapeDtypeStruct((B, MLA_HEADS, QK_WIDTH, S), BF16),
        jax.ShapeDtypeStruct((B, MLA_HEADS, S, QK_WIDTH), BF16),
        jax.ShapeDtypeStruct((B, MLA_HEADS, MLA_VDIM, S), BF16),
        jax.ShapeDtypeStruct((B, S, ZGATE_WIDTH), BF16),
        jax.ShapeDtypeStruct((B, S, POOL_WIDTH), F32),
        jax.ShapeDtypeStruct((B, S, ZGLA_WIDTH), BF16),
        jax.ShapeDtypeStruct((B, S, 2 * GLA_KEY), F32),
    ]
    assert len(specs) == N_PROJ_OUTS
    return specs, shapes


def _proj(x, lw, tab):
    B, S, _ = x.shape
    tm = min(ROW_TILE, S)
    operands, op_specs = _proj_operands(lw, tab, tm)
    out_specs, out_shape = _proj_outputs(B, S, tm)
    return pl.pallas_call(
        _proj_kernel,
        grid=(B, S // tm),
        in_specs=[pl.BlockSpec((1, tm, D_MODEL), lambda b, i: (b, i, 0))] + op_specs,
        out_specs=out_specs,
        out_shape=out_shape,
        compiler_params=pltpu.CompilerParams(
            dimension_semantics=("parallel", "parallel"), vmem_limit_bytes=VMEM_LIMIT),
        name="proj",
    )(x, *operands)


def _attn_kernel(qt_ref, qn_ref, k_ref, vt_ref, o_ref, s2_ref, mxa_ref, acc_ref, *, tk, nk):
    tq = qn_ref.shape[3]
    n_tiles = qt_ref.shape[3] // tq
    zero = jnp.minimum(pl.program_id(2), 0)
    sa_ref = s2_ref.at[zero]
    sb_ref = s2_ref.at[zero + 1]

    def scores(i, q, s_ref):
        q_ref, tile = q
        start = i * tk
        s = _dot(k_ref[0, 0, pl.ds(start, tk), :], q_ref[0, 0, :, tile * tq:(tile + 1) * tq])
        s_ref[:, 0:tq] = s
        return jnp.max(s, axis=0, keepdims=True)

    def update(i, s_ref, mx, m, l):
        start = i * tk
        m_new = jnp.maximum(m, mx)
        alpha = jnp.exp2(m - m_new)
        p = jnp.exp2(s_ref[:, 0:tq] - m_new)
        l = alpha * l + jnp.sum(p, axis=0, keepdims=True)
        acc_ref[...] = alpha * acc_ref[...] + _dot(vt_ref[0, 0, :, pl.ds(start, tk)], p.astype(BF16))
        return m_new, l

    @pl.when(pl.program_id(2) == 0)
    def _():
        mxa_ref[...] = scores(0, (qt_ref, 0), sa_ref)

    mx_a = mxa_ref[...]
    for tile in range(n_tiles):
        q = (qt_ref, tile)
        q_next = (qt_ref, tile + 1) if tile + 1 < n_tiles else (qn_ref, 0)
        m = jnp.full((1, tq), -jnp.inf, F32)
        l = jnp.zeros((1, tq), F32)
        acc_ref[...] = jnp.zeros_like(acc_ref)
        for i in range(0, nk, 2):
            mx_b = scores(i + 1, q, sb_ref)
            m, l = update(i, sa_ref, mx_a, m, l)
            mx_a = scores(i + 2, q, sa_ref) if i + 2 < nk else scores(0, q_next, sa_ref)
            m, l = update(i + 1, sb_ref, mx_b, m, l)
        o_ref[0, tile * tq:(tile + 1) * tq] = (acc_ref[...] / l).T.astype(o_ref.dtype)
    mxa_ref[...] = mx_a


def _attn(qt, k, vt):
    B, H, S, _ = k.shape
    tq = min(ATTN_TQ, S)
    tk = min(ATTN_TK, S // 2)
    nk = S // tk
    nq = S // tq
    per_step = min(ATTN_TILES_PER_STEP, nq)
    steps = nq // per_step
    assert nk % 2 == 0 and nq % per_step == 0
    return pl.pallas_call(
        functools.partial(_attn_kernel, tk=tk, nk=nk),
        grid=(B, H, steps),
        in_specs=[
            pl.BlockSpec((1, 1, QK_WIDTH, per_step * tq), lambda b, h, i: (b, h, 0, i)),
            pl.BlockSpec((1, 1, QK_WIDTH, tq), lambda b, h, i: (b, h, 0, jnp.minimum((i + 1) * per_step, nq - 1))),
            pl.BlockSpec((1, 1, S, QK_WIDTH), lambda b, h, i: (b, h, 0, 0)),
            pl.BlockSpec((1, 1, MLA_VDIM, S), lambda b, h, i: (b, h, 0, 0)),
        ],
        out_specs=pl.BlockSpec((1, per_step * tq, MLA_VDIM), lambda b, h, i: (b, i, h)),
        out_shape=jax.ShapeDtypeStruct((B, S, MLA_WIDTH), BF16),
        scratch_shapes=[pltpu.VMEM((2, tk, tq + 128), F32),
                        pltpu.VMEM((1, tq), F32), pltpu.VMEM((MLA_VDIM, tq), F32)],
        compiler_params=pltpu.CompilerParams(
            dimension_semantics=("parallel", "parallel", "arbitrary"), vmem_limit_bytes=VMEM_LIMIT),
        name="attn",
    )(qt, qt, k, vt)


def _split2(x):
    hi = x.astype(BF16)
    lo = (x - hi.astype(F32)).astype(BF16)
    return jnp.concatenate([hi, lo], axis=1)


def _sum2(y, w):
    return y[:, :w] + y[:, w:]


def _gla_prep(z, g, pin, pmid, pex):
    T = z.shape[0]
    nsub = T // GLA_SUB
    q = z[:, 0:GLA_KEY].astype(F32) * (GLA_DK ** -0.5)
    k = z[:, GLA_KEY:2 * GLA_KEY].astype(F32)
    g2 = _split2(g)
    b_in = _sum2(_dot(pin, g2), GLA_KEY)
    yield
    b_mid = _sum2(_dot(pmid, g2), GLA_KEY)
    yield
    b_ex = _sum2(_dot(pex, g2), GLA_KEY)
    yield
    qt = (q * jnp.exp(b_in)).astype(BF16)
    qa = (q * jnp.exp(b_in - b_mid)).astype(BF16)
    kt = (k * jnp.exp(b_mid - b_in)).astype(BF16)
    kp = k * jnp.exp(b_ex)
    dd = jnp.exp(b_in + b_ex)
    kpt = kp.T
    sub_of_lane = lax.broadcasted_iota(jnp.int32, (1, T), 1) // GLA_SUB
    stack = jnp.concatenate(
        [jnp.where(sub_of_lane == c, kpt, 0.0).astype(BF16) for c in range(nsub)], axis=0)
    return qt, qa, kt, stack, dd.T


def _gla_local(z, g, pin, pmid, pex):
    qt, qa16, kt, stack, ddt = yield from _gla_prep(z, g, pin, pmid, pex)
    v = z[:, 2 * GLA_KEY:2 * GLA_KEY + GLA_WIDTH]
    T = v.shape[0]
    HT = GLA_HALF
    qhead = lax.broadcasted_iota(jnp.int32, (1, GLA_KEY), 1) // GLA_DK
    vhead = lax.broadcasted_iota(jnp.int32, (1, GLA_WIDTH), 1) // GLA_DV
    half_mask = pin[0:HT, 0:HT] > 0
    pmask = jnp.concatenate([half_mask, half_mask], axis=1)
    a = []
    for hf in range(T // HT):
        rows = slice(hf * HT, (hf + 1) * HT)
        for hp in range(GLA_HEADS // 2):
            kstack = jnp.concatenate([jnp.where(qhead == 2 * hp + e, kt[rows], jnp.zeros_like(kt[rows]))
                                      for e in range(2)], axis=0)
            a.append(jnp.where(pmask, _dot_nt(qa16[rows], kstack), 0.0).astype(BF16))
            yield

    inc = _dot(stack, v)
    yield

    o_halves = []
    for hf in range(T // HT):
        rows = slice(hf * HT, (hf + 1) * HT)
        o_half = jnp.zeros((HT, GLA_WIDTH), F32)
        for hp in range(GLA_HEADS // 2):
            vstack = jnp.concatenate([jnp.where(vhead == 2 * hp + e, v[rows], jnp.zeros_like(v[rows]))
                                      for e in range(2)], axis=0)
            o_half = o_half + _dot(a[hf * (GLA_HEADS // 2) + hp], vstack)
            yield
        o_halves.append(o_half)
    return jnp.concatenate(o_halves, axis=0), qt, ddt, inc


def _gla_scan(local, state, reverse):
    o, qt, ddt, inc = local
    nsub = qt.shape[0] // GLA_SUB
    same_head = ((lax.broadcasted_iota(jnp.int32, (GLA_KEY, 1), 0) // GLA_DK)
                 == (lax.broadcasted_iota(jnp.int32, (1, GLA_WIDTH), 1) // GLA_DV))
    inter = [None] * nsub
    order = range(nsub - 1, -1, -1) if reverse else range(nsub)
    for c in order:
        rows = slice(c * GLA_SUB, (c + 1) * GLA_SUB)
        inter[c] = _dot(qt[rows], state.astype(BF16))
        state = state * ddt[:, c * GLA_SUB:c * GLA_SUB + 1] + jnp.where(
            same_head, inc[c * GLA_KEY:(c + 1) * GLA_KEY], 0.0)
        yield
    return o + jnp.concatenate(inter, axis=0), state


def _lockstep(gens):
    results = [None] * len(gens)
    live = list(range(len(gens)))
    while live:
        for idx in list(live):
            try:
                next(gens[idx])
            except StopIteration as done:
                results[idx] = done.value
                live.remove(idx)
    return results


def _gla_kernel(zf_ref, zb_ref, gf_ref, gb_ref, pinf_ref, pmidf_ref, pexf_ref,
                pinb_ref, pmidb_ref, pexb_ref, of_ref, ob_ref, sf_ref, sb_ref):
    T = GLA_TILE if zf_ref.shape[1] >= GLA_TILE else zf_ref.shape[1]
    rounds = zf_ref.shape[1] // T
    masks_f = (pinf_ref[...], pmidf_ref[...], pexf_ref[...])
    masks_b = (pinb_ref[...], pmidb_ref[...], pexb_ref[...])

    @pl.when(pl.program_id(1) == 0)
    def _():
        sf_ref[...] = jnp.zeros_like(sf_ref)
        sb_ref[...] = jnp.zeros_like(sb_ref)

    def rows_of(r, reverse):
        t = rounds - 1 - r if reverse else r
        return slice(t * T, (t + 1) * T)

    def local_gens(r):
        rf, rb = rows_of(r, False), rows_of(r, True)
        return [_gla_local(zf_ref[0, rf], gf_ref[0, rf], *masks_f),
                _gla_local(zb_ref[0, rb], gb_ref[0, rb], *masks_b)]

    sf, sb = sf_ref[...], sb_ref[...]
    local_f, local_b = _lockstep(local_gens(0))
    for r in range(rounds):
        gens = [_gla_scan(local_f, sf, False), _gla_scan(local_b, sb, True)]
        if r + 1 < rounds:
            gens += local_gens(r + 1)
        results = _lockstep(gens)
        (of, sf), (ob, sb) = results[:2]
        of_ref[0, rows_of(r, False)] = of.astype(of_ref.dtype)
        ob_ref[0, rows_of(r, True)] = ob.astype(ob_ref.dtype)
        if r + 1 < rounds:
            local_f, local_b = results[2:]
    sf_ref[...] = sf
    sb_ref[...] = sb


def _gla_masks(T, reverse):
    t = np.arange(T)
    same = (t[:, None] // GLA_SUB) == (t[None, :] // GLA_SUB)
    before = (t[None, :] >= t[:, None]) if reverse else (t[None, :] <= t[:, None])
    pin = same & before
    pex = same & ~before
    in_sub = t[None, :] % GLA_SUB
    first_half = (in_sub >= GLA_SUB // 2) if reverse else (in_sub < GLA_SUB // 2)
    pmid = same & first_half
    return jnp.asarray(pin, BF16), jnp.asarray(pmid, BF16), jnp.asarray(pex, BF16)


def _gla(zgla, glog):
    B, S, _ = zgla.shape
    T = min(GLA_TILE, S)
    TB = min(GLA_TILES_PER_STEP * T, S)
    n = S // TB
    consts = _gla_masks(T, False) + _gla_masks(T, True)
    fwd = lambda b, i: (b, i, 0)
    bwd = lambda b, i: (b, n - 1 - i, 0)
    return pl.pallas_call(
        _gla_kernel,
        grid=(B, n),
        in_specs=[pl.BlockSpec((1, TB, ZGLA_WIDTH), fwd), pl.BlockSpec((1, TB, ZGLA_WIDTH), bwd),
                  pl.BlockSpec((1, TB, GLA_KEY), fwd), pl.BlockSpec((1, TB, GLA_KEY), lambda b, i: (b, n - 1 - i, 1))]
        + [_full_spec(c) for c in consts],
        out_specs=[pl.BlockSpec((1, TB, GLA_WIDTH), fwd), pl.BlockSpec((1, TB, GLA_WIDTH), bwd)],
        out_shape=[jax.ShapeDtypeStruct((B, S, GLA_WIDTH), BF16)] * 2,
        scratch_shapes=[pltpu.VMEM((GLA_KEY, GLA_WIDTH), F32)] * 2,
        compiler_params=pltpu.CompilerParams(
            dimension_semantics=("parallel", "arbitrary"), vmem_limit_bytes=VMEM_LIMIT),
        name="gla",
    )(zgla, zgla, glog, glog, *consts)


def _silu(x):
    h = 0.5 * x
    return h + h * jnp.tanh(h)


N_MIX_INPUTS = 13


def _mix_final_kernel(*refs, seq_len):
    ins, (fng_ref, out_ref), scratch = refs[:N_MIX_INPUTS], refs[N_MIX_INPUTS:N_MIX_INPUTS + 2], refs[-3:]
    (y,) = _lockstep([_mix_body(*ins, *scratch, seq_len=seq_len, tile=pl.program_id(1),
                                n_tiles=pl.num_programs(1))])
    out_ref[0] = _rms(y, fng_ref[...])


def _mix_proj_kernel(*refs, seq_len):
    ins, scratch = refs[:N_MIX_INPUTS], refs[-3:]
    proj_ops = refs[N_MIX_INPUTS:N_MIX_INPUTS + N_PROJ_OPERANDS]
    out_ref = refs[N_MIX_INPUTS + N_PROJ_OPERANDS]
    proj_outs = refs[N_MIX_INPUTS + N_PROJ_OPERANDS + 1:-3]
    (y,) = _lockstep([_mix_body(*ins, *scratch, seq_len=seq_len, tile=pl.program_id(1),
                                n_tiles=pl.num_programs(1))])
    out_ref[0] = y
    _lockstep([_proj_body(_proj_norm(y, proj_ops[0]), *proj_ops, *proj_outs)])


def _mix_body(x_ref, omla_ref, zgate_ref, u_ref, uprev_ref, unext_ref, gf_ref, gb_ref, wpool_ref, pscale_ref,
              gng_ref, hsum_ref, wout_ref, ubuf, lvl_a, lvl_b, *, seq_len, tile, n_tiles):
    tm = x_ref.shape[1]
    i = tile
    zg = zgate_ref[0].astype(F32)
    gate_mla = zg[:, 0:MLA_WIDTH]
    gate_pool = zg[:, MLA_WIDTH:MLA_WIDTH + POOL_WIDTH]
    gate_gla = zg[:, MLA_WIDTH + POOL_WIDTH:]
    u = u_ref[0]

    H = POOL_HALO
    lo, hi = H, tm + 3 * H
    pad = jnp.zeros((H, POOL_WIDTH), F32)
    for buf in (ubuf, lvl_a, lvl_b):
        buf[0:H] = pad[:, 0:buf.shape[1]]
        buf[hi:] = pad[:, 0:buf.shape[1]]
    ubuf[lo:2 * H] = jnp.where(i > 0, uprev_ref[0], 0.0)
    ubuf[2 * H:2 * H + tm] = u
    ubuf[2 * H + tm:hi] = jnp.where(i < n_tiles - 1, unext_ref[0], 0.0)

    def level(src, cols, before, after):
        return src[lo - before:hi - before, cols] + src[lo + after:hi + after, cols]

    tile_rows = slice(H, H + tm)
    all_cols = slice(0, 128)
    p2 = level(ubuf, slice(0, 128), 1, 0)
    lvl_a[lo:hi] = p2
    p4 = level(lvl_a, all_cols, 1, 1)
    sums = [(p2[tile_rows], p4[tile_rows])]
    yield
    p2 = level(ubuf, slice(128, 256), 1, 0)
    lvl_a[lo:hi] = p2
    lvl_b[lo:hi] = level(lvl_a, all_cols, 1, 1)
    p8 = level(lvl_b, all_cols, 2, 2)
    lvl_a[lo:hi] = p8
    p16 = level(lvl_a, all_cols, 4, 4)
    sums.append((p8[tile_rows], p16[tile_rows]))
    yield

    pos = i * tm + lax.broadcasted_iota(jnp.int32, (tm, 128), 0)
    lane = lax.broadcasted_iota(jnp.int32, (1, 128), 1)
    cnt = lambda w: (jnp.minimum(pos + w // 2, seq_len) - jnp.maximum(pos - w // 2, 0)).astype(F32)
    pooled = [jnp.where(lane < POOL_GROUP_DIM, small / cnt(ws), big / cnt(wb))
              for (small, big), (ws, wb) in zip(sums, ((2, 4), (8, 16)))]
    pooled = jnp.concatenate(pooled, axis=1) - u
    o_pool = _dot(pooled.astype(BF16), wpool_ref[...]) * pscale_ref[...]
    yield

    og = gf_ref[0].astype(F32) + gb_ref[0].astype(F32)
    sq = og * og
    sq_hi = sq.astype(BF16)
    sq_lo = (sq - sq_hi.astype(F32)).astype(BF16)
    ms = (_dot(sq_hi, hsum_ref[...]) + _dot(sq_lo, hsum_ref[...])) * (1.0 / GLA_DV)
    yield
    o_gla = og * lax.rsqrt(ms + NORM_EPS) * gng_ref[...]

    m_mla = (omla_ref[0].astype(F32) * _silu(gate_mla)).astype(BF16)
    y = x_ref[0] + _dot(m_mla, wout_ref[0:MLA_WIDTH])
    yield
    m_pool = (o_pool * _silu(gate_pool)).astype(BF16)
    y = y + _dot(m_pool, wout_ref[MLA_WIDTH:MLA_WIDTH + POOL_WIDTH])
    yield
    m_gla = (o_gla * _silu(gate_gla)).astype(BF16)
    y = y + _dot(m_gla, wout_ref[MLA_WIDTH + POOL_WIDTH:])
    yield
    return y


def _mix(x, omla, zgate, upool, gf, gb, lw, fng=None, next_lw=None, tab=None):
    B, S, _ = x.shape
    tm = min(ROW_TILE, S)
    n = S // tm
    hb = tm // POOL_HALO
    nhalo = S // POOL_HALO
    tile = lambda w: pl.BlockSpec((1, tm, w), lambda b, i: (b, i, 0))
    weights = (lw["wpool"], lw["pool_scale"], lw["gla_norm_g"], lw["hsum"], lw["wout"])
    inputs = (x, omla, zgate, upool, upool, upool, gf, gb) + weights
    assert len(inputs) == N_MIX_INPUTS
    in_specs = [
        tile(D_MODEL), tile(MLA_WIDTH), tile(ZGATE_WIDTH), tile(POOL_WIDTH),
        pl.BlockSpec((1, POOL_HALO, POOL_WIDTH), lambda b, i: (b, jnp.maximum(i * hb - 1, 0), 0)),
        pl.BlockSpec((1, POOL_HALO, POOL_WIDTH), lambda b, i: (b, jnp.minimum((i + 1) * hb, nhalo - 1), 0)),
        tile(GLA_WIDTH), tile(GLA_WIDTH),
    ] + [_full_spec(w) for w in weights]
    out_specs = [tile(D_MODEL)]
    out_shape = [jax.ShapeDtypeStruct((B, S, D_MODEL), F32)]
    scratch = [pltpu.VMEM((tm + 4 * POOL_HALO, POOL_WIDTH), F32),
               pltpu.VMEM((tm + 4 * POOL_HALO, 128), F32),
               pltpu.VMEM((tm + 4 * POOL_HALO, 128), F32)]
    if next_lw is None:
        body, name = functools.partial(_mix_final_kernel, seq_len=S), "mix_final"
        inputs += (fng,)
        in_specs.append(_full_spec(fng))
    else:
        body, name = functools.partial(_mix_proj_kernel, seq_len=S), "mix_proj"
        operands, op_specs = _proj_operands(next_lw, tab, tm)
        inputs += operands
        in_specs += op_specs
        proj_specs, proj_shapes = _proj_outputs(B, S, tm)
        out_specs += proj_specs
        out_shape += proj_shapes
    return pl.pallas_call(
        body,
        grid=(B, n),
        in_specs=in_specs,
        out_specs=out_specs,
        out_shape=out_shape,
        scratch_shapes=scratch,
        compiler_params=pltpu.CompilerParams(
            dimension_semantics=("parallel", "parallel"), vmem_limit_bytes=VMEM_LIMIT),
        name=name,
    )(*inputs)


def _rope_table(seq):
    inv_freq = 1.0 / (ROPE_BASE ** (jnp.arange(0, MLA_ROPE, 2, dtype=F32) / MLA_ROPE))
    ang = jnp.arange(seq, dtype=F32)[:, None] * inv_freq[None, :]
    cos, sin = jnp.cos(ang), jnp.sin(ang)
    return jnp.concatenate([cos, cos, -sin, sin], axis=1)


def _swap_halves(w):
    half = w.shape[-1] // 2
    return jnp.concatenate([w[..., half:], w[..., :half]], axis=-1)


def _layer_weights(l, norm_g, w_in, q_norm_g, w_uq, kv_norm_g, w_ukv, w_pool, pool_scale,
                   gk_up_fwd, gk_bias_fwd, gk_up_bwd, gk_bias_bwd, gla_norm_g, w_out):
    cols = [w_in[l][:, IN_OFFS[j]:IN_OFFS[j + 1]] for j in range(len(IN_SIZES))]
    (c_q, c_kv, k_rope, gate_mla, u_pool, gate_pool, q_gla, k_gla, v_gla, lr_f, lr_b, gate_gla) = cols
    lr_pad = jnp.zeros((D_MODEL, GLA_LR_PAD - 2 * GLA_GATE_RANK), F32)
    wa = jnp.concatenate([c_q, c_kv, k_rope, _swap_halves(k_rope)], axis=1)
    wgate = jnp.concatenate([gate_mla, gate_pool, gate_gla], axis=1)
    wgla = jnp.concatenate([q_gla, k_gla, v_gla, lr_f, lr_b, lr_pad], axis=1)
    uq = w_uq[l].reshape(MLA_Q_LORA, MLA_HEADS, MLA_NOPE + MLA_ROPE)
    uq_rope = uq[..., MLA_NOPE:]
    wuq = jnp.concatenate([uq, _swap_halves(uq_rope)], axis=-1).reshape(MLA_Q_LORA, MLA_HEADS * QK_WIDTH)
    ukv = w_ukv[l].reshape(MLA_KV_LORA, MLA_HEADS, MLA_NOPE + MLA_VDIM)
    wuk = ukv[..., :MLA_NOPE].reshape(MLA_KV_LORA, MLA_HEADS * MLA_NOPE)
    wuvt = ukv[..., MLA_NOPE:].reshape(MLA_KV_LORA, MLA_HEADS * MLA_VDIM).T
    wpool = jnp.zeros((POOL_WIDTH, POOL_WIDTH), F32)
    for gi in range(len(POOL_WINDOWS)):
        sl = slice(gi * POOL_GROUP_DIM, (gi + 1) * POOL_GROUP_DIM)
        wpool = wpool.at[sl, sl].set(w_pool[l, gi])
    gup = jnp.zeros((GLA_LR_PAD, 2 * GLA_KEY), F32)
    gup = gup.at[:GLA_GATE_RANK, :GLA_KEY].set(gk_up_fwd[l])
    gup = gup.at[GLA_GATE_RANK:2 * GLA_GATE_RANK, GLA_KEY:].set(gk_up_bwd[l])
    gbias = jnp.concatenate([gk_bias_fwd[l], gk_bias_bwd[l]])[None, :]
    head = np.arange(GLA_WIDTH) // GLA_DV
    hsum = jnp.asarray(head[:, None] == head[None, :], BF16)
    return {
        "norm_g": norm_g[l][None, :],
        "wa": wa.astype(BF16), "wgate": wgate.astype(BF16), "wpu": u_pool.astype(BF16), "wgla": wgla.astype(BF16),
        "q_norm_g": q_norm_g[l][None, :], "wuqt": wuq.T.astype(BF16),
        "kv_norm_g": kv_norm_g[l][None, :], "wuk": wuk.astype(BF16), "wuvt": wuvt.astype(BF16),
        "wpool": wpool.astype(BF16), "pool_scale": pool_scale[l][None, :],
        "gup": gup.astype(BF16), "gbias": gbias,
        "gla_norm_g": jnp.tile(gla_norm_g[l], GLA_HEADS)[None, :], "hsum": hsum,
        "wout": w_out[l].astype(BF16),
    }


def _trunk(x, layers, fng):
    tab = _rope_table(x.shape[1])
    tab = (tab, tab.T)
    q, k, vt, zgate, upool, zgla, glog = _proj(x, layers[0], tab)
    for l, lw in enumerate(layers):
        omla = _attn(q, k, vt)
        gf, gb = _gla(zgla, glog)
        if l + 1 < len(layers):
            x, q, k, vt, zgate, upool, zgla, glog = _mix(x, omla, zgate, upool, gf, gb, lw,
                                                         next_lw=layers[l + 1], tab=tab)
        else:
            (x,) = _mix(x, omla, zgate, upool, gf, gb, lw, fng=fng)
    return x


def kernel(x_prompt, x_sample, norm_g, w_in, q_norm_g, w_uq, kv_norm_g, w_ukv, w_pool, pool_scale,
           gk_up_fwd, gk_bias_fwd, gk_up_bwd, gk_bias_bwd, gla_norm_g, w_out, final_norm_g):
    depth = w_in.shape[0]
    layers = [_layer_weights(l, norm_g, w_in, q_norm_g, w_uq, kv_norm_g, w_ukv, w_pool, pool_scale,
                             gk_up_fwd, gk_bias_fwd, gk_up_bwd, gk_bias_bwd, gla_norm_g, w_out)
              for l in range(depth)]
    fng = final_norm_g[None, :]
    return (_trunk(x_prompt, layers, fng), _trunk(x_sample, layers, fng))
```

```python
import functools
import math

import jax
import jax.numpy as jnp
import numpy as np
from jax import lax
from jax.experimental import pallas as pl
from jax.experimental.pallas import tpu as pltpu

F32 = jnp.float32
BF16 = jnp.bfloat16

D_MODEL = 1024
NORM_EPS = 1e-6
MLA_HEADS = 4
MLA_NOPE = 128
MLA_ROPE = 64
MLA_VDIM = 128
MLA_Q_LORA = 256
MLA_KV_LORA = 128
MLA_WIDTH = MLA_HEADS * MLA_VDIM
MLA_SCALE = (MLA_NOPE + MLA_ROPE) ** -0.5
ROPE_BASE = 10000.0
QK_WIDTH = 256
POOL_WINDOWS = (2, 4, 8, 16)
POOL_WIDTH = 256
POOL_GROUP_DIM = 64
POOL_HALO = 8
GLA_HEADS = 4
GLA_WIDTH = 256
GLA_KEY = 128
GLA_DK = 32
GLA_DV = 64
GLA_GATE_RANK = 16
GLA_GATE_NORM = 16.0
GLA_SUB = 32
GLA_TILE = 256
GLA_HALF = 128
GLA_TILES_PER_STEP = 4

IN_SIZES = (MLA_Q_LORA, MLA_KV_LORA, MLA_ROPE, MLA_WIDTH, POOL_WIDTH, POOL_WIDTH,
            GLA_KEY, GLA_KEY, GLA_WIDTH, GLA_GATE_RANK, GLA_GATE_RANK, GLA_WIDTH)
IN_OFFS = tuple(int(v) for v in np.cumsum((0,) + IN_SIZES))

ZGATE_WIDTH = MLA_WIDTH + POOL_WIDTH + GLA_WIDTH
ZGLA_WIDTH = 2 * GLA_KEY + GLA_WIDTH
GLA_LR_PAD = 128

ROW_TILE = 512
FINAL_ROW_TILE = 1024
ATTN_TQ = 512
ATTN_TK = 1024
ATTN_TILES_PER_STEP = 4
VMEM_LIMIT = 56 * 1024 * 1024


def _rms(x, g):
    return x * lax.rsqrt(jnp.mean(x * x, axis=-1, keepdims=True) + NORM_EPS) * g


def _dot(a, b):
    return jnp.dot(a, b, preferred_element_type=F32)


def _dot_nt(a, b):
    return lax.dot_general(a, b, (((1,), (1,)), ((), ())), preferred_element_type=F32)


N_PROJ_OPERANDS = 14
N_PROJ_OUTS = 7


def _proj_kernel(x_ref, *refs):
    _lockstep([_proj_body(_proj_norm(x_ref[0], refs[0]), *refs)])


def _proj_norm(x, ng_ref):
    return _rms(x, ng_ref[...]).astype(BF16)


def _proj_body(h, ng_ref, wa_ref, wgate_ref, wpu_ref, wgla_ref, qg_ref, wuqt_ref, kvg_ref, wuk_ref,
               wuvt_ref, gup_ref, gbias_ref, tab_ref, tabt_ref,
               qt_ref, k_ref, vt_ref, zgate_ref, upool_ref, zgla_ref, glog_ref):
    za = _dot(h, wa_ref[...])
    yield
    zgate_ref[0] = _dot(h, wgate_ref[...]).astype(BF16)
    yield
    upool_ref[0] = _dot(h, wpu_ref[...])
    yield
    zgla = _dot(h, wgla_ref[...])
    yield
    zgla_ref[0] = zgla[:, :ZGLA_WIDTH].astype(BF16)
    zg = _dot(zgla[:, ZGLA_WIDTH:].astype(BF16), gup_ref[...]) + gbias_ref[...]
    yield
    glog_ref[0] = (jnp.minimum(zg, 0.0) - jnp.log1p(jnp.exp(-jnp.abs(zg)))) * (1.0 / GLA_GATE_NORM)
    tab = tab_ref[...]
    tabt = tabt_ref[...]
    qscale = MLA_SCALE * math.log2(math.e)
    cqn = _rms(za[:, :MLA_Q_LORA], qg_ref[...]).astype(BF16)
    qall = _dot_nt(wuqt_ref[...], cqn)
    yield
    zeros = jnp.zeros((QK_WIDTH - MLA_NOPE - MLA_ROPE, qall.shape[1]), BF16)
    for hh in range(MLA_HEADS):
        base = hh * QK_WIDTH
        qn = qall[base:base + MLA_NOPE] * qscale
        t = qall[base + MLA_NOPE:base + QK_WIDTH] * tabt
        r = (t[:MLA_ROPE] + t[MLA_ROPE:]) * qscale
        qt_ref[0, hh, 0:MLA_NOPE] = qn.astype(BF16)
        qt_ref[0, hh, MLA_NOPE:MLA_NOPE + MLA_ROPE] = r.astype(BF16)
        qt_ref[0, hh, MLA_NOPE + MLA_ROPE:] = zeros
    ckvn = _rms(za[:, MLA_Q_LORA:MLA_Q_LORA + MLA_KV_LORA], kvg_ref[...]).astype(BF16)
    kn = _dot(ckvn, wuk_ref[...])
    yield
    t = za[:, MLA_Q_LORA + MLA_KV_LORA:] * tab
    kr = t + pltpu.roll(t, MLA_ROPE, 1)
    lane = lax.broadcasted_iota(jnp.int32, kr.shape, 1)
    kr = jnp.where(lane < MLA_ROPE, kr, 0.0).astype(BF16)
    for hh in range(MLA_HEADS):
        k_ref[0, hh, :, 0:MLA_NOPE] = kn[:, hh * MLA_NOPE:(hh + 1) * MLA_NOPE].astype(BF16)
        k_ref[0, hh, :, MLA_NOPE:QK_WIDTH] = kr
    vt = _dot_nt(wuvt_ref[...], ckvn)
    yield
    for hh in range(MLA_HEADS):
        vt_ref[0, hh] = vt[hh * MLA_VDIM:(hh + 1) * MLA_VDIM].astype(BF16)


def _full_spec(a):
    return pl.BlockSpec(a.shape, lambda *g: (0,) * a.ndim)


def _proj_operands(lw, tab, tm):
    weights = (lw["norm_g"], lw["wa"], lw["wgate"], lw["wpu"], lw["wgla"], lw["q_norm_g"], lw["wuqt"],
               lw["kv_norm_g"], lw["wuk"], lw["wuvt"], lw["gup"], lw["gbias"])
    specs = [_full_spec(w) for w in weights] + [pl.BlockSpec((tm, 128), lambda b, i: (i, 0)),
                                                pl.BlockSpec((128, tm), lambda b, i: (0, i))]
    operands = weights + tuple(tab)
    assert len(operands) == N_PROJ_OPERANDS
    return operands, specs


def _proj_outputs(B, S, tm):
    tok = lambda w: pl.BlockSpec((1, tm, w), lambda b, i: (b, i, 0))
    specs = [
        pl.BlockSpec((1, MLA_HEADS, QK_WIDTH, tm), lambda b, i: (b, 0, 0, i)),
        pl.BlockSpec((1, MLA_HEADS, tm, QK_WIDTH), lambda b, i: (b, 0, i, 0)),
        pl.BlockSpec((1, MLA_HEADS, MLA_VDIM, tm), lambda b, i: (b, 0, 0, i)),
        tok(ZGATE_WIDTH), tok(POOL_WIDTH), tok(ZGLA_WIDTH), tok(2 * GLA_KEY),
    ]
    shapes = [
        jax.ShapeDtypeStruct((B, MLA_HEADS, QK_WIDTH, S), BF16),
        jax.ShapeDtypeStruct((B, MLA_HEADS, S, QK_WIDTH), BF16),
        jax.ShapeDtypeStruct((B, MLA_HEADS, MLA_VDIM, S), BF16),
        jax.ShapeDtypeStruct((B, S, ZGATE_WIDTH), BF16),
        jax.ShapeDtypeStruct((B, S, POOL_WIDTH), F32),
        jax.ShapeDtypeStruct((B, S, ZGLA_WIDTH), BF16),
        jax.ShapeDtypeStruct((B, S, 2 * GLA_KEY), F32),
    ]
    assert len(specs) == N_PROJ_OUTS
    return specs, shapes


def _proj(x, lw, tab):
    B, S, _ = x.shape
    tm = min(ROW_TILE, S)
    operands, op_specs = _proj_operands(lw, tab, tm)
    out_specs, out_shape = _proj_outputs(B, S, tm)
    return pl.pallas_call(
        _proj_kernel,
        grid=(B, S // tm),
        in_specs=[pl.BlockSpec((1, tm, D_MODEL), lambda b, i: (b, i, 0))] + op_specs,
        out_specs=out_specs,
        out_shape=out_shape,
        compiler_params=pltpu.CompilerParams(
            dimension_semantics=("parallel", "parallel"), vmem_limit_bytes=VMEM_LIMIT),
        name="proj",
    )(x, *operands)


def _attn_kernel(qt_ref, qn_ref, k_ref, vt_ref, o_ref, s2_ref, mxa_ref, acc_ref, *, tk, nk):
    tq = qn_ref.shape[3]
    n_tiles = qt_ref.shape[3] // tq
    zero = jnp.minimum(pl.program_id(2), 0)
    sa_ref = s2_ref.at[zero]
    sb_ref = s2_ref.at[zero + 1]

    def scores(i, q, s_ref):
        q_ref, tile = q
        start = i * tk
        s = _dot(k_ref[0, 0, pl.ds(start, tk), :], q_ref[0, 0, :, tile * tq:(tile + 1) * tq])
        s_ref[:, 0:tq] = s
        return jnp.max(s, axis=0, keepdims=True)

    def update(i, s_ref, mx, m, l):
        start = i * tk
        m_new = jnp.maximum(m, mx)
        alpha = jnp.exp2(m - m_new)
        p = jnp.exp2(s_ref[:, 0:tq] - m_new)
        l = alpha * l + jnp.sum(p, axis=0, keepdims=True)
        acc_ref[...] = alpha * acc_ref[...] + _dot(vt_ref[0, 0, :, pl.ds(start, tk)], p.astype(BF16))
        return m_new, l

    @pl.when(pl.program_id(2) == 0)
    def _():
        mxa_ref[...] = scores(0, (qt_ref, 0), sa_ref)

    mx_a = mxa_ref[...]
    for tile in range(n_tiles):
        q = (qt_ref, tile)
        q_next = (qt_ref, tile + 1) if tile + 1 < n_tiles else (qn_ref, 0)
        m = jnp.full((1, tq), -jnp.inf, F32)
        l = jnp.zeros((1, tq), F32)
        acc_ref[...] = jnp.zeros_like(acc_ref)
        for i in range(0, nk, 2):
            mx_b = scores(i + 1, q, sb_ref)
            m, l = update(i, sa_ref, mx_a, m, l)
            mx_a = scores(i + 2, q, sa_ref) if i + 2 < nk else scores(0, q_next, sa_ref)
            m, l = update(i + 1, sb_ref, mx_b, m, l)
        o_ref[0, tile * tq:(tile + 1) * tq] = (acc_ref[...] / l).T.astype(o_ref.dtype)
    mxa_ref[...] = mx_a


def _attn(qt, k, vt):
    B, H, S, _ = k.shape
    tq = min(ATTN_TQ, S)
    tk = min(ATTN_TK, S // 2)
    nk = S // tk
    nq = S // tq
    per_step = min(ATTN_TILES_PER_STEP, nq)
    steps = nq // per_step
    assert nk % 2 == 0 and nq % per_step == 0
    return pl.pallas_call(
        functools.partial(_attn_kernel, tk=tk, nk=nk),
        grid=(B, H, steps),
        in_specs=[
            pl.BlockSpec((1, 1, QK_WIDTH, per_step * tq), lambda b, h, i: (b, h, 0, i)),
            pl.BlockSpec((1, 1, QK_WIDTH, tq), lambda b, h, i: (b, h, 0, jnp.minimum((i + 1) * per_step, nq - 1))),
            pl.BlockSpec((1, 1, S, QK_WIDTH), lambda b, h, i: (b, h, 0, 0)),
            pl.BlockSpec((1, 1, MLA_VDIM, S), lambda b, h, i: (b, h, 0, 0)),
        ],
        out_specs=pl.BlockSpec((1, per_step * tq, MLA_VDIM), lambda b, h, i: (b, i, h)),
        out_shape=jax.ShapeDtypeStruct((B, S, MLA_WIDTH), BF16),
        scratch_shapes=[pltpu.VMEM((2, tk, tq + 128), F32),
                        pltpu.VMEM((1, tq), F32), pltpu.VMEM((MLA_VDIM, tq), F32)],
        compiler_params=pltpu.CompilerParams(
            dimension_semantics=("parallel", "parallel", "arbitrary"), vmem_limit_bytes=VMEM_LIMIT),
        name="attn",
    )(qt, qt, k, vt)


def _split2(x):
    hi = x.astype(BF16)
    lo = (x - hi.astype(F32)).astype(BF16)
    return jnp.concatenate([hi, lo], axis=1)


def _sum2(y, w):
    return y[:, :w] + y[:, w:]


def _gla_prep(z, g, pin, pmid, pex):
    T = z.shape[0]
    nsub = T // GLA_SUB
    q = z[:, 0:GLA_KEY].astype(F32) * (GLA_DK ** -0.5)
    k = z[:, GLA_KEY:2 * GLA_KEY].astype(F32)
    g2 = _split2(g)
    b_in = _sum2(_dot(pin, g2), GLA_KEY)
    yield
    b_mid = _sum2(_dot(pmid, g2), GLA_KEY)
    yield
    b_ex = _sum2(_dot(pex, g2), GLA_KEY)
    yield
    qt = (q * jnp.exp(b_in)).astype(BF16)
    qa = (q * jnp.exp(b_in - b_mid)).astype(BF16)
    kt = (k * jnp.exp(b_mid - b_in)).astype(BF16)
    kp = k * jnp.exp(b_ex)
    dd = jnp.exp(b_in + b_ex)
    kpt = kp.T.astype(BF16)
    sub_of_lane = lax.broadcasted_iota(jnp.int32, (1, T), 1) // GLA_SUB
    stack = jnp.concatenate(
        [jnp.where(sub_of_lane == c, kpt, jnp.zeros_like(kpt)) for c in range(nsub)], axis=0)
    return qt, qa, kt, stack, dd.T


def _gla_local(z, g, pin, pmid, pex):
    qt, qa16, kt, stack, ddt = yield from _gla_prep(z, g, pin, pmid, pex)
    v = z[:, 2 * GLA_KEY:2 * GLA_KEY + GLA_WIDTH]
    T = v.shape[0]
    HT = GLA_HALF
    qhead = lax.broadcasted_iota(jnp.int32, (1, GLA_KEY), 1) // GLA_DK
    vhead = lax.broadcasted_iota(jnp.int32, (1, GLA_WIDTH), 1) // GLA_DV
    half_mask = pin[0:HT, 0:HT] > 0
    pmask = jnp.concatenate([half_mask, half_mask], axis=1)
    a = []
    for hf in range(T // HT):
        rows = slice(hf * HT, (hf + 1) * HT)
        for hp in range(GLA_HEADS // 2):
            kstack = jnp.concatenate([jnp.where(qhead == 2 * hp + e, kt[rows], jnp.zeros_like(kt[rows]))
                                      for e in range(2)], axis=0)
            a.append(jnp.where(pmask, _dot_nt(qa16[rows], kstack), 0.0).astype(BF16))
            yield

    inc = _dot(stack, v)
    yield

    o_halves = []
    for hf in range(T // HT):
        rows = slice(hf * HT, (hf + 1) * HT)
        o_half = jnp.zeros((HT, GLA_WIDTH), F32)
        for hp in range(GLA_HEADS // 2):
            vstack = jnp.concatenate([jnp.where(vhead == 2 * hp + e, v[rows], jnp.zeros_like(v[rows]))
                                      for e in range(2)], axis=0)
            o_half = o_half + _dot(a[hf * (GLA_HEADS // 2) + hp], vstack)
            yield
        o_halves.append(o_half)
    return jnp.concatenate(o_halves, axis=0), qt, ddt, inc


def _gla_scan(local, state, reverse):
    o, qt, ddt, inc = local
    nsub = qt.shape[0] // GLA_SUB
    same_head = ((lax.broadcasted_iota(jnp.int32, (GLA_KEY, 1), 0) // GLA_DK)
                 == (lax.broadcasted_iota(jnp.int32, (1, GLA_WIDTH), 1) // GLA_DV))
    inter = [None] * nsub
    order = range(nsub - 1, -1, -1) if reverse else range(nsub)
    for c in order:
        rows = slice(c * GLA_SUB, (c + 1) * GLA_SUB)
        inter[c] = _dot(qt[rows], state.astype(BF16))
        state = state * ddt[:, c * GLA_SUB:c * GLA_SUB + 1] + jnp.where(
            same_head, inc[c * GLA_KEY:(c + 1) * GLA_KEY], 0.0)
        yield
    return o + jnp.concatenate(inter, axis=0), state


def _lockstep(gens):
    results = [None] * len(gens)
    live = list(range(len(gens)))
    while live:
        for idx in list(live):
            try:
                next(gens[idx])
            except StopIteration as done:
                results[idx] = done.value
                live.remove(idx)
    return results


def _gla_kernel(zf_ref, zb_ref, gf_ref, gb_ref, pinf_ref, pmidf_ref, pexf_ref,
                pinb_ref, pmidb_ref, pexb_ref, of_ref, ob_ref, sf_ref, sb_ref):
    T = GLA_TILE if zf_ref.shape[1] >= GLA_TILE else zf_ref.shape[1]
    rounds = zf_ref.shape[1] // T
    masks_f = (pinf_ref[...], pmidf_ref[...], pexf_ref[...])
    masks_b = (pinb_ref[...], pmidb_ref[...], pexb_ref[...])

    @pl.when(pl.program_id(1) == 0)
    def _():
        sf_ref[...] = jnp.zeros_like(sf_ref)
        sb_ref[...] = jnp.zeros_like(sb_ref)

    def rows_of(r, reverse):
        t = rounds - 1 - r if reverse else r
        return slice(t * T, (t + 1) * T)

    def local_gens(r):
        rf, rb = rows_of(r, False), rows_of(r, True)
        return [_gla_local(zf_ref[0, rf], gf_ref[0, rf], *masks_f),
                _gla_local(zb_ref[0, rb], gb_ref[0, rb], *masks_b)]

    sf, sb = sf_ref[...], sb_ref[...]
    local_f, local_b = _lockstep(local_gens(0))
    for r in range(rounds):
        gens = [_gla_scan(local_f, sf, False), _gla_scan(local_b, sb, True)]
        if r + 1 < rounds:
            gens += local_gens(r + 1)
        results = _lockstep(gens)
        (of, sf), (ob, sb) = results[:2]
        of_ref[0, rows_of(r, False)] = of.astype(of_ref.dtype)
        ob_ref[0, rows_of(r, True)] = ob.astype(ob_ref.dtype)
        if r + 1 < rounds:
            local_f, local_b = results[2:]
    sf_ref[...] = sf
    sb_ref[...] = sb


def _gla_masks(T, reverse):
    t = np.arange(T)
    same = (t[:, None] // GLA_SUB) == (t[None, :] // GLA_SUB)
    before = (t[None, :] >= t[:, None]) if reverse else (t[None, :] <= t[:, None])
    pin = same & before
    pex = same & ~before
    in_sub = t[None, :] % GLA_SUB
    first_half = (in_sub >= GLA_SUB // 2) if reverse else (in_sub < GLA_SUB // 2)
    pmid = same & first_half
    return jnp.asarray(pin, BF16), jnp.asarray(pmid, BF16), jnp.asarray(pex, BF16)


def _gla(zgla, glog):
    B, S, _ = zgla.shape
    T = min(GLA_TILE, S)
    TB = min(GLA_TILES_PER_STEP * T, S)
    n = S // TB
    consts = _gla_masks(T, False) + _gla_masks(T, True)
    fwd = lambda b, i: (b, i, 0)
    bwd = lambda b, i: (b, n - 1 - i, 0)
    return pl.pallas_call(
        _gla_kernel,
        grid=(B, n),
        in_specs=[pl.BlockSpec((1, TB, ZGLA_WIDTH), fwd), pl.BlockSpec((1, TB, ZGLA_WIDTH), bwd),
                  pl.BlockSpec((1, TB, GLA_KEY), fwd), pl.BlockSpec((1, TB, GLA_KEY), lambda b, i: (b, n - 1 - i, 1))]
        + [_full_spec(c) for c in consts],
        out_specs=[pl.BlockSpec((1, TB, GLA_WIDTH), fwd), pl.BlockSpec((1, TB, GLA_WIDTH), bwd)],
        out_shape=[jax.ShapeDtypeStruct((B, S, GLA_WIDTH), BF16)] * 2,
        scratch_shapes=[pltpu.VMEM((GLA_KEY, GLA_WIDTH), F32)] * 2,
        compiler_params=pltpu.CompilerParams(
            dimension_semantics=("parallel", "arbitrary"), vmem_limit_bytes=VMEM_LIMIT),
        name="gla",
    )(zgla, zgla, glog, glog, *consts)


def _silu(x):
    h = 0.5 * x
    return h + h * jnp.tanh(h)


N_MIX_INPUTS = 13


def _mix_final_kernel(*refs, seq_len):
    ins, (fng_ref, out_ref), scratch = refs[:N_MIX_INPUTS], refs[N_MIX_INPUTS:N_MIX_INPUTS + 2], refs[-3:]
    (y,) = _lockstep([_mix_body(*ins, *scratch, seq_len=seq_len, tile=pl.program_id(1),
                                n_tiles=pl.num_programs(1))])
    out_ref[0] = _rms(y, fng_ref[...])


def _mix_proj_kernel(*refs, seq_len):
    ins, scratch = refs[:N_MIX_INPUTS], refs[-3:]
    proj_ops = refs[N_MIX_INPUTS:N_MIX_INPUTS + N_PROJ_OPERANDS]
    out_ref = refs[N_MIX_INPUTS + N_PROJ_OPERANDS]
    proj_outs = refs[N_MIX_INPUTS + N_PROJ_OPERANDS + 1:-3]
    (y,) = _lockstep([_mix_body(*ins, *scratch, seq_len=seq_len, tile=pl.program_id(1),
                                n_tiles=pl.num_programs(1))])
    out_ref[0] = y
    _lockstep([_proj_body(_proj_norm(y, proj_ops[0]), *proj_ops, *proj_outs)])


def _mix_body(x_ref, omla_ref, zgate_ref, u_ref, uprev_ref, unext_ref, gf_ref, gb_ref, wpool_ref, pscale_ref,
              gng_ref, hsum_ref, wout_ref, ubuf, lvl_a, lvl_b, *, seq_len, tile, n_tiles):
    tm = x_ref.shape[1]
    i = tile
    zg = zgate_ref[0].astype(F32)
    gate_mla = zg[:, 0:MLA_WIDTH]
    gate_pool = zg[:, MLA_WIDTH:MLA_WIDTH + POOL_WIDTH]
    gate_gla = zg[:, MLA_WIDTH + POOL_WIDTH:]
    u = u_ref[0]

    H = POOL_HALO
    lo, hi = H, tm + 3 * H
    pad = jnp.zeros((H, POOL_WIDTH), F32)
    for buf in (ubuf, lvl_a, lvl_b):
        buf[0:H] = pad[:, 0:buf.shape[1]]
        buf[hi:] = pad[:, 0:buf.shape[1]]
    ubuf[lo:2 * H] = jnp.where(i > 0, uprev_ref[0], 0.0)
    ubuf[2 * H:2 * H + tm] = u
    ubuf[2 * H + tm:hi] = jnp.where(i < n_tiles - 1, unext_ref[0], 0.0)

    def level(src, cols, before, after):
        return src[lo - before:hi - before, cols] + src[lo + after:hi + after, cols]

    tile_rows = slice(H, H + tm)
    all_cols = slice(0, 128)
    p2 = level(ubuf, slice(0, 128), 1, 0)
    lvl_a[lo:hi] = p2
    p4 = level(lvl_a, all_cols, 1, 1)
    sums = [(p2[tile_rows], p4[tile_rows])]
    yield
    p2 = level(ubuf, slice(128, 256), 1, 0)
    lvl_a[lo:hi] = p2
    lvl_b[lo:hi] = level(lvl_a, all_cols, 1, 1)
    p8 = level(lvl_b, all_cols, 2, 2)
    lvl_a[lo:hi] = p8
    p16 = level(lvl_a, all_cols, 4, 4)
    sums.append((p8[tile_rows], p16[tile_rows]))
    yield

    pos = i * tm + lax.broadcasted_iota(jnp.int32, (tm, 128), 0)
    lane = lax.broadcasted_iota(jnp.int32, (1, 128), 1)
    cnt = lambda w: (jnp.minimum(pos + w // 2, seq_len) - jnp.maximum(pos - w // 2, 0)).astype(F32)
    pooled = [jnp.where(lane < POOL_GROUP_DIM, small / cnt(ws), big / cnt(wb))
              for (small, big), (ws, wb) in zip(sums, ((2, 4), (8, 16)))]
    pooled = jnp.concatenate(pooled, axis=1) - u
    o_pool = _dot(pooled.astype(BF16), wpool_ref[...]) * pscale_ref[...]
    yield

    og = gf_ref[0].astype(F32) + gb_ref[0].astype(F32)
    sq = og * og
    sq_hi = sq.astype(BF16)
    sq_lo = (sq - sq_hi.astype(F32)).astype(BF16)
    ms = (_dot(sq_hi, hsum_ref[...]) + _dot(sq_lo, hsum_ref[...])) * (1.0 / GLA_DV)
    yield
    o_gla = og * lax.rsqrt(ms + NORM_EPS) * gng_ref[...]

    m_mla = (omla_ref[0].astype(F32) * _silu(gate_mla)).astype(BF16)
    y = x_ref[0] + _dot(m_mla, wout_ref[0:MLA_WIDTH])
    yield
    m_pool = (o_pool * _silu(gate_pool)).astype(BF16)
    y = y + _dot(m_pool, wout_ref[MLA_WIDTH:MLA_WIDTH + POOL_WIDTH])
    yield
    m_gla = (o_gla * _silu(gate_gla)).astype(BF16)
    y = y + _dot(m_gla, wout_ref[MLA_WIDTH + POOL_WIDTH:])
    yield
    return y


def _mix(x, omla, zgate, upool, gf, gb, lw, fng=None, next_lw=None, tab=None):
    B, S, _ = x.shape
    tm = min(ROW_TILE if next_lw is not None else FINAL_ROW_TILE, S)
    n = S // tm
    hb = tm // POOL_HALO
    nhalo = S // POOL_HALO
    tile = lambda w: pl.BlockSpec((1, tm, w), lambda b, i: (b, i, 0))
    weights = (lw["wpool"], lw["pool_scale"], lw["gla_norm_g"], lw["hsum"], lw["wout"])
    inputs = (x, omla, zgate, upool, upool, upool, gf, gb) + weights
    assert len(inputs) == N_MIX_INPUTS
    in_specs = [
        tile(D_MODEL), tile(MLA_WIDTH), tile(ZGATE_WIDTH), tile(POOL_WIDTH),
        pl.BlockSpec((1, POOL_HALO, POOL_WIDTH), lambda b, i: (b, jnp.maximum(i * hb - 1, 0), 0)),
        pl.BlockSpec((1, POOL_HALO, POOL_WIDTH), lambda b, i: (b, jnp.minimum((i + 1) * hb, nhalo - 1), 0)),
        tile(GLA_WIDTH), tile(GLA_WIDTH),
    ] + [_full_spec(w) for w in weights]
    out_specs = [tile(D_MODEL)]
    out_shape = [jax.ShapeDtypeStruct((B, S, D_MODEL), F32)]
    scratch = [pltpu.VMEM((tm + 4 * POOL_HALO, POOL_WIDTH), F32),
               pltpu.VMEM((tm + 4 * POOL_HALO, 128), F32),
               pltpu.VMEM((tm + 4 * POOL_HALO, 128), F32)]
    if next_lw is None:
        body, name = functools.partial(_mix_final_kernel, seq_len=S), "mix_final"
        inputs += (fng,)
        in_specs.append(_full_spec(fng))
    else:
        body, name = functools.partial(_mix_proj_kernel, seq_len=S), "mix_proj"
        operands, op_specs = _proj_operands(next_lw, tab, tm)
        inputs += operands
        in_specs += op_specs
        proj_specs, proj_shapes = _proj_outputs(B, S, tm)
        out_specs += proj_specs
        out_shape += proj_shapes
    return pl.pallas_call(
        body,
        grid=(B, n),
        in_specs=in_specs,
        out_specs=out_specs,
        out_shape=out_shape,
        scratch_shapes=scratch,
        compiler_params=pltpu.CompilerParams(
            dimension_semantics=("parallel", "parallel"), vmem_limit_bytes=VMEM_LIMIT),
        name=name,
    )(*inputs)


def _rope_table(seq):
    inv_freq = 1.0 / (ROPE_BASE ** (jnp.arange(0, MLA_ROPE, 2, dtype=F32) / MLA_ROPE))
    ang = jnp.arange(seq, dtype=F32)[:, None] * inv_freq[None, :]
    cos, sin = jnp.cos(ang), jnp.sin(ang)
    return jnp.concatenate([cos, cos, -sin, sin], axis=1)


def _swap_halves(w):
    half = w.shape[-1] // 2
    return jnp.concatenate([w[..., half:], w[..., :half]], axis=-1)


def _layer_weights(l, norm_g, w_in, q_norm_g, w_uq, kv_norm_g, w_ukv, w_pool, pool_scale,
                   gk_up_fwd, gk_bias_fwd, gk_up_bwd, gk_bias_bwd, gla_norm_g, w_out):
    cols = [w_in[l][:, IN_OFFS[j]:IN_OFFS[j + 1]] for j in range(len(IN_SIZES))]
    (c_q, c_kv, k_rope, gate_mla, u_pool, gate_pool, q_gla, k_gla, v_gla, lr_f, lr_b, gate_gla) = cols
    lr_pad = jnp.zeros((D_MODEL, GLA_LR_PAD - 2 * GLA_GATE_RANK), F32)
    wa = jnp.concatenate([c_q, c_kv, k_rope, _swap_halves(k_rope)], axis=1)
    wgate = jnp.concatenate([gate_mla, gate_pool, gate_gla], axis=1)
    wgla = jnp.concatenate([q_gla, k_gla, v_gla, lr_f, lr_b, lr_pad], axis=1)
    uq = w_uq[l].reshape(MLA_Q_LORA, MLA_HEADS, MLA_NOPE + MLA_ROPE)
    uq_rope = uq[..., MLA_NOPE:]
    wuq = jnp.concatenate([uq, _swap_halves(uq_rope)], axis=-1).reshape(MLA_Q_LORA, MLA_HEADS * QK_WIDTH)
    ukv = w_ukv[l].reshape(MLA_KV_LORA, MLA_HEADS, MLA_NOPE + MLA_VDIM)
    wuk = ukv[..., :MLA_NOPE].reshape(MLA_KV_LORA, MLA_HEADS * MLA_NOPE)
    wuvt = ukv[..., MLA_NOPE:].reshape(MLA_KV_LORA, MLA_HEADS * MLA_VDIM).T
    wpool = jnp.zeros((POOL_WIDTH, POOL_WIDTH), F32)
    for gi in range(len(POOL_WINDOWS)):
        sl = slice(gi * POOL_GROUP_DIM, (gi + 1) * POOL_GROUP_DIM)
        wpool = wpool.at[sl, sl].set(w_pool[l, gi])
    gup = jnp.zeros((GLA_LR_PAD, 2 * GLA_KEY), F32)
    gup = gup.at[:GLA_GATE_RANK, :GLA_KEY].set(gk_up_fwd[l])
    gup = gup.at[GLA_GATE_RANK:2 * GLA_GATE_RANK, GLA_KEY:].set(gk_up_bwd[l])
    gbias = jnp.concatenate([gk_bias_fwd[l], gk_bias_bwd[l]])[None, :]
    head = np.arange(GLA_WIDTH) // GLA_DV
    hsum = jnp.asarray(head[:, None] == head[None, :], BF16)
    return {
        "norm_g": norm_g[l][None, :],
        "wa": wa.astype(BF16), "wgate": wgate.astype(BF16), "wpu": u_pool.astype(BF16), "wgla": wgla.astype(BF16),
        "q_norm_g": q_norm_g[l][None, :], "wuqt": wuq.T.astype(BF16),
        "kv_norm_g": kv_norm_g[l][None, :], "wuk": wuk.astype(BF16), "wuvt": wuvt.astype(BF16),
        "wpool": wpool.astype(BF16), "pool_scale": pool_scale[l][None, :],
        "gup": gup.astype(BF16), "gbias": gbias,
        "gla_norm_g": jnp.tile(gla_norm_g[l], GLA_HEADS)[None, :], "hsum": hsum,
        "wout": w_out[l].astype(BF16),
    }


def _trunk(x, layers, fng):
    tab = _rope_table(x.shape[1])
    tab = (tab, tab.T)
    q, k, vt, zgate, upool, zgla, glog = _proj(x, layers[0], tab)
    for l, lw in enumerate(layers):
        omla = _attn(q, k, vt)
        gf, gb = _gla(zgla, glog)
        if l + 1 < len(layers):
            x, q, k, vt, zgate, upool, zgla, glog = _mix(x, omla, zgate, upool, gf, gb, lw,
                                                         next_lw=layers[l + 1], tab=tab)
        else:
            (x,) = _mix(x, omla, zgate, upool, gf, gb, lw, fng=fng)
    return x


def kernel(x_prompt, x_sample, norm_g, w_in, q_norm_g, w_uq, kv_norm_g, w_ukv, w_pool, pool_scale,
           gk_up_fwd, gk_bias_fwd, gk_up_bwd, gk_bias_bwd, gla_norm_g, w_out, final_norm_g):
    depth = w_in.shape[0]
    layers = [_layer_weights(l, norm_g, w_in, q_norm_g, w_uq, kv_norm_g, w_ukv, w_pool, pool_scale,
                             gk_up_fwd, gk_bias_fwd, gk_up_bwd, gk_bias_bwd, gla_norm_g, w_out)
              for l in range(depth)]
    fng = final_norm_g[None, :]
    return (_trunk(x_prompt, layers, fng), _trunk(x_sample, layers, fng))
```

```python
import functools
import math

import jax
import jax.numpy as jnp
import numpy as np
from jax import lax
from jax.experimental import pallas as pl
from jax.experimental.pallas import tpu as pltpu

F32 = jnp.float32
BF16 = jnp.bfloat16

D_MODEL = 1024
NORM_EPS = 1e-6
MLA_HEADS = 4
MLA_NOPE = 128
MLA_ROPE = 64
MLA_VDIM = 128
MLA_Q_LORA = 256
MLA_KV_LORA = 128
MLA_WIDTH = MLA_HEADS * MLA_VDIM
MLA_SCALE = (MLA_NOPE + MLA_ROPE) ** -0.5
ROPE_BASE = 10000.0
QK_WIDTH = 256
POOL_WINDOWS = (2, 4, 8, 16)
POOL_WIDTH = 256
POOL_GROUP_DIM = 64
POOL_HALO = 8
GLA_HEADS = 4
GLA_WIDTH = 256
GLA_KEY = 128
GLA_DK = 32
GLA_DV = 64
GLA_GATE_RANK = 16
GLA_GATE_NORM = 16.0
GLA_SUB = 32
GLA_TILE = 256
GLA_HALF = 128
GLA_TILES_PER_STEP = 4

IN_SIZES = (MLA_Q_LORA, MLA_KV_LORA, MLA_ROPE, MLA_WIDTH, POOL_WIDTH, POOL_WIDTH,
            GLA_KEY, GLA_KEY, GLA_WIDTH, GLA_GATE_RANK, GLA_GATE_RANK, GLA_WIDTH)
IN_OFFS = tuple(int(v) for v in np.cumsum((0,) + IN_SIZES))

ZGATE_WIDTH = MLA_WIDTH + POOL_WIDTH + GLA_WIDTH
ZGLA_WIDTH = 2 * GLA_KEY + GLA_WIDTH
GLA_LR_PAD = 128

ROW_TILE = 512
FINAL_ROW_TILE = 1024
ATTN_TQ = 512
ATTN_TK = 1024
ATTN_VT_ROWS = MLA_VDIM + 16
ATTN_TILES_PER_STEP = 4
VMEM_LIMIT = 56 * 1024 * 1024


def _rms(x, g):
    return x * lax.rsqrt(jnp.mean(x * x, axis=-1, keepdims=True) + NORM_EPS) * g


def _dot(a, b):
    return jnp.dot(a, b, preferred_element_type=F32)


def _dot_nt(a, b):
    return lax.dot_general(a, b, (((1,), (1,)), ((), ())), preferred_element_type=F32)


N_PROJ_OPERANDS = 14
N_PROJ_OUTS = 7


def _proj_kernel(x_ref, *refs):
    _lockstep([_proj_body(_proj_norm(x_ref[0], refs[0]), *refs)])


def _proj_norm(x, ng_ref):
    return _rms(x, ng_ref[...]).astype(BF16)


def _proj_body(h, ng_ref, wa_ref, wgate_ref, wpu_ref, wgla_ref, qg_ref, wuqt_ref, kvg_ref, wuk_ref,
               wuvt_ref, gup_ref, gbias_ref, tab_ref, tabt_ref,
               qt_ref, k_ref, vt_ref, zgate_ref, upool_ref, zgla_ref, glog_ref):
    za = _dot(h, wa_ref[...])
    yield
    zgate_ref[0] = _dot(h, wgate_ref[...]).astype(BF16)
    yield
    upool_ref[0] = _dot(h, wpu_ref[...])
    yield
    zgla = _dot(h, wgla_ref[...])
    yield
    zgla_ref[0] = zgla[:, :ZGLA_WIDTH].astype(BF16)
    zg = _dot(zgla[:, ZGLA_WIDTH:].astype(BF16), gup_ref[...]) + gbias_ref[...]
    yield
    glog_ref[0] = (jnp.minimum(zg, 0.0) - jnp.log1p(jnp.exp(-jnp.abs(zg)))) * (1.0 / GLA_GATE_NORM)
    tab = tab_ref[...]
    tabt = tabt_ref[...]
    qscale = MLA_SCALE * math.log2(math.e)
    cqn = _rms(za[:, :MLA_Q_LORA], qg_ref[...]).astype(BF16)
    qall = _dot_nt(wuqt_ref[...], cqn)
    yield
    zeros = jnp.zeros((QK_WIDTH - MLA_NOPE - MLA_ROPE, qall.shape[1]), BF16)
    for hh in range(MLA_HEADS):
        base = hh * QK_WIDTH
        qn = qall[base:base + MLA_NOPE] * qscale
        t = qall[base + MLA_NOPE:base + QK_WIDTH] * tabt
        r = (t[:MLA_ROPE] + t[MLA_ROPE:]) * qscale
        qt_ref[0, hh, 0:MLA_NOPE] = qn.astype(BF16)
        qt_ref[0, hh, MLA_NOPE:MLA_NOPE + MLA_ROPE] = r.astype(BF16)
        qt_ref[0, hh, MLA_NOPE + MLA_ROPE:] = zeros
    ckvn = _rms(za[:, MLA_Q_LORA:MLA_Q_LORA + MLA_KV_LORA], kvg_ref[...]).astype(BF16)
    kn = _dot(ckvn, wuk_ref[...])
    yield
    t = za[:, MLA_Q_LORA + MLA_KV_LORA:] * tab
    kr = t + pltpu.roll(t, MLA_ROPE, 1)
    lane = lax.broadcasted_iota(jnp.int32, kr.shape, 1)
    kr = jnp.where(lane < MLA_ROPE, kr, 0.0).astype(BF16)
    for hh in range(MLA_HEADS):
        k_ref[0, hh, :, 0:MLA_NOPE] = kn[:, hh * MLA_NOPE:(hh + 1) * MLA_NOPE].astype(BF16)
        k_ref[0, hh, :, MLA_NOPE:QK_WIDTH] = kr
    vt = _dot_nt(wuvt_ref[...], ckvn)
    yield
    for hh in range(MLA_HEADS):
        vt_ref[0, hh, 0:MLA_VDIM] = vt[hh * MLA_VDIM:(hh + 1) * MLA_VDIM].astype(BF16)
        vt_ref[0, hh, MLA_VDIM:] = jnp.ones((ATTN_VT_ROWS - MLA_VDIM, vt.shape[1]), BF16)


def _full_spec(a):
    return pl.BlockSpec(a.shape, lambda *g: (0,) * a.ndim)


def _proj_operands(lw, tab, tm):
    weights = (lw["norm_g"], lw["wa"], lw["wgate"], lw["wpu"], lw["wgla"], lw["q_norm_g"], lw["wuqt"],
               lw["kv_norm_g"], lw["wuk"], lw["wuvt"], lw["gup"], lw["gbias"])
    specs = [_full_spec(w) for w in weights] + [pl.BlockSpec((tm, 128), lambda b, i: (i, 0)),
                                                pl.BlockSpec((128, tm), lambda b, i: (0, i))]
    operands = weights + tuple(tab)
    assert len(operands) == N_PROJ_OPERANDS
    return operands, specs


def _proj_outputs(B, S, tm):
    tok = lambda w: pl.BlockSpec((1, tm, w), lambda b, i: (b, i, 0))
    specs = [
        pl.BlockSpec((1, MLA_HEADS, QK_WIDTH, tm), lambda b, i: (b, 0, 0, i)),
        pl.BlockSpec((1, MLA_HEADS, tm, QK_WIDTH), lambda b, i: (b, 0, i, 0)),
        pl.BlockSpec((1, MLA_HEADS, ATTN_VT_ROWS, tm), lambda b, i: (b, 0, 0, i)),
        tok(ZGATE_WIDTH), tok(POOL_WIDTH), tok(ZGLA_WIDTH), tok(2 * GLA_KEY),
    ]
    shapes = [
        jax.ShapeDtypeStruct((B, MLA_HEADS, QK_WIDTH, S), BF16),
        jax.ShapeDtypeStruct((B, MLA_HEADS, S, QK_WIDTH), BF16),
        jax.ShapeDtypeStruct((B, MLA_HEADS, ATTN_VT_ROWS, S), BF16),
        jax.ShapeDtypeStruct((B, S, ZGATE_WIDTH), BF16),
        jax.ShapeDtypeStruct((B, S, POOL_WIDTH), F32),
        jax.ShapeDtypeStruct((B, S, ZGLA_WIDTH), BF16),
        jax.ShapeDtypeStruct((B, S, 2 * GLA_KEY), F32),
    ]
    assert len(specs) == N_PROJ_OUTS
    return specs, shapes


def _proj(x, lw, tab):
    B, S, _ = x.shape
    tm = min(ROW_TILE, S)
    operands, op_specs = _proj_operands(lw, tab, tm)
    out_specs, out_shape = _proj_outputs(B, S, tm)
    return pl.pallas_call(
        _proj_kernel,
        grid=(B, S // tm),
        in_specs=[pl.BlockSpec((1, tm, D_MODEL), lambda b, i: (b, i, 0))] + op_specs,
        out_specs=out_specs,
        out_shape=out_shape,
        compiler_params=pltpu.CompilerParams(
            dimension_semantics=("parallel", "parallel"), vmem_limit_bytes=VMEM_LIMIT),
        name="proj",
    )(x, *operands)


def _attn_kernel(qt_ref, qn_ref, k_ref, vt_ref, o_ref, s2_ref, mxa_ref, acc_ref, *, tk, nk):
    tq = qn_ref.shape[3]
    n_tiles = qt_ref.shape[3] // tq
    zero = jnp.minimum(pl.program_id(2), 0)
    sa_ref = s2_ref.at[zero]
    sb_ref = s2_ref.at[zero + 1]

    def scores(i, q, s_ref):
        q_ref, tile = q
        start = i * tk
        s = _dot(k_ref[0, 0, pl.ds(start, tk), :], q_ref[0, 0, :, tile * tq:(tile + 1) * tq])
        s_ref[:, 0:tq] = s
        return jnp.max(s, axis=0, keepdims=True)

    def update(i, s_ref, mx, m, l):
        start = i * tk
        m_new = jnp.maximum(m, mx)
        alpha = jnp.exp2(m - m_new)
        p = jnp.exp2(s_ref[:, 0:tq] - m_new).astype(BF16)
        acc_ref[...] = alpha * acc_ref[...] + _dot(vt_ref[0, 0, :, pl.ds(start, tk)], p)
        return m_new, l

    @pl.when(pl.program_id(2) == 0)
    def _():
        mxa_ref[...] = scores(0, (qt_ref, 0), sa_ref)

    mx_a = mxa_ref[...]
    for tile in range(n_tiles):
        q = (qt_ref, tile)
        q_next = (qt_ref, tile + 1) if tile + 1 < n_tiles else (qn_ref, 0)
        m = jnp.full((1, tq), -jnp.inf, F32)
        l = jnp.zeros((1, tq), F32)
        acc_ref[...] = jnp.zeros_like(acc_ref)
        for i in range(0, nk, 2):
            mx_b = scores(i + 1, q, sb_ref)
            m, l = update(i, sa_ref, mx_a, m, l)
            mx_a = scores(i + 2, q, sa_ref) if i + 2 < nk else scores(0, q_next, sa_ref)
            m, l = update(i + 1, sb_ref, mx_b, m, l)
        o_ref[0, tile * tq:(tile + 1) * tq] = (
            acc_ref[0:MLA_VDIM] / acc_ref[MLA_VDIM:MLA_VDIM + 1]).T.astype(o_ref.dtype)
    mxa_ref[...] = mx_a


def _attn(qt, k, vt):
    B, H, S, _ = k.shape
    tq = min(ATTN_TQ, S)
    tk = min(ATTN_TK, S // 2)
    nk = S // tk
    nq = S // tq
    per_step = min(ATTN_TILES_PER_STEP, nq)
    steps = nq // per_step
    assert nk % 2 == 0 and nq % per_step == 0
    return pl.pallas_call(
        functools.partial(_attn_kernel, tk=tk, nk=nk),
        grid=(B, H, steps),
        in_specs=[
            pl.BlockSpec((1, 1, QK_WIDTH, per_step * tq), lambda b, h, i: (b, h, 0, i)),
            pl.BlockSpec((1, 1, QK_WIDTH, tq), lambda b, h, i: (b, h, 0, jnp.minimum((i + 1) * per_step, nq - 1))),
            pl.BlockSpec((1, 1, S, QK_WIDTH), lambda b, h, i: (b, h, 0, 0)),
            pl.BlockSpec((1, 1, ATTN_VT_ROWS, S), lambda b, h, i: (b, h, 0, 0)),
        ],
        out_specs=pl.BlockSpec((1, per_step * tq, MLA_VDIM), lambda b, h, i: (b, i, h)),
        out_shape=jax.ShapeDtypeStruct((B, S, MLA_WIDTH), BF16),
        scratch_shapes=[pltpu.VMEM((2, tk, tq + 128), F32),
                        pltpu.VMEM((1, tq), F32), pltpu.VMEM((ATTN_VT_ROWS, tq), F32)],
        compiler_params=pltpu.CompilerParams(
            dimension_semantics=("parallel", "parallel", "arbitrary"), vmem_limit_bytes=VMEM_LIMIT),
        name="attn",
    )(qt, qt, k, vt)


def _split2(x):
    hi = x.astype(BF16)
    lo = (x - hi.astype(F32)).astype(BF16)
    return jnp.concatenate([hi, lo], axis=1)


def _sum2(y, w):
    return y[:, :w] + y[:, w:]


def _gla_prep(z, g, pin, pmid, pex):
    T = z.shape[0]
    nsub = T // GLA_SUB
    q = z[:, 0:GLA_KEY].astype(F32) * (GLA_DK ** -0.5)
    k = z[:, GLA_KEY:2 * GLA_KEY].astype(F32)
    g2 = _split2(g)
    b_in = _sum2(_dot(pin, g2), GLA_KEY)
    yield
    b_mid = _sum2(_dot(pmid, g2), GLA_KEY)
    yield
    b_ex = _sum2(_dot(pex, g2), GLA_KEY)
    yield
    qt = (q * jnp.exp(b_in)).astype(BF16)
    qa = (q * jnp.exp(b_in - b_mid)).astype(BF16)
    kt = (k * jnp.exp(b_mid - b_in)).astype(BF16)
    kp = k * jnp.exp(b_ex)
    dd = jnp.exp(b_in + b_ex)
    kpt = kp.T.astype(BF16)
    sub_of_lane = lax.broadcasted_iota(jnp.int32, (1, T), 1) // GLA_SUB
    stack = jnp.concatenate(
        [jnp.where(sub_of_lane == c, kpt, jnp.zeros_like(kpt)) for c in range(nsub)], axis=0)
    return qt, qa, kt, stack, dd.T


def _gla_local(z, g, pin, pmid, pex):
    qt, qa16, kt, stack, ddt = yield from _gla_prep(z, g, pin, pmid, pex)
    v = z[:, 2 * GLA_KEY:2 * GLA_KEY + GLA_WIDTH]
    T = v.shape[0]
    HT = GLA_HALF
    qhead = lax.broadcasted_iota(jnp.int32, (1, GLA_KEY), 1) // GLA_DK
    vhead = lax.broadcasted_iota(jnp.int32, (1, GLA_WIDTH), 1) // GLA_DV
    half_mask = pin[0:HT, 0:HT] > 0
    pmask = jnp.concatenate([half_mask, half_mask], axis=1)
    a = []
    for hf in range(T // HT):
        rows = slice(hf * HT, (hf + 1) * HT)
        for hp in range(GLA_HEADS // 2):
            kstack = jnp.concatenate([jnp.where(qhead == 2 * hp + e, kt[rows], jnp.zeros_like(kt[rows]))
                                      for e in range(2)], axis=0)
            a.append(jnp.where(pmask, _dot_nt(qa16[rows], kstack), 0.0).astype(BF16))
            yield

    inc = _dot(stack, v)
    yield

    o_halves = []
    for hf in range(T // HT):
        rows = slice(hf * HT, (hf + 1) * HT)
        o_half = jnp.zeros((HT, GLA_WIDTH), F32)
        for hp in range(GLA_HEADS // 2):
            vstack = jnp.concatenate([jnp.where(vhead == 2 * hp + e, v[rows], jnp.zeros_like(v[rows]))
                                      for e in range(2)], axis=0)
            o_half = o_half + _dot(a[hf * (GLA_HEADS // 2) + hp], vstack)
            yield
        o_halves.append(o_half)
    return jnp.concatenate(o_halves, axis=0), qt, ddt, inc


def _gla_scan(local, state, reverse):
    o, qt, ddt, inc = local
    nsub = qt.shape[0] // GLA_SUB
    same_head = ((lax.broadcasted_iota(jnp.int32, (GLA_KEY, 1), 0) // GLA_DK)
                 == (lax.broadcasted_iota(jnp.int32, (1, GLA_WIDTH), 1) // GLA_DV))
    inter = [None] * nsub
    order = range(nsub - 1, -1, -1) if reverse else range(nsub)
    for c in order:
        rows = slice(c * GLA_SUB, (c + 1) * GLA_SUB)
        inter[c] = _dot(qt[rows], state.astype(BF16))
        state = state * ddt[:, c * GLA_SUB:c * GLA_SUB + 1] + jnp.where(
            same_head, inc[c * GLA_KEY:(c + 1) * GLA_KEY], 0.0)
        yield
    return o + jnp.concatenate(inter, axis=0), state


def _lockstep(gens):
    results = [None] * len(gens)
    live = list(range(len(gens)))
    while live:
        for idx in list(live):
            try:
                next(gens[idx])
            except StopIteration as done:
                results[idx] = done.value
                live.remove(idx)
    return results


def _gla_kernel(zf_ref, zb_ref, gf_ref, gb_ref, pinf_ref, pmidf_ref, pexf_ref,
                pinb_ref, pmidb_ref, pexb_ref, of_ref, ob_ref, sf_ref, sb_ref):
    T = GLA_TILE if zf_ref.shape[1] >= GLA_TILE else zf_ref.shape[1]
    rounds = zf_ref.shape[1] // T
    masks_f = (pinf_ref[...], pmidf_ref[...], pexf_ref[...])
    masks_b = (pinb_ref[...], pmidb_ref[...], pexb_ref[...])

    @pl.when(pl.program_id(1) == 0)
    def _():
        sf_ref[...] = jnp.zeros_like(sf_ref)
        sb_ref[...] = jnp.zeros_like(sb_ref)

    def rows_of(r, reverse):
        t = rounds - 1 - r if reverse else r
        return slice(t * T, (t + 1) * T)

    def local_gens(r):
        rf, rb = rows_of(r, False), rows_of(r, True)
        return [_gla_local(zf_ref[0, rf], gf_ref[0, rf], *masks_f),
                _gla_local(zb_ref[0, rb], gb_ref[0, rb], *masks_b)]

    sf, sb = sf_ref[...], sb_ref[...]
    local_f, local_b = _lockstep(local_gens(0))
    for r in range(rounds):
        gens = [_gla_scan(local_f, sf, False), _gla_scan(local_b, sb, True)]
        if r + 1 < rounds:
            gens += local_gens(r + 1)
        results = _lockstep(gens)
        (of, sf), (ob, sb) = results[:2]
        of_ref[0, rows_of(r, False)] = of.astype(of_ref.dtype)
        ob_ref[0, rows_of(r, True)] = ob.astype(ob_ref.dtype)
        if r + 1 < rounds:
            local_f, local_b = results[2:]
    sf_ref[...] = sf
    sb_ref[...] = sb


def _gla_masks(T, reverse):
    t = np.arange(T)
    same = (t[:, None] // GLA_SUB) == (t[None, :] // GLA_SUB)
    before = (t[None, :] >= t[:, None]) if reverse else (t[None, :] <= t[:, None])
    pin = same & before
    pex = same & ~before
    in_sub = t[None, :] % GLA_SUB
    first_half = (in_sub >= GLA_SUB // 2) if reverse else (in_sub < GLA_SUB // 2)
    pmid = same & first_half
    return jnp.asarray(pin, BF16), jnp.asarray(pmid, BF16), jnp.asarray(pex, BF16)


def _gla(zgla, glog):
    B, S, _ = zgla.shape
    T = min(GLA_TILE, S)
    TB = min(GLA_TILES_PER_STEP * T, S)
    n = S // TB
    consts = _gla_masks(T, False) + _gla_masks(T, True)
    fwd = lambda b, i: (b, i, 0)
    bwd = lambda b, i: (b, n - 1 - i, 0)
    return pl.pallas_call(
        _gla_kernel,
        grid=(B, n),
        in_specs=[pl.BlockSpec((1, TB, ZGLA_WIDTH), fwd), pl.BlockSpec((1, TB, ZGLA_WIDTH), bwd),
                  pl.BlockSpec((1, TB, GLA_KEY), fwd), pl.BlockSpec((1, TB, GLA_KEY), lambda b, i: (b, n - 1 - i, 1))]
        + [_full_spec(c) for c in consts],
        out_specs=[pl.BlockSpec((1, TB, GLA_WIDTH), fwd), pl.BlockSpec((1, TB, GLA_WIDTH), bwd)],
        out_shape=[jax.ShapeDtypeStruct((B, S, GLA_WIDTH), BF16)] * 2,
        scratch_shapes=[pltpu.VMEM((GLA_KEY, GLA_WIDTH), F32)] * 2,
        compiler_params=pltpu.CompilerParams(
            dimension_semantics=("parallel", "arbitrary"), vmem_limit_bytes=VMEM_LIMIT),
        name="gla",
    )(zgla, zgla, glog, glog, *consts)


def _silu(x):
    h = 0.5 * x
    return h + h * jnp.tanh(h)


N_MIX_INPUTS = 13


def _mix_final_kernel(*refs, seq_len):
    ins, (fng_ref, out_ref), scratch = refs[:N_MIX_INPUTS], refs[N_MIX_INPUTS:N_MIX_INPUTS + 2], refs[-3:]
    (y,) = _lockstep([_mix_body(*ins, *scratch, seq_len=seq_len, tile=pl.program_id(1),
                                n_tiles=pl.num_programs(1))])
    out_ref[0] = _rms(y, fng_ref[...])


def _mix_proj_kernel(*refs, seq_len):
    ins, scratch = refs[:N_MIX_INPUTS], refs[-3:]
    proj_ops = refs[N_MIX_INPUTS:N_MIX_INPUTS + N_PROJ_OPERANDS]
    out_ref = refs[N_MIX_INPUTS + N_PROJ_OPERANDS]
    proj_outs = refs[N_MIX_INPUTS + N_PROJ_OPERANDS + 1:-3]
    (y,) = _lockstep([_mix_body(*ins, *scratch, seq_len=seq_len, tile=pl.program_id(1),
                                n_tiles=pl.num_programs(1))])
    out_ref[0] = y
    _lockstep([_proj_body(_proj_norm(y, proj_ops[0]), *proj_ops, *proj_outs)])


def _mix_body(x_ref, omla_ref, zgate_ref, u_ref, uprev_ref, unext_ref, gf_ref, gb_ref, wpool_ref, pscale_ref,
              gng_ref, hsum_ref, wout_ref, ubuf, lvl_a, lvl_b, *, seq_len, tile, n_tiles):
    tm = x_ref.shape[1]
    i = tile
    zg = zgate_ref[0].astype(F32)
    gate_mla = zg[:, 0:MLA_WIDTH]
    gate_pool = zg[:, MLA_WIDTH:MLA_WIDTH + POOL_WIDTH]
    gate_gla = zg[:, MLA_WIDTH + POOL_WIDTH:]
    u = u_ref[0]

    H = POOL_HALO
    lo, hi = H, tm + 3 * H
    pad = jnp.zeros((H, POOL_WIDTH), F32)
    for buf in (ubuf, lvl_a, lvl_b):
        buf[0:H] = pad[:, 0:buf.shape[1]]
        buf[hi:] = pad[:, 0:buf.shape[1]]
    ubuf[lo:2 * H] = jnp.where(i > 0, uprev_ref[0], 0.0)
    ubuf[2 * H:2 * H + tm] = u
    ubuf[2 * H + tm:hi] = jnp.where(i < n_tiles - 1, unext_ref[0], 0.0)

    def level(src, cols, before, after):
        return src[lo - before:hi - before, cols] + src[lo + after:hi + after, cols]

    tile_rows = slice(H, H + tm)
    all_cols = slice(0, 128)
    p2 = level(ubuf, slice(0, 128), 1, 0)
    lvl_a[lo:hi] = p2
    p4 = level(lvl_a, all_cols, 1, 1)
    sums = [(p2[tile_rows], p4[tile_rows])]
    yield
    p2 = level(ubuf, slice(128, 256), 1, 0)
    lvl_a[lo:hi] = p2
    lvl_b[lo:hi] = level(lvl_a, all_cols, 1, 1)
    p8 = level(lvl_b, all_cols, 2, 2)
    lvl_a[lo:hi] = p8
    p16 = level(lvl_a, all_cols, 4, 4)
    sums.append((p8[tile_rows], p16[tile_rows]))
    yield

    pos = i * tm + lax.broadcasted_iota(jnp.int32, (tm, 128), 0)
    lane = lax.broadcasted_iota(jnp.int32, (1, 128), 1)
    cnt = lambda w: (jnp.minimum(pos + w // 2, seq_len) - jnp.maximum(pos - w // 2, 0)).astype(F32)
    pooled = [jnp.where(lane < POOL_GROUP_DIM, small / cnt(ws), big / cnt(wb))
              for (small, big), (ws, wb) in zip(sums, ((2, 4), (8, 16)))]
    pooled = jnp.concatenate(pooled, axis=1) - u
    o_pool = _dot(pooled.astype(BF16), wpool_ref[...]) * pscale_ref[...]
    yield

    og = gf_ref[0].astype(F32) + gb_ref[0].astype(F32)
    sq = og * og
    sq_hi = sq.astype(BF16)
    sq_lo = (sq - sq_hi.astype(F32)).astype(BF16)
    ms = (_dot(sq_hi, hsum_ref[...]) + _dot(sq_lo, hsum_ref[...])) * (1.0 / GLA_DV)
    yield
    o_gla = og * lax.rsqrt(ms + NORM_EPS) * gng_ref[...]

    m_mla = (omla_ref[0].astype(F32) * _silu(gate_mla)).astype(BF16)
    y = x_ref[0] + _dot(m_mla, wout_ref[0:MLA_WIDTH])
    yield
    m_pool = (o_pool * _silu(gate_pool)).astype(BF16)
    y = y + _dot(m_pool, wout_ref[MLA_WIDTH:MLA_WIDTH + POOL_WIDTH])
    yield
    m_gla = (o_gla * _silu(gate_gla)).astype(BF16)
    y = y + _dot(m_gla, wout_ref[MLA_WIDTH + POOL_WIDTH:])
    yield
    return y


def _mix(x, omla, zgate, upool, gf, gb, lw, fng=None, next_lw=None, tab=None):
    B, S, _ = x.shape
    tm = min(ROW_TILE if next_lw is not None else FINAL_ROW_TILE, S)
    n = S // tm
    hb = tm // POOL_HALO
    nhalo = S // POOL_HALO
    tile = lambda w: pl.BlockSpec((1, tm, w), lambda b, i: (b, i, 0))
    weights = (lw["wpool"], lw["pool_scale"], lw["gla_norm_g"], lw["hsum"], lw["wout"])
    inputs = (x, omla, zgate, upool, upool, upool, gf, gb) + weights
    assert len(inputs) == N_MIX_INPUTS
    in_specs = [
        tile(D_MODEL), tile(MLA_WIDTH), tile(ZGATE_WIDTH), tile(POOL_WIDTH),
        pl.BlockSpec((1, POOL_HALO, POOL_WIDTH), lambda b, i: (b, jnp.maximum(i * hb - 1, 0), 0)),
        pl.BlockSpec((1, POOL_HALO, POOL_WIDTH), lambda b, i: (b, jnp.minimum((i + 1) * hb, nhalo - 1), 0)),
        tile(GLA_WIDTH), tile(GLA_WIDTH),
    ] + [_full_spec(w) for w in weights]
    out_specs = [tile(D_MODEL)]
    out_shape = [jax.ShapeDtypeStruct((B, S, D_MODEL), F32)]
    scratch = [pltpu.VMEM((tm + 4 * POOL_HALO, POOL_WIDTH), F32),
               pltpu.VMEM((tm + 4 * POOL_HALO, 128), F32),
               pltpu.VMEM((tm + 4 * POOL_HALO, 128), F32)]
    if next_lw is None:
        body, name = functools.partial(_mix_final_kernel, seq_len=S), "mix_final"
        inputs += (fng,)
        in_specs.append(_full_spec(fng))
    else:
        body, name = functools.partial(_mix_proj_kernel, seq_len=S), "mix_proj"
        operands, op_specs = _proj_operands(next_lw, tab, tm)
        inputs += operands
        in_specs += op_specs
        proj_specs, proj_shapes = _proj_outputs(B, S, tm)
        out_specs += proj_specs
        out_shape += proj_shapes
    return pl.pallas_call(
        body,
        grid=(B, n),
        in_specs=in_specs,
        out_specs=out_specs,
        out_shape=out_shape,
        scratch_shapes=scratch,
        compiler_params=pltpu.CompilerParams(
            dimension_semantics=("parallel", "parallel"), vmem_limit_bytes=VMEM_LIMIT),
        name=name,
    )(*inputs)


def _rope_table(seq):
    inv_freq = 1.0 / (ROPE_BASE ** (jnp.arange(0, MLA_ROPE, 2, dtype=F32) / MLA_ROPE))
    ang = jnp.arange(seq, dtype=F32)[:, None] * inv_freq[None, :]
    cos, sin = jnp.cos(ang), jnp.sin(ang)
    return jnp.concatenate([cos, cos, -sin, sin], axis=1)


def _swap_halves(w):
    half = w.shape[-1] // 2
    return jnp.concatenate([w[..., half:], w[..., :half]], axis=-1)


def _layer_weights(l, norm_g, w_in, q_norm_g, w_uq, kv_norm_g, w_ukv, w_pool, pool_scale,
                   gk_up_fwd, gk_bias_fwd, gk_up_bwd, gk_bias_bwd, gla_norm_g, w_out):
    cols = [w_in[l][:, IN_OFFS[j]:IN_OFFS[j + 1]] for j in range(len(IN_SIZES))]
    (c_q, c_kv, k_rope, gate_mla, u_pool, gate_pool, q_gla, k_gla, v_gla, lr_f, lr_b, gate_gla) = cols
    lr_pad = jnp.zeros((D_MODEL, GLA_LR_PAD - 2 * GLA_GATE_RANK), F32)
    wa = jnp.concatenate([c_q, c_kv, k_rope, _swap_halves(k_rope)], axis=1)
    wgate = jnp.concatenate([gate_mla, gate_pool, gate_gla], axis=1)
    wgla = jnp.concatenate([q_gla, k_gla, v_gla, lr_f, lr_b, lr_pad], axis=1)
    uq = w_uq[l].reshape(MLA_Q_LORA, MLA_HEADS, MLA_NOPE + MLA_ROPE)
    uq_rope = uq[..., MLA_NOPE:]
    wuq = jnp.concatenate([uq, _swap_halves(uq_rope)], axis=-1).reshape(MLA_Q_LORA, MLA_HEADS * QK_WIDTH)
    ukv = w_ukv[l].reshape(MLA_KV_LORA, MLA_HEADS, MLA_NOPE + MLA_VDIM)
    wuk = ukv[..., :MLA_NOPE].reshape(MLA_KV_LORA, MLA_HEADS * MLA_NOPE)
    wuvt = ukv[..., MLA_NOPE:].reshape(MLA_KV_LORA, MLA_HEADS * MLA_VDIM).T
    wpool = jnp.zeros((POOL_WIDTH, POOL_WIDTH), F32)
    for gi in range(len(POOL_WINDOWS)):
        sl = slice(gi * POOL_GROUP_DIM, (gi + 1) * POOL_GROUP_DIM)
        wpool = wpool.at[sl, sl].set(w_pool[l, gi])
    gup = jnp.zeros((GLA_LR_PAD, 2 * GLA_KEY), F32)
    gup = gup.at[:GLA_GATE_RANK, :GLA_KEY].set(gk_up_fwd[l])
    gup = gup.at[GLA_GATE_RANK:2 * GLA_GATE_RANK, GLA_KEY:].set(gk_up_bwd[l])
    gbias = jnp.concatenate([gk_bias_fwd[l], gk_bias_bwd[l]])[None, :]
    head = np.arange(GLA_WIDTH) // GLA_DV
    hsum = jnp.asarray(head[:, None] == head[None, :], BF16)
    return {
        "norm_g": norm_g[l][None, :],
        "wa": wa.astype(BF16), "wgate": wgate.astype(BF16), "wpu": u_pool.astype(BF16), "wgla": wgla.astype(BF16),
        "q_norm_g": q_norm_g[l][None, :], "wuqt": wuq.T.astype(BF16),
        "kv_norm_g": kv_norm_g[l][None, :], "wuk": wuk.astype(BF16), "wuvt": wuvt.astype(BF16),
        "wpool": wpool.astype(BF16), "pool_scale": pool_scale[l][None, :],
        "gup": gup.astype(BF16), "gbias": gbias,
        "gla_norm_g": jnp.tile(gla_norm_g[l], GLA_HEADS)[None, :], "hsum": hsum,
        "wout": w_out[l].astype(BF16),
    }


def _trunk(x, layers, fng):
    tab = _rope_table(x.shape[1])
    tab = (tab, tab.T)
    q, k, vt, zgate, upool, zgla, glog = _proj(x, layers[0], tab)
    for l, lw in enumerate(layers):
        omla = _attn(q, k, vt)
        gf, gb = _gla(zgla, glog)
        if l + 1 < len(layers):
            x, q, k, vt, zgate, upool, zgla, glog = _mix(x, omla, zgate, upool, gf, gb, lw,
                                                         next_lw=layers[l + 1], tab=tab)
        else:
            (x,) = _mix(x, omla, zgate, upool, gf, gb, lw, fng=fng)
    return x


def kernel(x_prompt, x_sample, norm_g, w_in, q_norm_g, w_uq, kv_norm_g, w_ukv, w_pool, pool_scale,
           gk_up_fwd, gk_bias_fwd, gk_up_bwd, gk_bias_bwd, gla_norm_g, w_out, final_norm_g):
    depth = w_in.shape[0]
    layers = [_layer_weights(l, norm_g, w_in, q_norm_g, w_uq, kv_norm_g, w_ukv, w_pool, pool_scale,
                             gk_up_fwd, gk_bias_fwd, gk_up_bwd, gk_bias_bwd, gla_norm_g, w_out)
              for l in range(depth)]
    fng = final_norm_g[None, :]
    return (_trunk(x_prompt, layers, fng), _trunk(x_sample, layers, fng))
```
